```python
import math
import jax
import jax.numpy as jnp
from jax import lax
import numpy as np

D_MODEL = 2048
BATCH = 4
SEQ = 2048
DEPTH = 4

CTX_LEN = 256
GRID_W = 64
NORM_EPS = 1e-6

GM_CHUNK = 128
GM_GROUPS = 4
GM_GROUP_CH = 128
GM_WIDTH = GM_GROUPS * GM_GROUP_CH
GDN_HEADS = 4
GDN_HEAD_DIM = 128
GDN_CONV = 5
GDN_CHUNK = 64
RET_HEADS = 4
RET_KEY_DIM = 128
RET_VAL_DIM = 256
RET_CHUNK = 64
ROPE_BASE = 10000.0
MOE_GROUPS = 4
MOE_EXPERTS_PER_GROUP = 8
MOE_EXPERTS = MOE_GROUPS * MOE_EXPERTS_PER_GROUP
MOE_TOP_K = 2
MOE_HIDDEN = 512
MOE_BLOCK = 128

GM_UV_COLS = 2 * GM_WIDTH
GDN_QKV_COLS = 3 * GDN_HEADS * GDN_HEAD_DIM
GDN_AB_COLS = 2 * GDN_HEADS
GDN_GATE_COLS = GDN_HEADS * GDN_HEAD_DIM
RET_QK_COLS = RET_HEADS * RET_KEY_DIM
RET_V_COLS = RET_HEADS * RET_VAL_DIM
MERGE_COLS = 3 * D_MODEL
REC_START = GM_UV_COLS
REC_SIZES = (GDN_QKV_COLS, GDN_AB_COLS, GDN_AB_COLS, RET_QK_COLS, RET_QK_COLS, RET_V_COLS)
REC_END = REC_START + sum(REC_SIZES)
POST_SIZES = (GDN_GATE_COLS, RET_V_COLS, MERGE_COLS)
IN_COLS = REC_END + sum(POST_SIZES)

kernel_name = 'hybrid_gated_branch_dit_block'


def split_cols(t, sizes):
    idx, acc = [], 0
    for s in sizes[:-1]:
        acc += s
        idx.append(acc)
    return jnp.split(t, idx, axis=-1)


def rms_norm(t, w):
    tf = t.astype(jnp.float32)
    y = tf * lax.rsqrt(jnp.mean(tf * tf, axis=-1, keepdims=True) + NORM_EPS)
    return (y * w).astype(t.dtype)


def standardize(t):
    tf = t.astype(jnp.float32)
    mu = jnp.mean(tf, axis=-1, keepdims=True)
    d = tf - mu
    return d * lax.rsqrt(jnp.mean(d * d, axis=-1, keepdims=True) + NORM_EPS)


def l2_normalize(t):
    return t * lax.rsqrt(jnp.sum(t * t, axis=-1, keepdims=True) + NORM_EPS)


def modulate(h, shift, scale):
    return h * (1.0 + scale) + shift


def flip(t):
    return jnp.flip(t, axis=1)


def centred_dwconv(t, w):
    K = w.shape[0]
    r = K // 2
    L = t.shape[1]
    tp = jnp.pad(t, ((0, 0), (r, r), (0, 0)))
    out = tp[:, 0:L] * w[0]
    for j in range(1, K):
        out = out + tp[:, j:j + L] * w[j]
    return out


def axial_rope(rows, dim):
    n = dim // 4
    freq = ROPE_BASE ** (-jnp.arange(n, dtype=jnp.float32) / n)
    row = jnp.repeat(jnp.arange(rows, dtype=jnp.float32), GRID_W)
    col = (jnp.arange(rows * GRID_W) % GRID_W).astype(jnp.float32)
    ang = jnp.concatenate([row[:, None] * freq, col[:, None] * freq], axis=-1)
    ang = jnp.concatenate([ang, ang], axis=-1)[:, None, :]
    return jnp.cos(ang), jnp.sin(ang)


def apply_rope(t, cos, sin):
    t1, t2 = jnp.split(t, 2, axis=-1)
    return t * cos + jnp.concatenate([-t2, t1], axis=-1) * sin


def to_chunks(t, size):
    B, L, H = t.shape[:3]
    return jnp.moveaxis(t.reshape(B, L // size, size, H, -1), 3, 2)


def from_chunks(o):
    N, B, H, C, d = o.shape
    return jnp.moveaxis(jnp.moveaxis(o, 0, 1), 2, 3).reshape(B, N * C, H, d)


def gated_delta_chunked(q, k, v, g, beta, state0):
    Dk = q.shape[-1]
    Dv = v.shape[-1]
    C = GDN_CHUNK
    qc = to_chunks(q * Dk ** -0.5, C)
    kc = to_chunks(k, C)
    vc = to_chunks(v, C)
    bc = to_chunks(beta[..., None], C)
    gc = jnp.cumsum(to_chunks(g[..., None], C)[..., 0], axis=-1)
    causal = jnp.tril(jnp.ones((C, C), bool))
    strict = jnp.tril(jnp.ones((C, C), bool), -1)
    decay = jnp.exp(jnp.where(causal, gc[..., :, None] - gc[..., None, :], -jnp.inf))
    kb = kc * bc
    a_mat = jnp.where(strict, jnp.einsum('bnhik,bnhjk->bnhij', kb, kc) * decay, 0.0) + jnp.eye(C, dtype=kc.dtype)
    rhs = jnp.concatenate([vc * bc, kb * jnp.exp(gc)[..., None]], axis=-1)
    sol = lax.linalg.triangular_solve(a_mat, rhs, left_side=True, lower=True, unit_diagonal=True)
    u, w = sol[..., :Dv], sol[..., Dv:]
    qk = jnp.einsum('bnhik,bnhjk->bnhij', qc, kc) * decay

    def step(S, xs):
        q_i, k_i, u_i, w_i, g_i, qk_i = xs
        v_new = u_i - jnp.einsum('bhck,bhkv->bhcv', w_i, S)
        o_i = (jnp.einsum('bhck,bhkv->bhcv', q_i * jnp.exp(g_i)[..., None], S)
               + jnp.einsum('bhij,bhjv->bhiv', qk_i, v_new))
        g_last = g_i[..., -1:]
        S = (S * jnp.exp(g_last)[..., None]
             + jnp.einsum('bhck,bhcv->bhkv', k_i * jnp.exp(g_last - g_i)[..., None], v_new))
        return S, o_i

    xs = tuple(jnp.moveaxis(t, 1, 0) for t in (qc, kc, u, w, gc, qk))
    S, o = lax.scan(step, state0, xs)
    return from_chunks(o), S


def retention_chunked(q, k, v, log_gamma, state0):
    C = RET_CHUNK
    qc, kc, vc = to_chunks(q, C), to_chunks(k, C), to_chunks(v, C)
    pos = jnp.arange(C, dtype=jnp.float32)
    diff = pos[:, None] - pos[None, :]
    lg = log_gamma[:, None, None]
    decay = jnp.exp(jnp.where(diff >= 0, lg * diff, -jnp.inf))
    cross_decay = jnp.exp(log_gamma[:, None] * (pos + 1.0))[..., None]
    in_decay = jnp.exp(log_gamma[:, None] * (C - 1.0 - pos))[..., None]
    chunk_decay = jnp.exp(log_gamma * C)[:, None, None]
    inner = jnp.einsum('bnhij,bnhjv->bnhiv', jnp.einsum('bnhik,bnhjk->bnhij', qc, kc) * decay, vc)

    def step(S, xs):
        q_i, k_i, v_i = xs
        cross = jnp.einsum('bhck,bhkv->bhcv', q_i, S) * cross_decay
        S = S * chunk_decay + jnp.einsum('bhck,bhcv->bhkv', k_i * in_decay, v_i)
        return S, cross

    S, cross = lax.scan(step, state0, tuple(jnp.moveaxis(t, 1, 0) for t in (qc, kc, vc)))
    return from_chunks(jnp.moveaxis(inner, 1, 0) + cross), S


def recurrent_passes(rec, conv_w, a_log, dt_bias, log_gamma, rope, init):
    gdn_qkv, gdn_a, gdn_b, ret_q, ret_k, ret_v = rec
    B, L, _ = gdn_qkv.shape
    f32 = jnp.float32
    qkv = jax.nn.silu(centred_dwconv(gdn_qkv, conv_w)).astype(f32)
    gq, gk, gv = [t.reshape(B, L, GDN_HEADS, GDN_HEAD_DIM) for t in jnp.split(qkv, 3, axis=-1)]
    gq, gk = l2_normalize(gq), l2_normalize(gk)
    a = gdn_a.astype(f32).reshape(B, L, 2, GDN_HEADS)
    log_decay = -jnp.exp(a_log.astype(f32)) * jax.nn.softplus(a + dt_bias.astype(f32))
    beta = jax.nn.sigmoid(gdn_b.astype(f32).reshape(B, L, 2, GDN_HEADS))
    o_gf, s_gf = gated_delta_chunked(gq, gk, gv, log_decay[:, :, 0], beta[:, :, 0], init[0])
    o_gb, s_gb = gated_delta_chunked(flip(gq), flip(gk), flip(gv), flip(log_decay[:, :, 1]),
                                     flip(beta[:, :, 1]), init[1])
    o_gdn = o_gf + flip(o_gb)

    rq = ret_q.astype(f32).reshape(B, L, RET_HEADS, RET_KEY_DIM)
    rk = ret_k.astype(f32).reshape(B, L, RET_HEADS, RET_KEY_DIM)
    rv = ret_v.astype(f32).reshape(B, L, RET_HEADS, RET_VAL_DIM)
    if rope is not None:
        rq, rk = apply_rope(rq, *rope), apply_rope(rk, *rope)
    rq = rq * RET_KEY_DIM ** -0.5
    o_rf, s_rf = retention_chunked(rq, rk, rv, log_gamma[0], init[2])
    o_rb, s_rb = retention_chunked(flip(rq), flip(rk), flip(rv), log_gamma[1], init[3])
    return o_gdn, o_rf + flip(o_rb), (s_gf, s_gb, s_rf, s_rb)


def chunk_spatial_gating(uv, norm_w, w_s, b_s):
    B, L, _ = uv.shape
    u, v = jnp.split(jax.nn.gelu(uv, approximate=False), 2, axis=-1)
    v = (standardize(v) * norm_w).astype(uv.dtype)
    v = v.reshape(B, L // GM_CHUNK, GM_CHUNK, GM_GROUPS, GM_GROUP_CH)
    s = jnp.einsum('gpq,bnqgc->bnpgc', w_s, v) + b_s.T[:, :, None]
    return u * s.reshape(B, L, GM_WIDTH)


def merge_branches(p, o_gdn, o_ret, gm_norm_w, gm_w, gm_b, gdn_norm_w, ret_norm_w,
                   w_br_gm, w_br_gdn, w_br_ret, w_out):
    B, L, _ = p.shape
    dt = p.dtype
    uv = p[..., :REC_START]
    gdn_gate, ret_gate, merge_logits = split_cols(p[..., REC_END:], POST_SIZES)
    y_gm = chunk_spatial_gating(uv, gm_norm_w, gm_w, gm_b) @ w_br_gm
    og = o_gdn * lax.rsqrt(jnp.mean(o_gdn * o_gdn, axis=-1, keepdims=True) + NORM_EPS) * gdn_norm_w
    y_gdn = (og.reshape(B, L, -1).astype(dt) * jax.nn.silu(gdn_gate)) @ w_br_gdn
    orr = standardize(o_ret).reshape(B, L, -1) * ret_norm_w
    y_ret = (orr.astype(dt) * jax.nn.silu(ret_gate)) @ w_br_ret
    s_gm, s_gdn, s_ret = jnp.split(jax.nn.sigmoid(merge_logits), 3, axis=-1)
    return (s_gm * y_gm + s_gdn * y_gdn + s_ret * y_ret) @ w_out


def routed_experts(h, expert_idx, weights, w_gate, w_up, w_down):
    T, D = h.shape
    K = expert_idx.shape[1]
    E = w_gate.shape[0]
    A = T * K
    n_blocks = -(-A // MOE_BLOCK) + E
    n_slots = n_blocks * MOE_BLOCK
    flat_e = expert_idx.reshape(A)
    order = jnp.argsort(flat_e)
    e_sorted = flat_e[order]
    tok_sorted = (order // K).astype(jnp.int32)
    w_sorted = weights.reshape(A)[order]
    counts = jnp.bincount(flat_e, length=E)
    padded = (counts + MOE_BLOCK - 1) // MOE_BLOCK * MOE_BLOCK
    pad_end = jnp.cumsum(padded)
    dest = (pad_end - padded)[e_sorted] + jnp.arange(A) - (jnp.cumsum(counts) - counts)[e_sorted]
    slot_tok = jnp.full((n_slots,), T, jnp.int32).at[dest].set(tok_sorted)
    slot_w = jnp.zeros((n_slots,), weights.dtype).at[dest].set(w_sorted)
    block_expert = jnp.minimum(jnp.searchsorted(pad_end, jnp.arange(n_blocks) * MOE_BLOCK, side='right'), E - 1)
    h_pad = jnp.concatenate([h, jnp.zeros((1, D), h.dtype)], axis=0)

    def expert_block(args):
        tok, e = args
        xb = h_pad[tok]
        act = jax.nn.silu(xb @ w_gate[e]) * (xb @ w_up[e])
        return act @ w_down[e]

    y = lax.map(expert_block, (slot_tok.reshape(n_blocks, MOE_BLOCK), block_expert))
    y = y.reshape(n_slots, D) * slot_w[:, None].astype(y.dtype)
    return jnp.zeros((T + 1, D), y.dtype).at[slot_tok].add(y)[:T]


def hier_moe(h, wg, bg, we, be, w_gate, w_up, w_down):
    T, D = h.shape
    hf = h.astype(jnp.float32)
    p_group = jax.nn.softmax(hf @ wg.astype(jnp.float32) + bg.astype(jnp.float32), axis=-1)
    pg_top, grp = lax.top_k(p_group, 1)
    logits_e = (hf @ we.astype(jnp.float32) + be.astype(jnp.float32)).reshape(T, MOE_GROUPS, MOE_EXPERTS_PER_GROUP)
    sel = jnp.broadcast_to(grp[:, :, None], (T, 1, MOE_EXPERTS_PER_GROUP))
    p_e = jax.nn.softmax(jnp.take_along_axis(logits_e, sel, axis=1)[:, 0], axis=-1)
    pe_top, e_top = lax.top_k(p_e, MOE_TOP_K)
    weights = pg_top * pe_top / jnp.sum(pe_top, axis=-1, keepdims=True)
    expert_idx = grp * MOE_EXPERTS_PER_GROUP + e_top
    return routed_experts(h, expert_idx, weights, w_gate, w_up, w_down)


def setup_inputs(seed: int = 0) -> dict:
    key = jax.random.key(seed)
    ks = iter(jax.random.split(key, 40))
    f32 = jnp.float32

    def nrm(shape, scale):
        return jax.random.normal(next(ks), shape, f32) * scale

    x = nrm((BATCH, SEQ, D_MODEL), 1.0)
    c = nrm((BATCH, D_MODEL), 1.0)
    ctx = nrm((BATCH, CTX_LEN, D_MODEL), 1.0)
    c_ctx = nrm((D_MODEL,), 1.0)
    mod_w = nrm((DEPTH, D_MODEL, 6 * D_MODEL), 0.5 * D_MODEL ** -0.5)
    mod_b = nrm((DEPTH, 6 * D_MODEL), 0.02)
    norm1_w = 1.0 + nrm((DEPTH, D_MODEL), 0.02)
    w_in = nrm((DEPTH, D_MODEL, IN_COLS), D_MODEL ** -0.5)
    gm_norm_w = 1.0 + nrm((DEPTH, GM_WIDTH), 0.02)
    gm_spatial_w = nrm((DEPTH, GM_GROUPS, GM_CHUNK, GM_CHUNK), GM_CHUNK ** -0.5)
    gm_spatial_b = 1.0 + nrm((DEPTH, GM_GROUPS, GM_CHUNK), 0.02)
    gdn_conv_w = nrm((DEPTH, GDN_CONV, GDN_QKV_COLS), GDN_CONV ** -0.5)
    gdn_a_log = jnp.log(jax.random.uniform(next(ks), (DEPTH, 2, GDN_HEADS), f32, 1.0, 16.0))
    dt = jnp.exp(jax.random.uniform(next(ks), (DEPTH, 2, GDN_HEADS), f32, math.log(1e-3), math.log(1e-1)))
    gdn_dt_bias = dt + jnp.log(-jnp.expm1(-dt))
    gdn_norm_w = 1.0 + nrm((DEPTH, GDN_HEAD_DIM), 0.02)
    gamma0 = 1.0 - 2.0 ** (-5.0 - jnp.arange(RET_HEADS, dtype=f32))
    ret_decay_logit = jnp.log(gamma0 / (1.0 - gamma0)) + nrm((DEPTH, 2, RET_HEADS), 0.1)
    ret_norm_w = 1.0 + nrm((DEPTH, RET_V_COLS), 0.02)
    w_br_gm = nrm((DEPTH, GM_WIDTH, D_MODEL), GM_WIDTH ** -0.5)
    w_br_gdn = nrm((DEPTH, GDN_GATE_COLS, D_MODEL), GDN_GATE_COLS ** -0.5)
    w_br_ret = nrm((DEPTH, RET_V_COLS, D_MODEL), RET_V_COLS ** -0.5)
    w_out = nrm((DEPTH, D_MODEL, D_MODEL), D_MODEL ** -0.5)
    norm2_w = 1.0 + nrm((DEPTH, D_MODEL), 0.02)
    router_group_w = nrm((DEPTH, D_MODEL, MOE_GROUPS), D_MODEL ** -0.5)
    router_group_b = nrm((DEPTH, MOE_GROUPS), 0.01)
    router_expert_w = nrm((DEPTH, D_MODEL, MOE_EXPERTS), D_MODEL ** -0.5)
    router_expert_b = nrm((DEPTH, MOE_EXPERTS), 0.01)
    moe_w_gate = nrm((DEPTH, MOE_EXPERTS, D_MODEL, MOE_HIDDEN), D_MODEL ** -0.5)
    moe_w_up = nrm((DEPTH, MOE_EXPERTS, D_MODEL, MOE_HIDDEN), D_MODEL ** -0.5)
    moe_w_down = nrm((DEPTH, MOE_EXPERTS, MOE_HIDDEN, D_MODEL), MOE_HIDDEN ** -0.5)
    final_norm_w = 1.0 + nrm((D_MODEL,), 0.02)
    return {'x': x, 'c': c, 'ctx': ctx, 'c_ctx': c_ctx, 'mod_w': mod_w, 'mod_b': mod_b,
            'norm1_w': norm1_w, 'w_in': w_in, 'gm_norm_w': gm_norm_w, 'gm_spatial_w': gm_spatial_w,
            'gm_spatial_b': gm_spatial_b, 'gdn_conv_w': gdn_conv_w, 'gdn_a_log': gdn_a_log,
            'gdn_dt_bias': gdn_dt_bias, 'gdn_norm_w': gdn_norm_w, 'ret_decay_logit': ret_decay_logit,
            'ret_norm_w': ret_norm_w, 'w_br_gm': w_br_gm, 'w_br_gdn': w_br_gdn, 'w_br_ret': w_br_ret,
            'w_out': w_out, 'norm2_w': norm2_w, 'router_group_w': router_group_w,
            'router_group_b': router_group_b, 'router_expert_w': router_expert_w,
            'router_expert_b': router_expert_b, 'moe_w_gate': moe_w_gate, 'moe_w_up': moe_w_up,
            'moe_w_down': moe_w_down, 'final_norm_w': final_norm_w}


def reference(x, c, ctx, c_ctx, mod_w, mod_b, norm1_w, w_in, gm_norm_w, gm_spatial_w, gm_spatial_b,
              gdn_conv_w, gdn_a_log, gdn_dt_bias, gdn_norm_w, ret_decay_logit, ret_norm_w,
              w_br_gm, w_br_gdn, w_br_ret, w_out, norm2_w, router_group_w, router_group_b,
              router_expert_w, router_expert_b, moe_w_gate, moe_w_up, moe_w_down, final_norm_w):
    B, L, D = x.shape
    rows = L // GRID_W
    rope = axial_rope(rows, RET_KEY_DIM)
    gdn_zero = jnp.zeros((B, GDN_HEADS, GDN_HEAD_DIM, GDN_HEAD_DIM), jnp.float32)
    ret_zero = jnp.zeros((B, RET_HEADS, RET_KEY_DIM, RET_VAL_DIM), jnp.float32)
    zero_states = (gdn_zero, gdn_zero, ret_zero, ret_zero)
    s = ctx
    for l in range(DEPTH):
        last = l == DEPTH - 1
        mod = jax.nn.silu(c) @ mod_w[l] + mod_b[l]
        mod_c = jax.nn.silu(c_ctx) @ mod_w[l] + mod_b[l]
        sh1, sc1, gt1, sh2, sc2, gt2 = jnp.split(mod[:, None, :], 6, axis=-1)
        csh1, csc1, cgt1, csh2, csc2, cgt2 = jnp.split(mod_c[None, None, :], 6, axis=-1)
        log_gamma = jax.nn.log_sigmoid(ret_decay_logit[l].astype(jnp.float32))
        rec_params = (gdn_conv_w[l], gdn_a_log[l], gdn_dt_bias[l], log_gamma)
        merge_params = (gm_norm_w[l], gm_spatial_w[l], gm_spatial_b[l], gdn_norm_w[l], ret_norm_w[l],
                        w_br_gm[l], w_br_gdn[l], w_br_ret[l], w_out[l])

        hc = modulate(rms_norm(s, norm1_w[l]), csh1, csc1)
        if last:
            rec_c = split_cols(hc @ w_in[l, :, REC_START:REC_END], REC_SIZES)
        else:
            pc = hc @ w_in[l]
            rec_c = split_cols(pc[..., REC_START:REC_END], REC_SIZES)
        o_gdn_c, o_ret_c, ctx_states = recurrent_passes(rec_c, *rec_params, None, zero_states)

        h = modulate(rms_norm(x, norm1_w[l]), sh1, sc1)
        p = h @ w_in[l]
        o_gdn, o_ret, _ = recurrent_passes(split_cols(p[..., REC_START:REC_END], REC_SIZES),
                                           *rec_params, rope, ctx_states)
        x = x + gt1 * merge_branches(p, o_gdn, o_ret, *merge_params)
        if not last:
            s = s + cgt1 * merge_branches(pc, o_gdn_c, o_ret_c, *merge_params)

        moe_params = (router_group_w[l], router_group_b[l], router_expert_w[l], router_expert_b[l],
                      moe_w_gate[l], moe_w_up[l], moe_w_down[l])
        h2 = modulate(rms_norm(x, norm2_w[l]), sh2, sc2).reshape(B * L, D)
        if last:
            x = x + gt2 * hier_moe(h2, *moe_params).reshape(B, L, D)
        else:
            h2c = modulate(rms_norm(s, norm2_w[l]), csh2, csc2).reshape(-1, D)
            y2 = hier_moe(jnp.concatenate([h2, h2c], axis=0), *moe_params)
            x = x + gt2 * y2[:B * L].reshape(B, L, D)
            s = s + cgt2 * y2[B * L:].reshape(B, -1, D)
    return rms_norm(x, final_norm_w)
```

```python
import functools
import math

import jax
import jax.numpy as jnp
from jax import lax
from jax.experimental import pallas as pl
from jax.experimental.pallas import tpu as pltpu

D_MODEL = 2048
CTX_LEN = 256
GRID_W = 64
NORM_EPS = 1e-6

GM_CHUNK = 128
GM_GROUPS = 4
GM_GROUP_CH = 128
GM_WIDTH = GM_GROUPS * GM_GROUP_CH
GDN_HEADS = 4
GDN_HEAD_DIM = 128
GDN_CHUNK = 64
RET_HEADS = 4
RET_KEY_DIM = 128
RET_VAL_DIM = 256
RET_CHUNK = 64
ROPE_BASE = 10000.0
MOE_GROUPS = 4
MOE_EXPERTS_PER_GROUP = 8
MOE_EXPERTS = MOE_GROUPS * MOE_EXPERTS_PER_GROUP
MOE_TOP_K = 2
MOE_HIDDEN = 512

GM_UV_COLS = 2 * GM_WIDTH
GDN_QKV_COLS = 3 * GDN_HEADS * GDN_HEAD_DIM
GDN_AB_COLS = 2 * GDN_HEADS
GDN_GATE_COLS = GDN_HEADS * GDN_HEAD_DIM
RET_QK_COLS = RET_HEADS * RET_KEY_DIM
RET_V_COLS = RET_HEADS * RET_VAL_DIM
MERGE_COLS = 3 * D_MODEL
AB_START = GM_UV_COLS + GDN_QKV_COLS
AB_END = AB_START + 2 * GDN_AB_COLS

C_UV = 0
C_QKV = C_UV + GM_UV_COLS
C_RQ = C_QKV + GDN_QKV_COLS
C_RK = C_RQ + RET_QK_COLS
C_RV = C_RK + RET_QK_COLS
C_GG = C_RV + RET_V_COLS
C_RG = C_GG + GDN_GATE_COLS
C_ML = C_RG + RET_V_COLS
P_COLS = C_ML + MERGE_COLS

V7X_VMEM_LIMIT_BYTES = 56 * 1024 * 1024
MOE_ROWS = 256
MERGE_ROWS = 256
NORM_ROWS = 256
PROJ_COLS = 512


def _cparams(*sem):
    return pltpu.CompilerParams(dimension_semantics=sem, vmem_limit_bytes=V7X_VMEM_LIMIT_BYTES)


def _normmod_kernel(z_ref, nw_ref, sh_ref, sc_ref, o_ref):
    z = z_ref[0]
    y = z * lax.rsqrt(jnp.mean(z * z, axis=-1, keepdims=True) + NORM_EPS)
    y = y * nw_ref[...]
    o_ref[0] = (y * (1.0 + sc_ref[0, 0]) + sh_ref[0, 0]).astype(o_ref.dtype)


def _normmod(z, norm_w, shift, scale, out_dtype):
    B, LT, D = z.shape
    seg = lambda b, t: (b, jnp.minimum(t, 1), 0, 0)
    assert CTX_LEN == NORM_ROWS
    return pl.pallas_call(
        _normmod_kernel,
        out_shape=jax.ShapeDtypeStruct((B, LT, D), out_dtype),
        grid=(B, LT // NORM_ROWS),
        in_specs=[
            pl.BlockSpec((1, NORM_ROWS, D), lambda b, t: (b, t, 0)),
            pl.BlockSpec((1, D), lambda b, t: (0, 0)),
            pl.BlockSpec((1, 1, 1, D), seg),
            pl.BlockSpec((1, 1, 1, D), seg),
        ],
        out_specs=pl.BlockSpec((1, NORM_ROWS, D), lambda b, t: (b, t, 0)),
        compiler_params=_cparams("parallel", "parallel"),
        name="normmod",
    )(z, norm_w.reshape(1, D), shift, scale)


def _final_norm_kernel(z_ref, nw_ref, o_ref):
    z = z_ref[0]
    o_ref[0] = z * lax.rsqrt(jnp.mean(z * z, axis=-1, keepdims=True) + NORM_EPS) * nw_ref[...]


def _final_norm(z, norm_w):
    B, LT, D = z.shape
    skip = CTX_LEN // NORM_ROWS
    return pl.pallas_call(
        _final_norm_kernel,
        out_shape=jax.ShapeDtypeStruct((B, LT - CTX_LEN, D), jnp.float32),
        grid=(B, (LT - CTX_LEN) // NORM_ROWS),
        in_specs=[
            pl.BlockSpec((1, NORM_ROWS, D), lambda b, t: (b, t + skip, 0)),
            pl.BlockSpec((1, D), lambda b, t: (0, 0)),
        ],
        out_specs=pl.BlockSpec((1, NORM_ROWS, D), lambda b, t: (b, t, 0)),
        compiler_params=_cparams("parallel", "parallel"),
        name="final_norm",
    )(z, norm_w.reshape(1, D))


def _mm_kernel(a_ref, w_ref, o_ref):
    o_ref[...] = jnp.dot(a_ref[...], w_ref[...].astype(jnp.bfloat16), preferred_element_type=jnp.float32)


def _matmul(a, w, tm, tn, name):
    M, K = a.shape
    N = w.shape[1]
    return pl.pallas_call(
        _mm_kernel,
        out_shape=jax.ShapeDtypeStruct((M, N), jnp.float32),
        grid=(M // tm, N // tn),
        in_specs=[
            pl.BlockSpec((tm, K), lambda i, j: (i, 0)),
            pl.BlockSpec((K, tn), lambda i, j: (0, j)),
        ],
        out_specs=pl.BlockSpec((tm, tn), lambda i, j: (i, j)),
        compiler_params=_cparams("parallel", "arbitrary"),
        name=name,
    )(a, w)


def _resid_mm_kernel(a_ref, w_ref, x_ref, gl_ref, gc_ref, o_ref):
    acc = jnp.dot(a_ref[...], w_ref[...], preferred_element_type=jnp.float32)
    row = lax.broadcasted_iota(jnp.int32, (acc.shape[0], 1), 0)
    gate = jnp.where(row < CTX_LEN, gc_ref[...], gl_ref[0])
    o_ref[...] = x_ref[...] + gate * acc


def _resid_matmul(a, w, x, gate_lat, gate_ctx, LT):
    M, K = a.shape
    N = w.shape[1]
    B = M // LT
    tn = PROJ_COLS
    return pl.pallas_call(
        _resid_mm_kernel,
        out_shape=jax.ShapeDtypeStruct((M, N), jnp.float32),
        grid=(B, N // tn),
        in_specs=[
            pl.BlockSpec((LT, K), lambda i, j: (i, 0)),
            pl.BlockSpec((K, tn), lambda i, j: (0, j)),
            pl.BlockSpec((LT, tn), lambda i, j: (i, j)),
            pl.BlockSpec((1, 1, tn), lambda i, j: (i, 0, j)),
            pl.BlockSpec((1, tn), lambda i, j: (0, j)),
        ],
        out_specs=pl.BlockSpec((LT, tn), lambda i, j: (i, j)),
        compiler_params=_cparams("parallel", "arbitrary"),
        name="out_proj",
    )(a, w, x, gate_lat.reshape(B, 1, N), gate_ctx.reshape(1, N))


def _gelu_exact(x):
    return 0.5 * x * (1.0 + lax.erf(x * (1.0 / math.sqrt(2.0))))


def _sigmoid(x):
    return 1.0 / (1.0 + jnp.exp(-x))


def _merge_kernel(uv_ref, gg_ref, rg_ref, ml_ref, og_ref, or_ref, gmn_ref, ws_ref, bs_ref, gdn_nw_ref,
                  ret_nw_ref, wgm_ref, wgdn_ref, wret_ref, o_ref):
    bf16 = jnp.bfloat16
    f32 = jnp.float32
    rows = uv_ref.shape[0]
    g = _gelu_exact(uv_ref[...])
    u = g[:, :GM_WIDTH]
    v = g[:, GM_WIDTH:]
    d = v - jnp.mean(v, axis=-1, keepdims=True)
    vn = (d * lax.rsqrt(jnp.mean(d * d, axis=-1, keepdims=True) + NORM_EPS) * gmn_ref[...]).astype(bf16)
    chunks = []
    for c in range(rows // GM_CHUNK):
        groups = []
        for gi in range(GM_GROUPS):
            blk = vn[c * GM_CHUNK:(c + 1) * GM_CHUNK, gi * GM_GROUP_CH:(gi + 1) * GM_GROUP_CH]
            groups.append(jnp.dot(ws_ref[gi], blk, preferred_element_type=f32) + bs_ref[gi])
        chunks.append(jnp.concatenate(groups, axis=-1))
    s = jnp.concatenate(chunks, axis=0)
    y_gm = jnp.dot((u * s).astype(bf16), wgm_ref[...], preferred_element_type=f32)

    og = og_ref[...]
    heads = []
    for h in range(GDN_HEADS):
        oh = og[:, h * GDN_HEAD_DIM:(h + 1) * GDN_HEAD_DIM]
        heads.append(oh * lax.rsqrt(jnp.mean(oh * oh, axis=-1, keepdims=True) + NORM_EPS) * gdn_nw_ref[...])
    gg = gg_ref[...]
    a_gdn = (jnp.concatenate(heads, axis=-1) * (gg * _sigmoid(gg))).astype(bf16)
    y_gdn = jnp.dot(a_gdn, wgdn_ref[...], preferred_element_type=f32)

    orr = or_ref[...]
    heads = []
    for h in range(RET_HEADS):
        oh = orr[:, h * RET_VAL_DIM:(h + 1) * RET_VAL_DIM]
        dh = oh - jnp.mean(oh, axis=-1, keepdims=True)
        heads.append(dh * lax.rsqrt(jnp.mean(dh * dh, axis=-1, keepdims=True) + NORM_EPS))
    rg = rg_ref[...]
    a_ret = (jnp.concatenate(heads, axis=-1) * ret_nw_ref[...] * (rg * _sigmoid(rg))).astype(bf16)
    y_ret = jnp.dot(a_ret, wret_ref[...], preferred_element_type=f32)

    sg = _sigmoid(ml_ref[...])
    merged = (sg[:, :D_MODEL] * y_gm + sg[:, D_MODEL:2 * D_MODEL] * y_gdn + sg[:, 2 * D_MODEL:] * y_ret)
    o_ref[...] = merged.astype(o_ref.dtype)


def _merge(p, o_gdn, o_ret, gm_norm_w, ws, bs, gdn_norm_w, ret_norm_w, wgm, wgdn, wret):
    M = p.shape[0]
    tm = MERGE_ROWS
    const = lambda *shape: pl.BlockSpec(shape, lambda i: (0,) * len(shape))
    return pl.pallas_call(
        _merge_kernel,
        out_shape=jax.ShapeDtypeStruct((M, D_MODEL), jnp.bfloat16),
        grid=(M // tm,),
        in_specs=[
            pl.BlockSpec((tm, GM_UV_COLS), lambda i: (i, C_UV // GM_UV_COLS)),
            pl.BlockSpec((tm, GDN_GATE_COLS), lambda i: (i, C_GG // GDN_GATE_COLS)),
            pl.BlockSpec((tm, RET_V_COLS), lambda i: (i, C_RG // RET_V_COLS)),
            pl.BlockSpec((tm, MERGE_COLS), lambda i: (i, C_ML // MERGE_COLS)),
            pl.BlockSpec((tm, GDN_GATE_COLS), lambda i: (i, 0)),
            pl.BlockSpec((tm, RET_V_COLS), lambda i: (i, 0)),
            const(1, GM_WIDTH),
            const(GM_GROUPS, GM_CHUNK, GM_CHUNK),
            const(GM_GROUPS, GM_CHUNK, GM_GROUP_CH),
            const(1, GDN_HEAD_DIM),
            const(1, RET_V_COLS),
            const(GM_WIDTH, D_MODEL),
            const(GDN_GATE_COLS, D_MODEL),
            const(RET_V_COLS, D_MODEL),
        ],
        out_specs=pl.BlockSpec((tm, D_MODEL), lambda i: (i, 0)),
        compiler_params=_cparams("parallel"),
        name="merge",
    )(p, p, p, p, o_gdn, o_ret, gm_norm_w.reshape(1, -1), ws, bs, gdn_norm_w.reshape(1, -1),
      ret_norm_w.reshape(1, -1), wgm, wgdn, wret)


def _moe_kernel(bexp_ref, bcnt_ref, stok_ref, sdst_ref,
                h_hbm, sw_ref, wg_ref, wu_ref, wd_ref, y_hbm,
                xbuf, ybuf, wgb, wub, wdb, gsem, ssem):
    i = pl.program_id(0)
    n = pl.num_programs(0)
    R = MOE_ROWS
    slot = lax.rem(i, 2)

    def gather_copy(blk, r, s):
        tok = stok_ref[blk * R + r]
        return pltpu.make_async_copy(h_hbm.at[pl.ds(tok, 1), :], xbuf.at[s, pl.ds(r, 1), :], gsem.at[s])

    def scatter_copy(blk, r, s):
        dst = sdst_ref[blk * R + r]
        return pltpu.make_async_copy(ybuf.at[s, pl.ds(r, 1), :], y_hbm.at[pl.ds(dst, 1), :], ssem.at[s])

    def start_gather(blk, s):
        lax.fori_loop(0, bcnt_ref[blk], lambda r, c: (gather_copy(blk, r, s).start(), c)[1], 0)

    def wait_gather(blk, s):
        lax.fori_loop(0, bcnt_ref[blk], lambda r, c: (gather_copy(blk, r, s).wait(), c)[1], 0)

    def start_scatter(blk, s):
        lax.fori_loop(0, bcnt_ref[blk], lambda r, c: (scatter_copy(blk, r, s).start(), c)[1], 0)

    def wait_scatter(blk, s):
        lax.fori_loop(0, bcnt_ref[blk], lambda r, c: (scatter_copy(blk, r, s).wait(), c)[1], 0)

    @pl.when(i == 0)
    def _():
        xbuf[...] = jnp.zeros_like(xbuf)
        start_gather(0, 0)

    @pl.when(i + 1 < n)
    def _():
        start_gather(i + 1, 1 - slot)

    changed = jnp.logical_or(i == 0, bexp_ref[i] != bexp_ref[jnp.maximum(i - 1, 0)])

    @pl.when(changed)
    def _():
        wgb[...] = wg_ref[0].astype(jnp.bfloat16)
        wub[...] = wu_ref[0].astype(jnp.bfloat16)
        wdb[...] = wd_ref[0].astype(jnp.bfloat16)

    wait_gather(i, slot)

    @pl.when(i >= 2)
    def _():
        wait_scatter(i - 2, slot)

    xb = xbuf[slot].astype(jnp.bfloat16)
    hg = jnp.dot(xb, wgb[...], preferred_element_type=jnp.float32)
    hu = jnp.dot(xb, wub[...], preferred_element_type=jnp.float32)
    act = (hg * _sigmoid(hg)) * hu
    y = jnp.dot(act.astype(jnp.bfloat16), wdb[...], preferred_element_type=jnp.float32)
    ybuf[slot] = y * sw_ref[...]
    start_scatter(i, slot)

    @pl.when(i == n - 1)
    def _():
        @pl.when(n >= 2)
        def _():
            wait_scatter(i - 1, 1 - slot)
        wait_scatter(i, slot)


def _moe(h, block_expert, block_cnt, slot_tok, slot_dst, slot_w, w_gate, w_up, w_down):
    T, D = h.shape
    n_blocks = block_expert.shape[0]
    R = MOE_ROWS
    wmap = lambda i, be, bc, st, sd: (be[i], 0, 0)
    grid_spec = pltpu.PrefetchScalarGridSpec(
        num_scalar_prefetch=4,
        grid=(n_blocks,),
        in_specs=[
            pl.BlockSpec(memory_space=pl.ANY),
            pl.BlockSpec((R, 1), lambda i, be, bc, st, sd: (i, 0)),
            pl.BlockSpec((1, D, MOE_HIDDEN), wmap),
            pl.BlockSpec((1, D, MOE_HIDDEN), wmap),
            pl.BlockSpec((1, MOE_HIDDEN, D), wmap),
        ],
        out_specs=pl.BlockSpec(memory_space=pl.ANY),
        scratch_shapes=[
            pltpu.VMEM((2, R, D), jnp.float32),
            pltpu.VMEM((2, R, D), jnp.float32),
            pltpu.VMEM((D, MOE_HIDDEN), jnp.bfloat16),
            pltpu.VMEM((D, MOE_HIDDEN), jnp.bfloat16),
            pltpu.VMEM((MOE_HIDDEN, D), jnp.bfloat16),
            pltpu.SemaphoreType.DMA((2,)),
            pltpu.SemaphoreType.DMA((2,)),
        ],
    )
    return pl.pallas_call(
        _moe_kernel,
        out_shape=jax.ShapeDtypeStruct((MOE_TOP_K * T, D), jnp.float32),
        grid_spec=grid_spec,
        compiler_params=_cparams("arbitrary"),
        name="moe_experts",
    )(block_expert, block_cnt, slot_tok, slot_dst, h, slot_w, w_gate, w_up, w_down)


def _combine_kernel(x_ref, y_ref, g_ref, o_ref):
    o_ref[0] = x_ref[0] + g_ref[0, 0] * (y_ref[0, 0] + y_ref[1, 0])


def _combine(z, y, gate):
    B, LT, D = z.shape
    return pl.pallas_call(
        _combine_kernel,
        out_shape=jax.ShapeDtypeStruct((B, LT, D), jnp.float32),
        grid=(B, LT // NORM_ROWS),
        in_specs=[
            pl.BlockSpec((1, NORM_ROWS, D), lambda b, t: (b, t, 0)),
            pl.BlockSpec((2, 1, NORM_ROWS, D), lambda b, t: (0, b, t, 0)),
            pl.BlockSpec((1, 1, 1, D), lambda b, t: (b, jnp.minimum(t, 1), 0, 0)),
        ],
        out_specs=pl.BlockSpec((1, NORM_ROWS, D), lambda b, t: (b, t, 0)),
        compiler_params=_cparams("parallel", "parallel"),
        name="moe_combine",
    )(z, y, gate)


def _route(h, wg, bg, we, be):
    T = h.shape[0]
    hi = lax.Precision.HIGHEST
    p_group = jax.nn.softmax(jnp.dot(h, wg, precision=hi) + bg, axis=-1)
    pg_top, grp = lax.top_k(p_group, 1)
    logits_e = (jnp.dot(h, we, precision=hi) + be).reshape(T, MOE_GROUPS, MOE_EXPERTS_PER_GROUP)
    sel = jnp.broadcast_to(grp[:, :, None], (T, 1, MOE_EXPERTS_PER_GROUP))
    p_e = jax.nn.softmax(jnp.take_along_axis(logits_e, sel, axis=1)[:, 0], axis=-1)
    pe_top, e_top = lax.top_k(p_e, MOE_TOP_K)
    weights = pg_top * pe_top / jnp.sum(pe_top, axis=-1, keepdims=True)
    return grp * MOE_EXPERTS_PER_GROUP + e_top, weights


def _dispatch_tables(expert_idx, weights):
    T, K = expert_idx.shape
    E, R = MOE_EXPERTS, MOE_ROWS
    A = T * K
    n_blocks = A // R + E
    flat_e = expert_idx.reshape(A).astype(jnp.int32)
    order = jnp.argsort(flat_e).astype(jnp.int32)
    e_sorted = flat_e[order]
    tok_sorted = order // K
    dst_sorted = (order % K) * T + tok_sorted
    w_sorted = weights.reshape(A)[order]
    counts = jnp.bincount(flat_e, length=E).astype(jnp.int32)
    nblk = (counts + R - 1) // R
    blk_end = jnp.cumsum(nblk)
    blk_start = blk_end - nblk
    cnt_start = jnp.cumsum(counts) - counts
    pos = blk_start[e_sorted] * R + jnp.arange(A, dtype=jnp.int32) - cnt_start[e_sorted]
    slot_tok = jnp.zeros((n_blocks * R,), jnp.int32).at[pos].set(tok_sorted)
    slot_dst = jnp.zeros((n_blocks * R,), jnp.int32).at[pos].set(dst_sorted)
    slot_w = jnp.zeros((n_blocks * R,), jnp.float32).at[pos].set(w_sorted)
    blk = jnp.arange(n_blocks, dtype=jnp.int32)
    block_expert = jnp.minimum(jnp.searchsorted(blk_end, blk, side='right'), E - 1).astype(jnp.int32)
    block_cnt = jnp.clip(counts[block_expert] - (blk - blk_start[block_expert]) * R, 0, R).astype(jnp.int32)
    return block_expert, block_cnt, slot_tok, slot_dst, slot_w.reshape(-1, 1)


def _flip(t):
    return jnp.flip(t, axis=1)


def _l2_normalize(t):
    return t * lax.rsqrt(jnp.sum(t * t, axis=-1, keepdims=True) + NORM_EPS)


def _centred_dwconv(t, w):
    K = w.shape[0]
    r = K // 2
    L = t.shape[1]
    tp = jnp.pad(t, ((0, 0), (r, r), (0, 0)))
    out = tp[:, 0:L] * w[0]
    for j in range(1, K):
        out = out + tp[:, j:j + L] * w[j]
    return out


def _axial_rope(rows, dim):
    n = dim // 4
    freq = ROPE_BASE ** (-jnp.arange(n, dtype=jnp.float32) / n)
    row = jnp.repeat(jnp.arange(rows, dtype=jnp.float32), GRID_W)
    col = (jnp.arange(rows * GRID_W) % GRID_W).astype(jnp.float32)
    ang = jnp.concatenate([row[:, None] * freq, col[:, None] * freq], axis=-1)
    ang = jnp.concatenate([ang, ang], axis=-1)[:, None, :]
    return jnp.cos(ang), jnp.sin(ang)


def _apply_rope(t, cos, sin):
    t1, t2 = jnp.split(t, 2, axis=-1)
    return t * cos + jnp.concatenate([-t2, t1], axis=-1) * sin


def _to_chunks(t, size):
    B, L, H = t.shape[:3]
    return jnp.moveaxis(t.reshape(B, L // size, size, H, -1), 3, 2)


def _from_chunks(o):
    N, B, H, C, d = o.shape
    return jnp.moveaxis(jnp.moveaxis(o, 0, 1), 2, 3).reshape(B, N * C, H, d)


def _gated_delta_chunked(q, k, v, g, beta, state0):
    Dk = q.shape[-1]
    Dv = v.shape[-1]
    C = GDN_CHUNK
    qc = _to_chunks(q * Dk ** -0.5, C)
    kc = _to_chunks(k, C)
    vc = _to_chunks(v, C)
    bc = _to_chunks(beta[..., None], C)
    gc = jnp.cumsum(_to_chunks(g[..., None], C)[..., 0], axis=-1)
    causal = jnp.tril(jnp.ones((C, C), bool))
    strict = jnp.tril(jnp.ones((C, C), bool), -1)
    decay = jnp.exp(jnp.where(causal, gc[..., :, None] - gc[..., None, :], -jnp.inf))
    kb = kc * bc
    a_mat = jnp.where(strict, jnp.einsum('bnhik,bnhjk->bnhij', kb, kc) * decay, 0.0) + jnp.eye(C, dtype=kc.dtype)
    rhs = jnp.concatenate([vc * bc, kb * jnp.exp(gc)[..., None]], axis=-1)
    sol = lax.linalg.triangular_solve(a_mat, rhs, left_side=True, lower=True, unit_diagonal=True)
    u, w = sol[..., :Dv], sol[..., Dv:]
    qk = jnp.einsum('bnhik,bnhjk->bnhij', qc, kc) * decay

    def step(S, xs):
        q_i, k_i, u_i, w_i, g_i, qk_i = xs
        v_new = u_i - jnp.einsum('bhck,bhkv->bhcv', w_i, S)
        o_i = (jnp.einsum('bhck,bhkv->bhcv', q_i * jnp.exp(g_i)[..., None], S)
               + jnp.einsum('bhij,bhjv->bhiv', qk_i, v_new))
        g_last = g_i[..., -1:]
        S = (S * jnp.exp(g_last)[..., None]
             + jnp.einsum('bhck,bhcv->bhkv', k_i * jnp.exp(g_last - g_i)[..., None], v_new))
        return S, o_i

    xs = tuple(jnp.moveaxis(t, 1, 0) for t in (qc, kc, u, w, gc, qk))
    S, o = lax.scan(step, state0, xs)
    return _from_chunks(o), S


def _retention_chunked(q, k, v, log_gamma, state0):
    C = RET_CHUNK
    qc, kc, vc = _to_chunks(q, C), _to_chunks(k, C), _to_chunks(v, C)
    pos = jnp.arange(C, dtype=jnp.float32)
    diff = pos[:, None] - pos[None, :]
    lg = log_gamma[:, None, None]
    decay = jnp.exp(jnp.where(diff >= 0, lg * diff, -jnp.inf))
    cross_decay = jnp.exp(log_gamma[:, None] * (pos + 1.0))[..., None]
    in_decay = jnp.exp(log_gamma[:, None] * (C - 1.0 - pos))[..., None]
    chunk_decay = jnp.exp(log_gamma * C)[:, None, None]
    inner = jnp.einsum('bnhij,bnhjv->bnhiv', jnp.einsum('bnhik,bnhjk->bnhij', qc, kc) * decay, vc)

    def step(S, xs):
        q_i, k_i, v_i = xs
        cross = jnp.einsum('bhck,bhkv->bhcv', q_i, S) * cross_decay
        S = S * chunk_decay + jnp.einsum('bhck,bhcv->bhkv', k_i * in_decay, v_i)
        return S, cross

    S, cross = lax.scan(step, state0, tuple(jnp.moveaxis(t, 1, 0) for t in (qc, kc, vc)))
    return _from_chunks(jnp.moveaxis(inner, 1, 0) + cross), S


def _recurrent_passes(rec, conv_w, a_log, dt_bias, log_gamma, rope, init):
    gdn_qkv, gdn_a, gdn_b, ret_q, ret_k, ret_v = rec
    B, L, _ = gdn_qkv.shape
    qkv = jax.nn.silu(_centred_dwconv(gdn_qkv, conv_w))
    gq, gk, gv = [t.reshape(B, L, GDN_HEADS, GDN_HEAD_DIM) for t in jnp.split(qkv, 3, axis=-1)]
    gq, gk = _l2_normalize(gq), _l2_normalize(gk)
    a = gdn_a.reshape(B, L, 2, GDN_HEADS)
    log_decay = -jnp.exp(a_log) * jax.nn.softplus(a + dt_bias)
    beta = jax.nn.sigmoid(gdn_b.reshape(B, L, 2, GDN_HEADS))
    o_gf, s_gf = _gated_delta_chunked(gq, gk, gv, log_decay[:, :, 0], beta[:, :, 0], init[0])
    o_gb, s_gb = _gated_delta_chunked(_flip(gq), _flip(gk), _flip(gv), _flip(log_decay[:, :, 1]),
                                      _flip(beta[:, :, 1]), init[1])
    o_gdn = o_gf + _flip(o_gb)

    rq = ret_q.reshape(B, L, RET_HEADS, RET_KEY_DIM)
    rk = ret_k.reshape(B, L, RET_HEADS, RET_KEY_DIM)
    rv = ret_v.reshape(B, L, RET_HEADS, RET_VAL_DIM)
    if rope is not None:
        rq, rk = _apply_rope(rq, *rope), _apply_rope(rk, *rope)
    rq = rq * RET_KEY_DIM ** -0.5
    o_rf, s_rf = _retention_chunked(rq, rk, rv, log_gamma[0], init[2])
    o_rb, s_rb = _retention_chunked(_flip(rq), _flip(rk), _flip(rv), log_gamma[1], init[3])
    return o_gdn, o_rf + _flip(o_rb), (s_gf, s_gb, s_rf, s_rb)


def _seg_table(ctx_vec, lat_vec):
    B, D = lat_vec.shape
    return jnp.stack([jnp.broadcast_to(ctx_vec, (B, D)), lat_vec], axis=1).reshape(B, 2, 1, D)


def kernel(x, c, ctx, c_ctx, mod_w, mod_b, norm1_w, w_in, gm_norm_w, gm_spatial_w, gm_spatial_b, gdn_conv_w, gdn_a_log, gdn_dt_bias, gdn_norm_w, ret_decay_logit, ret_norm_w, w_br_gm, w_br_gdn, w_br_ret, w_out, norm2_w, router_group_w, router_group_b, router_expert_w, router_expert_b, moe_w_gate, moe_w_up, moe_w_down, final_norm_w):
    B, L, D = x.shape
    depth = mod_w.shape[0]
    LT = CTX_LEN + L
    T = B * LT
    bf16 = jnp.bfloat16
    rope = _axial_rope(L // GRID_W, RET_KEY_DIM)
    gdn_zero = jnp.zeros((B, GDN_HEADS, GDN_HEAD_DIM, GDN_HEAD_DIM), jnp.float32)
    ret_zero = jnp.zeros((B, RET_HEADS, RET_KEY_DIM, RET_VAL_DIM), jnp.float32)
    zero_states = (gdn_zero, gdn_zero, ret_zero, ret_zero)

    w_main = jnp.concatenate([w_in[:, :, :AB_START], w_in[:, :, AB_END:]], axis=-1).astype(bf16)
    w_ab = jnp.pad(w_in[:, :, AB_START:AB_END], ((0, 0), (0, 0), (0, 128 - 2 * GDN_AB_COLS))).astype(bf16)
    wgm_b, wgdn_b, wret_b, wout_b = (t.astype(bf16) for t in (w_br_gm, w_br_gdn, w_br_ret, w_out))
    ws_b = gm_spatial_w.astype(bf16)
    bs_full = jnp.broadcast_to(gm_spatial_b[..., None], gm_spatial_b.shape + (GM_GROUP_CH,))

    cc = jnp.concatenate([jax.nn.silu(c), jax.nn.silu(c_ctx)[None], jnp.zeros((16 - B - 1, D), c.dtype)], axis=0)
    cc = cc.astype(bf16)

    z = jnp.concatenate([ctx, x], axis=1)
    for l in range(depth):
        mod = _matmul(cc, mod_w[l], 16, 1024, "adaln_mod") + mod_b[l]
        tabs = [_seg_table(mod[B, i * D:(i + 1) * D], mod[:B, i * D:(i + 1) * D]) for i in range(6)]
        sh1, sc1, gt1, sh2, sc2, gt2 = tabs
        log_gamma = jax.nn.log_sigmoid(ret_decay_logit[l])
        rec_params = (gdn_conv_w[l], gdn_a_log[l], gdn_dt_bias[l], log_gamma)

        h = _normmod(z, norm1_w[l], sh1, sc1, bf16).reshape(T, D)
        p = _matmul(h, w_main[l], LT, PROJ_COLS, "in_proj")
        pab = _matmul(h, w_ab[l], LT, 128, "in_proj_ab")[:, :2 * GDN_AB_COLS]
        p3 = p.reshape(B, LT, P_COLS)
        ab3 = pab.reshape(B, LT, 2 * GDN_AB_COLS)

        def rec_of(lo, hi):
            return (p3[:, lo:hi, C_QKV:C_RQ], ab3[:, lo:hi, :GDN_AB_COLS], ab3[:, lo:hi, GDN_AB_COLS:],
                    p3[:, lo:hi, C_RQ:C_RK], p3[:, lo:hi, C_RK:C_RV], p3[:, lo:hi, C_RV:C_GG])

        o_gdn_c, o_ret_c, ctx_states = _recurrent_passes(rec_of(0, CTX_LEN), *rec_params, None, zero_states)
        o_gdn_l, o_ret_l, _ = _recurrent_passes(rec_of(CTX_LEN, LT), *rec_params, rope, ctx_states)
        o_gdn = jnp.concatenate([o_gdn_c, o_gdn_l], axis=1).reshape(T, GDN_GATE_COLS)
        o_ret = jnp.concatenate([o_ret_c, o_ret_l], axis=1).reshape(T, RET_V_COLS)

        merged = _merge(p, o_gdn, o_ret, gm_norm_w[l], ws_b[l], bs_full[l], gdn_norm_w[l], ret_norm_w[l],
                        wgm_b[l], wgdn_b[l], wret_b[l])
        z = _resid_matmul(merged, wout_b[l], z.reshape(T, D), gt1[:, 1, 0], gt1[0, 0, 0], LT).reshape(B, LT, D)

        h2 = _normmod(z, norm2_w[l], sh2, sc2, jnp.float32).reshape(T, D)
        expert_idx, weights = _route(h2, router_group_w[l], router_group_b[l], router_expert_w[l],
                                     router_expert_b[l])
        tables = _dispatch_tables(expert_idx, weights)
        y = _moe(h2, *tables, moe_w_gate[l], moe_w_up[l], moe_w_down[l])
        z = _combine(z, y.reshape(MOE_TOP_K, B, LT, D), gt2)
    return _final_norm(z, final_norm_w)
```

```python
import functools
import math

import jax
import jax.numpy as jnp
from jax import lax
from jax.experimental import pallas as pl
from jax.experimental.pallas import tpu as pltpu

D_MODEL = 2048
CTX_LEN = 256
GRID_W = 64
NORM_EPS = 1e-6

GM_CHUNK = 128
GM_GROUPS = 4
GM_GROUP_CH = 128
GM_WIDTH = GM_GROUPS * GM_GROUP_CH
GDN_HEADS = 4
GDN_HEAD_DIM = 128
GDN_ROWS = 128
GDN_CHUNKS_PER_STEP = 6
RET_HEADS = 4
RET_KEY_DIM = 128
RET_VAL_DIM = 256
RET_ROWS = 256
ROPE_BASE = 10000.0
MOE_GROUPS = 4
MOE_EXPERTS_PER_GROUP = 8
MOE_EXPERTS = MOE_GROUPS * MOE_EXPERTS_PER_GROUP
MOE_TOP_K = 2
MOE_HIDDEN = 512

GM_UV_COLS = 2 * GM_WIDTH
GDN_QKV_COLS = 3 * GDN_HEADS * GDN_HEAD_DIM
GDN_AB_COLS = 2 * GDN_HEADS
GDN_GATE_COLS = GDN_HEADS * GDN_HEAD_DIM
RET_QK_COLS = RET_HEADS * RET_KEY_DIM
RET_V_COLS = RET_HEADS * RET_VAL_DIM
MERGE_COLS = 3 * D_MODEL
AB_START = GM_UV_COLS + GDN_QKV_COLS
AB_END = AB_START + 2 * GDN_AB_COLS

C_UV = 0
C_QKV = C_UV + GM_UV_COLS
C_RQ = C_QKV + GDN_QKV_COLS
C_RK = C_RQ + RET_QK_COLS
C_RV = C_RK + RET_QK_COLS
C_GG = C_RV + RET_V_COLS
C_RG = C_GG + GDN_GATE_COLS
C_ML = C_RG + RET_V_COLS
P_COLS = C_ML + MERGE_COLS

V7X_VMEM_LIMIT_BYTES = 56 * 1024 * 1024
MOE_ROWS = 256
MOE_DMA_GROUP = 8
MERGE_ROWS = 256
NORM_ROWS = 256
PROJ_COLS = 512


def _cparams(*sem):
    return pltpu.CompilerParams(dimension_semantics=sem, vmem_limit_bytes=V7X_VMEM_LIMIT_BYTES)


def _normmod_kernel(z_ref, nw_ref, sh_ref, sc_ref, o_ref):
    z = z_ref[0]
    y = z * lax.rsqrt(jnp.mean(z * z, axis=-1, keepdims=True) + NORM_EPS)
    y = y * nw_ref[...]
    o_ref[0] = (y * (1.0 + sc_ref[0, 0]) + sh_ref[0, 0]).astype(o_ref.dtype)


def _normmod(z, norm_w, shift, scale, out_dtype):
    B, LT, D = z.shape
    seg = lambda b, t: (b, jnp.minimum(t, 1), 0, 0)
    assert CTX_LEN == NORM_ROWS
    return pl.pallas_call(
        _normmod_kernel,
        out_shape=jax.ShapeDtypeStruct((B, LT, D), out_dtype),
        grid=(B, LT // NORM_ROWS),
        in_specs=[
            pl.BlockSpec((1, NORM_ROWS, D), lambda b, t: (b, t, 0)),
            pl.BlockSpec((1, D), lambda b, t: (0, 0)),
            pl.BlockSpec((1, 1, 1, D), seg),
            pl.BlockSpec((1, 1, 1, D), seg),
        ],
        out_specs=pl.BlockSpec((1, NORM_ROWS, D), lambda b, t: (b, t, 0)),
        compiler_params=_cparams("parallel", "parallel"),
        name="normmod",
    )(z, norm_w.reshape(1, D), shift, scale)


def _final_norm_kernel(z_ref, nw_ref, o_ref):
    z = z_ref[0]
    o_ref[0] = z * lax.rsqrt(jnp.mean(z * z, axis=-1, keepdims=True) + NORM_EPS) * nw_ref[...]


def _final_norm(z, norm_w):
    B, LT, D = z.shape
    skip = CTX_LEN // NORM_ROWS
    return pl.pallas_call(
        _final_norm_kernel,
        out_shape=jax.ShapeDtypeStruct((B, LT - CTX_LEN, D), jnp.float32),
        grid=(B, (LT - CTX_LEN) // NORM_ROWS),
        in_specs=[
            pl.BlockSpec((1, NORM_ROWS, D), lambda b, t: (b, t + skip, 0)),
            pl.BlockSpec((1, D), lambda b, t: (0, 0)),
        ],
        out_specs=pl.BlockSpec((1, NORM_ROWS, D), lambda b, t: (b, t, 0)),
        compiler_params=_cparams("parallel", "parallel"),
        name="final_norm",
    )(z, norm_w.reshape(1, D))


def _mm_kernel(a_ref, w_ref, o_ref):
    o_ref[...] = jnp.dot(a_ref[...], w_ref[...].astype(jnp.bfloat16), preferred_element_type=jnp.float32)


def _matmul(a, w, tm, tn, name):
    M, K = a.shape
    N = w.shape[1]
    return pl.pallas_call(
        _mm_kernel,
        out_shape=jax.ShapeDtypeStruct((M, N), jnp.float32),
        grid=(M // tm, N // tn),
        in_specs=[
            pl.BlockSpec((tm, K), lambda i, j: (i, 0)),
            pl.BlockSpec((K, tn), lambda i, j: (0, j)),
        ],
        out_specs=pl.BlockSpec((tm, tn), lambda i, j: (i, j)),
        compiler_params=_cparams("parallel", "arbitrary"),
        name=name,
    )(a, w)


def _resid_mm_kernel(a_ref, w_ref, x_ref, gl_ref, gc_ref, o_ref):
    acc = jnp.dot(a_ref[...], w_ref[...], preferred_element_type=jnp.float32)
    row = lax.broadcasted_iota(jnp.int32, (acc.shape[0], 1), 0)
    gate = jnp.where(row < CTX_LEN, gc_ref[...], gl_ref[0])
    o_ref[...] = x_ref[...] + gate * acc


def _resid_matmul(a, w, x, gate_lat, gate_ctx, LT):
    M, K = a.shape
    N = w.shape[1]
    B = M // LT
    tn = PROJ_COLS
    return pl.pallas_call(
        _resid_mm_kernel,
        out_shape=jax.ShapeDtypeStruct((M, N), jnp.float32),
        grid=(B, N // tn),
        in_specs=[
            pl.BlockSpec((LT, K), lambda i, j: (i, 0)),
            pl.BlockSpec((K, tn), lambda i, j: (0, j)),
            pl.BlockSpec((LT, tn), lambda i, j: (i, j)),
            pl.BlockSpec((1, 1, tn), lambda i, j: (i, 0, j)),
            pl.BlockSpec((1, tn), lambda i, j: (0, j)),
        ],
        out_specs=pl.BlockSpec((LT, tn), lambda i, j: (i, j)),
        compiler_params=_cparams("parallel", "arbitrary"),
        name="out_proj",
    )(a, w, x, gate_lat.reshape(B, 1, N), gate_ctx.reshape(1, N))


def _gelu_exact(x):
    return 0.5 * x * (1.0 + lax.erf(x * (1.0 / math.sqrt(2.0))))


def _sigmoid(x):
    return 1.0 / (1.0 + jnp.exp(-x))


def _merge_kernel(uv_ref, gg_ref, rg_ref, ml_ref, og_ref, or_ref, gmn_ref, ws_ref, bs_ref, gdn_nw_ref,
                  ret_nw_ref, wgm_ref, wgdn_ref, wret_ref, o_ref):
    bf16 = jnp.bfloat16
    f32 = jnp.float32
    rows = uv_ref.shape[0]
    g = _gelu_exact(uv_ref[...])
    u = g[:, :GM_WIDTH]
    v = g[:, GM_WIDTH:]
    d = v - jnp.mean(v, axis=-1, keepdims=True)
    vn = (d * lax.rsqrt(jnp.mean(d * d, axis=-1, keepdims=True) + NORM_EPS) * gmn_ref[...]).astype(bf16)
    chunks = []
    for c in range(rows // GM_CHUNK):
        groups = []
        for gi in range(GM_GROUPS):
            blk = vn[c * GM_CHUNK:(c + 1) * GM_CHUNK, gi * GM_GROUP_CH:(gi + 1) * GM_GROUP_CH]
            groups.append(jnp.dot(ws_ref[gi], blk, preferred_element_type=f32) + bs_ref[gi])
        chunks.append(jnp.concatenate(groups, axis=-1))
    s = jnp.concatenate(chunks, axis=0)
    y_gm = jnp.dot((u * s).astype(bf16), wgm_ref[...], preferred_element_type=f32)

    og = og_ref[...]
    heads = []
    for h in range(GDN_HEADS):
        oh = og[:, h * GDN_HEAD_DIM:(h + 1) * GDN_HEAD_DIM]
        heads.append(oh * lax.rsqrt(jnp.mean(oh * oh, axis=-1, keepdims=True) + NORM_EPS) * gdn_nw_ref[...])
    gg = gg_ref[...]
    a_gdn = (jnp.concatenate(heads, axis=-1) * (gg * _sigmoid(gg))).astype(bf16)
    y_gdn = jnp.dot(a_gdn, wgdn_ref[...], preferred_element_type=f32)

    orr = or_ref[...]
    heads = []
    for h in range(RET_HEADS):
        oh = orr[:, h * RET_VAL_DIM:(h + 1) * RET_VAL_DIM]
        dh = oh - jnp.mean(oh, axis=-1, keepdims=True)
        heads.append(dh * lax.rsqrt(jnp.mean(dh * dh, axis=-1, keepdims=True) + NORM_EPS))
    rg = rg_ref[...]
    a_ret = (jnp.concatenate(heads, axis=-1) * ret_nw_ref[...] * (rg * _sigmoid(rg))).astype(bf16)
    y_ret = jnp.dot(a_ret, wret_ref[...], preferred_element_type=f32)

    sg = _sigmoid(ml_ref[...])
    merged = (sg[:, :D_MODEL] * y_gm + sg[:, D_MODEL:2 * D_MODEL] * y_gdn + sg[:, 2 * D_MODEL:] * y_ret)
    o_ref[...] = merged.astype(o_ref.dtype)


def _merge(p, o_gdn, o_ret, gm_norm_w, ws, bs, gdn_norm_w, ret_norm_w, wgm, wgdn, wret):
    M = p.shape[0]
    tm = MERGE_ROWS
    const = lambda *shape: pl.BlockSpec(shape, lambda i: (0,) * len(shape))
    return pl.pallas_call(
        _merge_kernel,
        out_shape=jax.ShapeDtypeStruct((M, D_MODEL), jnp.bfloat16),
        grid=(M // tm,),
        in_specs=[
            pl.BlockSpec((tm, GM_UV_COLS), lambda i: (i, C_UV // GM_UV_COLS)),
            pl.BlockSpec((tm, GDN_GATE_COLS), lambda i: (i, C_GG // GDN_GATE_COLS)),
            pl.BlockSpec((tm, RET_V_COLS), lambda i: (i, C_RG // RET_V_COLS)),
            pl.BlockSpec((tm, MERGE_COLS), lambda i: (i, C_ML // MERGE_COLS)),
            pl.BlockSpec((tm, GDN_GATE_COLS), lambda i: (i, 0)),
            pl.BlockSpec((tm, RET_V_COLS), lambda i: (i, 0)),
            const(1, GM_WIDTH),
            const(GM_GROUPS, GM_CHUNK, GM_CHUNK),
            const(GM_GROUPS, GM_CHUNK, GM_GROUP_CH),
            const(1, GDN_HEAD_DIM),
            const(1, RET_V_COLS),
            const(GM_WIDTH, D_MODEL),
            const(GDN_GATE_COLS, D_MODEL),
            const(RET_V_COLS, D_MODEL),
        ],
        out_specs=pl.BlockSpec((tm, D_MODEL), lambda i: (i, 0)),
        compiler_params=_cparams("parallel"),
        name="merge",
    )(p, p, p, p, o_gdn, o_ret, gm_norm_w.reshape(1, -1), ws, bs, gdn_norm_w.reshape(1, -1),
      ret_norm_w.reshape(1, -1), wgm, wgdn, wret)


def _moe_kernel(bexp_ref, bcnt_ref, stok_ref, sdst_ref,
                h_hbm, sw_ref, wg_ref, wu_ref, wd_ref, y_hbm,
                xbuf, ybuf, wgb, wub, wdb, gsem, ssem):
    i = pl.program_id(0)
    n = pl.num_programs(0)
    R = MOE_ROWS
    slot = lax.rem(i, 2)

    def gather_copy(blk, r, s):
        tok = stok_ref[blk * R + r]
        return pltpu.make_async_copy(h_hbm.at[pl.ds(tok, 1), :], xbuf.at[s, pl.ds(r, 1), :], gsem.at[s])

    def scatter_copy(blk, r, s):
        dst = sdst_ref[blk * R + r]
        return pltpu.make_async_copy(ybuf.at[s, pl.ds(r, 1), :], y_hbm.at[pl.ds(dst, 1), :], ssem.at[s])

    def for_rows(blk, fn):
        cnt = bcnt_ref[blk]
        full = lax.shift_right_logical(cnt, MOE_DMA_GROUP.bit_length() - 1)

        def group(g, c):
            for j in range(MOE_DMA_GROUP):
                fn(g * MOE_DMA_GROUP + j)
            return c
        lax.fori_loop(0, full, group, 0)
        lax.fori_loop(full * MOE_DMA_GROUP, cnt, lambda r, c: (fn(r), c)[1], 0)

    def start_gather(blk, s):
        for_rows(blk, lambda r: gather_copy(blk, r, s).start())

    def wait_gather(blk, s):
        for_rows(blk, lambda r: gather_copy(blk, r, s).wait())

    def start_scatter(blk, s):
        for_rows(blk, lambda r: scatter_copy(blk, r, s).start())

    def wait_scatter(blk, s):
        for_rows(blk, lambda r: scatter_copy(blk, r, s).wait())

    @pl.when(i == 0)
    def _():
        xbuf[...] = jnp.zeros_like(xbuf)
        start_gather(0, 0)

    @pl.when(i + 1 < n)
    def _():
        start_gather(i + 1, 1 - slot)

    changed = jnp.logical_or(i == 0, bexp_ref[i] != bexp_ref[jnp.maximum(i - 1, 0)])

    @pl.when(changed)
    def _():
        wgb[...] = wg_ref[0].astype(jnp.bfloat16)
        wub[...] = wu_ref[0].astype(jnp.bfloat16)
        wdb[...] = wd_ref[0].astype(jnp.bfloat16)

    wait_gather(i, slot)

    @pl.when(i >= 2)
    def _():
        wait_scatter(i - 2, slot)

    xb = xbuf[slot].astype(jnp.bfloat16)
    hg = jnp.dot(xb, wgb[...], preferred_element_type=jnp.float32)
    hu = jnp.dot(xb, wub[...], preferred_element_type=jnp.float32)
    act = (hg * _sigmoid(hg)) * hu
    y = jnp.dot(act.astype(jnp.bfloat16), wdb[...], preferred_element_type=jnp.float32)
    ybuf[slot] = y * sw_ref[...]
    start_scatter(i, slot)

    @pl.when(i == n - 1)
    def _():
        @pl.when(n >= 2)
        def _():
            wait_scatter(i - 1, 1 - slot)
        wait_scatter(i, slot)


def _moe(h, block_expert, block_cnt, slot_tok, slot_dst, slot_w, w_gate, w_up, w_down):
    T, D = h.shape
    n_blocks = block_expert.shape[0]
    R = MOE_ROWS
    wmap = lambda i, be, bc, st, sd: (be[i], 0, 0)
    grid_spec = pltpu.PrefetchScalarGridSpec(
        num_scalar_prefetch=4,
        grid=(n_blocks,),
        in_specs=[
            pl.BlockSpec(memory_space=pl.ANY),
            pl.BlockSpec((R, 1), lambda i, be, bc, st, sd: (i, 0)),
            pl.BlockSpec((1, D, MOE_HIDDEN), wmap),
            pl.BlockSpec((1, D, MOE_HIDDEN), wmap),
            pl.BlockSpec((1, MOE_HIDDEN, D), wmap),
        ],
        out_specs=pl.BlockSpec(memory_space=pl.ANY),
        scratch_shapes=[
            pltpu.VMEM((2, R, D), jnp.float32),
            pltpu.VMEM((2, R, D), jnp.float32),
            pltpu.VMEM((D, MOE_HIDDEN), jnp.bfloat16),
            pltpu.VMEM((D, MOE_HIDDEN), jnp.bfloat16),
            pltpu.VMEM((MOE_HIDDEN, D), jnp.bfloat16),
            pltpu.SemaphoreType.DMA((2,)),
            pltpu.SemaphoreType.DMA((2,)),
        ],
    )
    return pl.pallas_call(
        _moe_kernel,
        out_shape=jax.ShapeDtypeStruct((MOE_TOP_K * T, D), jnp.float32),
        grid_spec=grid_spec,
        compiler_params=_cparams("arbitrary"),
        name="moe_experts",
    )(block_expert, block_cnt, slot_tok, slot_dst, h, slot_w, w_gate, w_up, w_down)


def _combine_kernel(x_ref, y0_ref, y1_ref, g_ref, o_ref):
    o_ref[0] = x_ref[0] + g_ref[0, 0] * (y0_ref[...] + y1_ref[...])


def _combine(z, y, gate):
    B, LT, D = z.shape
    nt = LT // NORM_ROWS
    return pl.pallas_call(
        _combine_kernel,
        out_shape=jax.ShapeDtypeStruct((B, LT, D), jnp.float32),
        grid=(B, nt),
        in_specs=[
            pl.BlockSpec((1, NORM_ROWS, D), lambda b, t: (b, t, 0)),
            pl.BlockSpec((NORM_ROWS, D), lambda b, t: (b * nt + t, 0)),
            pl.BlockSpec((NORM_ROWS, D), lambda b, t: (B * nt + b * nt + t, 0)),
            pl.BlockSpec((1, 1, 1, D), lambda b, t: (b, jnp.minimum(t, 1), 0, 0)),
        ],
        out_specs=pl.BlockSpec((1, NORM_ROWS, D), lambda b, t: (b, t, 0)),
        compiler_params=_cparams("parallel", "parallel"),
        name="moe_combine",
    )(z, y, y, gate)


def _route(h, wg, bg, we, be):
    T = h.shape[0]
    hi = lax.Precision.HIGHEST
    p_group = jax.nn.softmax(jnp.dot(h, wg, precision=hi) + bg, axis=-1)
    pg_top, grp = lax.top_k(p_group, 1)
    logits_e = (jnp.dot(h, we, precision=hi) + be).reshape(T, MOE_GROUPS, MOE_EXPERTS_PER_GROUP)
    sel = jnp.broadcast_to(grp[:, :, None], (T, 1, MOE_EXPERTS_PER_GROUP))
    p_e = jax.nn.softmax(jnp.take_along_axis(logits_e, sel, axis=1)[:, 0], axis=-1)
    pe_top, e_top = lax.top_k(p_e, MOE_TOP_K)
    weights = pg_top * pe_top / jnp.sum(pe_top, axis=-1, keepdims=True)
    return grp * MOE_EXPERTS_PER_GROUP + e_top, weights


def _dispatch_tables(expert_idx, weights):
    T, K = expert_idx.shape
    E, R = MOE_EXPERTS, MOE_ROWS
    A = T * K
    n_blocks = A // R + E
    flat_e = expert_idx.reshape(A).astype(jnp.int32)
    order = jnp.argsort(flat_e).astype(jnp.int32)
    e_sorted = flat_e[order]
    tok_sorted = order // K
    dst_sorted = (order % K) * T + tok_sorted
    w_sorted = weights.reshape(A)[order]
    counts = jnp.bincount(flat_e, length=E).astype(jnp.int32)
    nblk = (counts + R - 1) // R
    blk_end = jnp.cumsum(nblk)
    blk_start = blk_end - nblk
    cnt_start = jnp.cumsum(counts) - counts
    pos = blk_start[e_sorted] * R + jnp.arange(A, dtype=jnp.int32) - cnt_start[e_sorted]
    slot_tok = jnp.zeros((n_blocks * R,), jnp.int32).at[pos].set(tok_sorted)
    slot_dst = jnp.zeros((n_blocks * R,), jnp.int32).at[pos].set(dst_sorted)
    slot_w = jnp.zeros((n_blocks * R,), jnp.float32).at[pos].set(w_sorted)
    blk = jnp.arange(n_blocks, dtype=jnp.int32)
    block_expert = jnp.minimum(jnp.searchsorted(blk_end, blk, side='right'), E - 1).astype(jnp.int32)
    block_cnt = jnp.clip(counts[block_expert] - (blk - blk_start[block_expert]) * R, 0, R).astype(jnp.int32)
    return block_expert, block_cnt, slot_tok, slot_dst, slot_w.reshape(-1, 1)


def _bdot(a, b):
    return jnp.dot(a.astype(jnp.bfloat16), b.astype(jnp.bfloat16), preferred_element_type=jnp.float32)


def _bdot_nt(a, b):
    return lax.dot_general(a.astype(jnp.bfloat16), b.astype(jnp.bfloat16), (((1,), (1,)), ((), ())),
                           preferred_element_type=jnp.float32)


def _bdot_tn(a, b):
    return lax.dot_general(a.astype(jnp.bfloat16), b.astype(jnp.bfloat16), (((0,), (0,)), ((), ())),
                           preferred_element_type=jnp.float32)


def _seq_pos(LT):
    row = lax.broadcasted_iota(jnp.int32, (LT, 1), 0)
    in_ctx = row < CTX_LEN
    return jnp.where(in_ctx, row, row - CTX_LEN), jnp.where(in_ctx, CTX_LEN, LT - CTX_LEN)


def _conv_silu(x, cw, pos, seq_len):
    LT = x.shape[0]
    K = cw.shape[0]
    r = K // 2
    out = x * cw[r:r + 1, :]
    for j in range(K):
        s = r - j
        if s == 0:
            continue
        xs = pltpu.roll(x, s % LT, axis=0)
        valid = (pos >= s) if s > 0 else (pos - s < seq_len)
        out = out + jnp.where(valid, xs, 0.0) * cw[j:j + 1, :]
    return out * _sigmoid(out)


def _l2n(t):
    return t * lax.rsqrt(jnp.sum(t * t, axis=-1, keepdims=True) + NORM_EPS)


def _lane_col(x, idx):
    lane = lax.broadcasted_iota(jnp.int32, x.shape, 1)
    return jnp.sum(jnp.where(lane == idx, x, 0.0), axis=1, keepdims=True)


def _inv_unit_tri(Ns, eye, m16, moffs):
    Dg = [N * m16 for N in Ns]
    D2 = [_bdot(d, d) for d in Dg]
    D4 = [_bdot(d, d) for d in D2]
    D8 = [_bdot(d, d) for d in D4]
    T = [eye - d for d in Dg]
    for P in (D2, D4, D8):
        T = [t + _bdot(t, p) for t, p in zip(T, P)]
    for m in moffs:
        X = [_bdot(t, N * m) for t, N in zip(T, Ns)]
        T = [t - _bdot(x, t) for t, x in zip(T, X)]
    return T


def _gdn_kernel(q_ref, k_ref, v_ref, ab_ref, cwq_ref, cwk_ref, cwv_ref, alog_ref, dtb_ref, o_ref,
                qn_s, kn_s, vn_s, gb_s, u_s, w_s, qs_s, kst_s, qkd_s, el_s, o_s):
    f32 = jnp.float32
    bf16 = jnp.bfloat16
    h = pl.program_id(1)
    LT = q_ref.shape[1]
    C = GDN_ROWS
    nch = LT // C
    nctx = CTX_LEN // C

    pos, seq_len = _seq_pos(LT)
    qn_s[...] = _l2n(_conv_silu(q_ref[0], cwq_ref[...], pos, seq_len)) * (GDN_HEAD_DIM ** -0.5)
    kn_s[...] = _l2n(_conv_silu(k_ref[0], cwk_ref[...], pos, seq_len))
    vn_s[...] = _conv_silu(v_ref[0], cwv_ref[...], pos, seq_len)

    ab = ab_ref[0]
    xa = ab + dtb_ref[...]
    g_all = -jnp.exp(alog_ref[...]) * (jnp.maximum(xa, 0.0) + jnp.log1p(jnp.exp(-jnp.abs(xa))))
    b_all = _sigmoid(ab)
    lane = lax.broadcasted_iota(jnp.int32, (LT, 128), 1)
    gb = jnp.where(lane == 0, _lane_col(g_all, h), 0.0)
    gb = jnp.where(lane == 1, _lane_col(g_all, GDN_HEADS + h), gb)
    gb = jnp.where(lane == 2, _lane_col(b_all, 2 * GDN_HEADS + h), gb)
    gb = jnp.where(lane == 3, _lane_col(b_all, 3 * GDN_HEADS + h), gb)
    gb_s[...] = gb

    ri = lax.broadcasted_iota(jnp.int32, (C, C), 0)
    ci = lax.broadcasted_iota(jnp.int32, (C, C), 1)
    eye = (ri == ci).astype(f32)
    low = (ri >= ci).astype(f32)
    upp = (ri <= ci).astype(f32)
    m16 = ((ri // 16) == (ci // 16)).astype(f32)
    moffs = [(((ri // (2 * s)) == (ci // (2 * s))) & ((ri // s) != (ci // s))).astype(f32) for s in (16, 32, 64)]
    hi = lax.Precision.HIGHEST

    def chunk_group(i, carry):
        probs = []
        for j in range(GDN_CHUNKS_PER_STEP):
            c = i * GDN_CHUNKS_PER_STEP + j
            rows = pl.ds(pl.multiple_of(c * C, C), C)
            q = qn_s[rows, :]
            k = kn_s[rows, :]
            v = vn_s[rows, :]
            gbc = gb_s[rows, :]
            kk = _bdot_nt(k, k)
            qk = _bdot_nt(q, k)
            cs_f = jnp.dot(low, gbc, precision=hi, preferred_element_type=f32)
            cs_b = jnp.dot(upp, gbc, precision=hi, preferred_element_type=f32)
            for d in range(2):
                cs = cs_f if d == 0 else cs_b
                gc = cs[:, d:d + 1]
                gc_row = cs.T[d:d + 1, :]
                beta = gbc[:, 2 + d:3 + d]
                incl = low if d == 0 else upp
                decay = jnp.exp(jnp.where(incl > 0, gc - gc_row, -jnp.inf))
                g_last = gc[C - 1:C, :] if d == 0 else gc[0:1, :]
                probs.append(dict(d=d, c=c, q=q, k=k, v=v, qk=qk, gc=gc, beta=beta, decay=decay, g_last=g_last,
                                  N=(beta * kk) * decay * (1.0 - eye)))
        Ts = _inv_unit_tri([p["N"] for p in probs], eye, m16, moffs)
        for p, T in zip(probs, Ts):
            d, c, gc, beta = p["d"], p["c"], p["gc"], p["beta"]
            eg = jnp.exp(gc)
            u_s[d, c] = _bdot(T, p["v"] * beta)
            w_s[d, c] = _bdot(T, p["k"] * beta * eg).astype(bf16)
            qs_s[d, c] = (p["q"] * eg).astype(bf16)
            kst_s[d, c] = (p["k"] * jnp.exp(p["g_last"] - gc)).T.astype(bf16)
            qkd_s[d, c] = (p["qk"] * p["decay"]).astype(bf16)
            el_s[d, c] = jnp.broadcast_to(jnp.exp(p["g_last"]), (8, 128))
        return carry

    lax.fori_loop(0, nch // GDN_CHUNKS_PER_STEP, chunk_group, 0)

    def step(t, S):
        cb = jnp.where(t < nctx, nctx - 1 - t, nch - 1 + nctx - t)
        outs = []
        for d, (c, Sd) in enumerate(((t, S[0]), (cb, S[1]))):
            Sbf = Sd.astype(bf16)
            v_new = u_s[d, c] - jnp.dot(w_s[d, c], Sbf, preferred_element_type=f32)
            vb = v_new.astype(bf16)
            o_s[d, c] = (jnp.dot(qs_s[d, c], Sbf, preferred_element_type=f32)
                         + jnp.dot(qkd_s[d, c], vb, preferred_element_type=f32))
            outs.append(Sd * el_s[d, c][0:1, :] + jnp.dot(kst_s[d, c], vb, preferred_element_type=f32))
        return tuple(outs)

    z = jnp.zeros((GDN_HEAD_DIM, GDN_HEAD_DIM), f32)
    lax.fori_loop(0, nch, step, (z, z))
    o_ref[0] = (o_s[0] + o_s[1]).reshape(LT, GDN_HEAD_DIM)


def _gdn_mixer(p3, ab, conv_w, a_log, dt_bias):
    B, LT, _ = p3.shape
    H, Dh, C = GDN_HEADS, GDN_HEAD_DIM, GDN_ROWS
    nch = LT // C
    qb = C_QKV // Dh
    alog_row = jnp.zeros((1, 128), jnp.float32).at[0, :2 * H].set(a_log.reshape(-1))
    dtb_row = jnp.zeros((1, 128), jnp.float32).at[0, :2 * H].set(dt_bias.reshape(-1))
    taps = conv_w.shape[0]
    per_chunk = lambda dt: pltpu.VMEM((2, nch, C, Dh), dt)
    return pl.pallas_call(
        _gdn_kernel,
        out_shape=jax.ShapeDtypeStruct((B, LT, H * Dh), jnp.float32),
        grid=(B, H),
        in_specs=[
            pl.BlockSpec((1, LT, Dh), lambda b, h: (b, 0, qb + h)),
            pl.BlockSpec((1, LT, Dh), lambda b, h: (b, 0, qb + H + h)),
            pl.BlockSpec((1, LT, Dh), lambda b, h: (b, 0, qb + 2 * H + h)),
            pl.BlockSpec((1, LT, 128), lambda b, h: (b, 0, 0)),
            pl.BlockSpec((taps, Dh), lambda b, h: (0, h)),
            pl.BlockSpec((taps, Dh), lambda b, h: (0, H + h)),
            pl.BlockSpec((taps, Dh), lambda b, h: (0, 2 * H + h)),
            pl.BlockSpec((1, 128), lambda b, h: (0, 0)),
            pl.BlockSpec((1, 128), lambda b, h: (0, 0)),
        ],
        out_specs=pl.BlockSpec((1, LT, Dh), lambda b, h: (b, 0, h)),
        scratch_shapes=[
            pltpu.VMEM((LT, Dh), jnp.float32), pltpu.VMEM((LT, Dh), jnp.float32), pltpu.VMEM((LT, Dh), jnp.float32),
            pltpu.VMEM((LT, 128), jnp.float32),
            per_chunk(jnp.float32), per_chunk(jnp.bfloat16), per_chunk(jnp.bfloat16), per_chunk(jnp.bfloat16),
            per_chunk(jnp.bfloat16),
            pltpu.VMEM((2, nch, 8, 128), jnp.float32),
            per_chunk(jnp.float32),
        ],
        compiler_params=_cparams("parallel", "parallel"),
        name="gdn_mixer",
    )(p3, p3, p3, ab, conv_w, conv_w, conv_w, alog_row, dtb_row)


def _ret_kernel(lg_ref, q_ref, k_ref, v_ref, cos_ref, sin_ref, o_ref, qr_s, kr_s):
    f32 = jnp.float32
    h = pl.program_id(1)
    LT = q_ref.shape[1]
    C = RET_ROWS
    nch = LT // C
    nctx = CTX_LEN // C
    cos = cos_ref[...]
    sin = sin_ref[...]
    half = RET_KEY_DIM // 2
    q = q_ref[0]
    k = k_ref[0]
    qr_s[...] = (q * cos + pltpu.roll(q, half, axis=1) * sin) * (RET_KEY_DIM ** -0.5)
    kr_s[...] = k * cos + pltpu.roll(k, half, axis=1) * sin

    lg_f = lg_ref[0, h]
    lg_b = lg_ref[1, h]
    ri = lax.broadcasted_iota(jnp.int32, (C, C), 0)
    ci = lax.broadcasted_iota(jnp.int32, (C, C), 1)
    diff = (ri - ci).astype(f32)
    dm = (jnp.exp(jnp.where(diff >= 0, lg_f * diff, -jnp.inf))
          + jnp.exp(jnp.where(diff <= 0, -lg_b * diff, -jnp.inf)))
    pcol = lax.broadcasted_iota(jnp.int32, (C, 1), 0).astype(f32)
    cross_f = jnp.exp(lg_f * (pcol + 1.0))
    in_f = jnp.exp(lg_f * (C - 1.0 - pcol))
    cross_b = jnp.exp(lg_b * (C - pcol))
    in_b = jnp.exp(lg_b * pcol)
    cd_f = jnp.exp(lg_f * C)
    cd_b = jnp.exp(lg_b * C)

    def inner(c, carry):
        rows = pl.ds(pl.multiple_of(c * C, C), C)
        s = _bdot_nt(qr_s[rows, :], kr_s[rows, :])
        o_ref[0, rows, :] = _bdot(s * dm, v_ref[0, rows, :])
        return carry

    lax.fori_loop(0, nch, inner, 0)

    def step(t, S):
        cb = jnp.where(t < nctx, nctx - 1 - t, nch - 1 + nctx - t)
        outs = []
        for c, Sd, cr, ind, cd in ((t, S[0], cross_f, in_f, cd_f), (cb, S[1], cross_b, in_b, cd_b)):
            rows = pl.ds(pl.multiple_of(c * C, C), C)
            o_ref[0, rows, :] += _bdot(qr_s[rows, :], Sd) * cr
            outs.append(Sd * cd + _bdot_tn(kr_s[rows, :] * ind, v_ref[0, rows, :]))
        return tuple(outs)

    z = jnp.zeros((RET_KEY_DIM, RET_VAL_DIM), f32)
    lax.fori_loop(0, nch, step, (z, z))


def _ret_mixer(p3, log_gamma, cos_t, sin_t):
    B, LT, _ = p3.shape
    H, Dk, Dv = RET_HEADS, RET_KEY_DIM, RET_VAL_DIM
    grid_spec = pltpu.PrefetchScalarGridSpec(
        num_scalar_prefetch=1,
        grid=(B, H),
        in_specs=[
            pl.BlockSpec((1, LT, Dk), lambda b, h, lg: (b, 0, C_RQ // Dk + h)),
            pl.BlockSpec((1, LT, Dk), lambda b, h, lg: (b, 0, C_RK // Dk + h)),
            pl.BlockSpec((1, LT, Dv), lambda b, h, lg: (b, 0, C_RV // Dv + h)),
            pl.BlockSpec((LT, Dk), lambda b, h, lg: (0, 0)),
            pl.BlockSpec((LT, Dk), lambda b, h, lg: (0, 0)),
        ],
        out_specs=pl.BlockSpec((1, LT, Dv), lambda b, h, lg: (b, 0, h)),
        scratch_shapes=[pltpu.VMEM((LT, Dk), jnp.float32), pltpu.VMEM((LT, Dk), jnp.float32)],
    )
    return pl.pallas_call(
        _ret_kernel,
        out_shape=jax.ShapeDtypeStruct((B, LT, H * Dv), jnp.float32),
        grid_spec=grid_spec,
        compiler_params=_cparams("parallel", "parallel"),
        name="ret_mixer",
    )(log_gamma, p3, p3, p3, cos_t, sin_t)


def _axial_rope(rows, dim):
    n = dim // 4
    freq = ROPE_BASE ** (-jnp.arange(n, dtype=jnp.float32) / n)
    row = jnp.repeat(jnp.arange(rows, dtype=jnp.float32), GRID_W)
    col = (jnp.arange(rows * GRID_W) % GRID_W).astype(jnp.float32)
    ang = jnp.concatenate([row[:, None] * freq, col[:, None] * freq], axis=-1)
    ang = jnp.concatenate([ang, ang], axis=-1)[:, None, :]
    return jnp.cos(ang), jnp.sin(ang)


def _rope_tables(L):
    cos, sin = _axial_rope(L // GRID_W, RET_KEY_DIM)
    half = RET_KEY_DIM // 2
    sign = jnp.concatenate([-jnp.ones((half,), jnp.float32), jnp.ones((half,), jnp.float32)])
    cos_t = jnp.concatenate([jnp.ones((CTX_LEN, RET_KEY_DIM), jnp.float32), cos[:, 0, :]], axis=0)
    sin_t = jnp.concatenate([jnp.zeros((CTX_LEN, RET_KEY_DIM), jnp.float32), sin[:, 0, :] * sign], axis=0)
    return cos_t, sin_t


def _seg_table(ctx_vec, lat_vec):
    B, D = lat_vec.shape
    return jnp.stack([jnp.broadcast_to(ctx_vec, (B, D)), lat_vec], axis=1).reshape(B, 2, 1, D)


def kernel(x, c, ctx, c_ctx, mod_w, mod_b, norm1_w, w_in, gm_norm_w, gm_spatial_w, gm_spatial_b, gdn_conv_w, gdn_a_log, gdn_dt_bias, gdn_norm_w, ret_decay_logit, ret_norm_w, w_br_gm, w_br_gdn, w_br_ret, w_out, norm2_w, router_group_w, router_group_b, router_expert_w, router_expert_b, moe_w_gate, moe_w_up, moe_w_down, final_norm_w):
    B, L, D = x.shape
    depth = mod_w.shape[0]
    LT = CTX_LEN + L
    T = B * LT
    bf16 = jnp.bfloat16
    cos_t, sin_t = _rope_tables(L)

    w_main = jnp.concatenate([w_in[:, :, :AB_START], w_in[:, :, AB_END:]], axis=-1).astype(bf16)
    w_ab = jnp.pad(w_in[:, :, AB_START:AB_END], ((0, 0), (0, 0), (0, 128 - 2 * GDN_AB_COLS))).astype(bf16)
    wgm_b, wgdn_b, wret_b, wout_b = (t.astype(bf16) for t in (w_br_gm, w_br_gdn, w_br_ret, w_out))
    ws_b = gm_spatial_w.astype(bf16)
    bs_full = jnp.broadcast_to(gm_spatial_b[..., None], gm_spatial_b.shape + (GM_GROUP_CH,))

    cc = jnp.concatenate([jax.nn.silu(c), jax.nn.silu(c_ctx)[None], jnp.zeros((16 - B - 1, D), c.dtype)], axis=0)
    cc = cc.astype(bf16)

    z = jnp.concatenate([ctx, x], axis=1)
    for l in range(depth):
        mod = _matmul(cc, mod_w[l], 16, 1024, "adaln_mod") + mod_b[l]
        tabs = [_seg_table(mod[B, i * D:(i + 1) * D], mod[:B, i * D:(i + 1) * D]) for i in range(6)]
        sh1, sc1, gt1, sh2, sc2, gt2 = tabs
        log_gamma = jax.nn.log_sigmoid(ret_decay_logit[l])

        h = _normmod(z, norm1_w[l], sh1, sc1, bf16).reshape(T, D)
        p = _matmul(h, w_main[l], LT, PROJ_COLS, "in_proj")
        pab = _matmul(h, w_ab[l], LT, 128, "in_proj_ab")
        p3 = p.reshape(B, LT, P_COLS)
        o_gdn = _gdn_mixer(p3, pab.reshape(B, LT, 128), gdn_conv_w[l], gdn_a_log[l], gdn_dt_bias[l])
        o_ret = _ret_mixer(p3, log_gamma, cos_t, sin_t)
        o_gdn = o_gdn.reshape(T, GDN_GATE_COLS)
        o_ret = o_ret.reshape(T, RET_V_COLS)

        merged = _merge(p, o_gdn, o_ret, gm_norm_w[l], ws_b[l], bs_full[l], gdn_norm_w[l], ret_norm_w[l],
                        wgm_b[l], wgdn_b[l], wret_b[l])
        z = _resid_matmul(merged, wout_b[l], z.reshape(T, D), gt1[:, 1, 0], gt1[0, 0, 0], LT).reshape(B, LT, D)

        h2 = _normmod(z, norm2_w[l], sh2, sc2, jnp.float32).reshape(T, D)
        expert_idx, weights = _route(h2, router_group_w[l], router_group_b[l], router_expert_w[l],
                                     router_expert_b[l])
        tables = _dispatch_tables(expert_idx, weights)
        y = _moe(h2, *tables, moe_w_gate[l], moe_w_up[l], moe_w_down[l])
        z = _combine(z, y, gt2)
    return _final_norm(z, final_norm_w)
```

```python
import functools
import math

import jax
import jax.numpy as jnp
from jax import lax
from jax.experimental import pallas as pl
from jax.experimental.pallas import tpu as pltpu

D_MODEL = 2048
CTX_LEN = 256
GRID_W = 64
NORM_EPS = 1e-6

GM_CHUNK = 128
GM_GROUPS = 4
GM_GROUP_CH = 128
GM_WIDTH = GM_GROUPS * GM_GROUP_CH
GDN_HEADS = 4
GDN_HEAD_DIM = 128
GDN_ROWS = 128
GDN_CHUNKS_PER_STEP = 6
RET_HEADS = 4
RET_KEY_DIM = 128
RET_VAL_DIM = 256
RET_ROWS = 256
ROPE_BASE = 10000.0
MOE_GROUPS = 4
MOE_EXPERTS_PER_GROUP = 8
MOE_EXPERTS = MOE_GROUPS * MOE_EXPERTS_PER_GROUP
MOE_TOP_K = 2
MOE_HIDDEN = 512

GM_UV_COLS = 2 * GM_WIDTH
GDN_QKV_COLS = 3 * GDN_HEADS * GDN_HEAD_DIM
GDN_AB_COLS = 2 * GDN_HEADS
GDN_GATE_COLS = GDN_HEADS * GDN_HEAD_DIM
RET_QK_COLS = RET_HEADS * RET_KEY_DIM
RET_V_COLS = RET_HEADS * RET_VAL_DIM
MERGE_COLS = 3 * D_MODEL
AB_START = GM_UV_COLS + GDN_QKV_COLS
AB_END = AB_START + 2 * GDN_AB_COLS

C_UV = 0
C_QKV = C_UV + GM_UV_COLS
C_RQ = C_QKV + GDN_QKV_COLS
C_RK = C_RQ + RET_QK_COLS
C_RV = C_RK + RET_QK_COLS
C_GG = C_RV + RET_V_COLS
C_RG = C_GG + GDN_GATE_COLS
C_ML = C_RG + RET_V_COLS
P_COLS = C_ML + MERGE_COLS

V7X_VMEM_LIMIT_BYTES = 56 * 1024 * 1024
MOE_ROWS = 256
MOE_DMA_GROUP = 8
MERGE_ROWS = 256
NORM_ROWS = 256
PROJ_COLS = 512


def _cparams(*sem):
    return pltpu.CompilerParams(dimension_semantics=sem, vmem_limit_bytes=V7X_VMEM_LIMIT_BYTES)


def _normmod_kernel(z_ref, nw_ref, sh_ref, sc_ref, o_ref):
    z = z_ref[0]
    y = z * lax.rsqrt(jnp.mean(z * z, axis=-1, keepdims=True) + NORM_EPS)
    y = y * nw_ref[...]
    o_ref[0] = (y * (1.0 + sc_ref[0, 0]) + sh_ref[0, 0]).astype(o_ref.dtype)


def _normmod(z, norm_w, shift, scale, out_dtype):
    B, LT, D = z.shape
    seg = lambda b, t: (b, jnp.minimum(t, 1), 0, 0)
    assert CTX_LEN == NORM_ROWS
    return pl.pallas_call(
        _normmod_kernel,
        out_shape=jax.ShapeDtypeStruct((B, LT, D), out_dtype),
        grid=(B, LT // NORM_ROWS),
        in_specs=[
            pl.BlockSpec((1, NORM_ROWS, D), lambda b, t: (b, t, 0)),
            pl.BlockSpec((1, D), lambda b, t: (0, 0)),
            pl.BlockSpec((1, 1, 1, D), seg),
            pl.BlockSpec((1, 1, 1, D), seg),
        ],
        out_specs=pl.BlockSpec((1, NORM_ROWS, D), lambda b, t: (b, t, 0)),
        compiler_params=_cparams("parallel", "parallel"),
        name="normmod",
    )(z, norm_w.reshape(1, D), shift, scale)


def _split_rows(ref, x):
    for c in range(x.shape[1] // 128):
        ref[:, c, :] = x[:, c * 128:(c + 1) * 128]


def _join_rows(ref):
    return jnp.concatenate([ref[:, c, :] for c in range(ref.shape[1])], axis=-1)


def _norm_route_kernel(z_ref, nw_ref, sh_ref, sc_ref, whi_ref, wlo_ref, h_ref, lg_ref):
    z = z_ref[0]
    y = z * lax.rsqrt(jnp.mean(z * z, axis=-1, keepdims=True) + NORM_EPS)
    h = y * nw_ref[...] * (1.0 + sc_ref[0, 0]) + sh_ref[0, 0]
    _split_rows(h_ref, h)
    hi = h.astype(jnp.bfloat16)
    lo = (h - hi.astype(jnp.float32)).astype(jnp.bfloat16)
    whi = whi_ref[...]
    lg_ref[...] = (jnp.dot(hi, whi, preferred_element_type=jnp.float32)
                   + jnp.dot(lo, whi, preferred_element_type=jnp.float32)
                   + jnp.dot(hi, wlo_ref[...], preferred_element_type=jnp.float32))


def _norm_route(z, norm_w, shift, scale, w_router):
    B, LT, D = z.shape
    nt = LT // NORM_ROWS
    seg = lambda b, t: (b, jnp.minimum(t, 1), 0, 0)
    whi = w_router.astype(jnp.bfloat16)
    wlo = (w_router - whi.astype(jnp.float32)).astype(jnp.bfloat16)
    return pl.pallas_call(
        _norm_route_kernel,
        out_shape=[jax.ShapeDtypeStruct((B * LT, D // 128, 128), jnp.float32),
                   jax.ShapeDtypeStruct((B * LT, 128), jnp.float32)],
        grid=(B, nt),
        in_specs=[
            pl.BlockSpec((1, NORM_ROWS, D), lambda b, t: (b, t, 0)),
            pl.BlockSpec((1, D), lambda b, t: (0, 0)),
            pl.BlockSpec((1, 1, 1, D), seg),
            pl.BlockSpec((1, 1, 1, D), seg),
            pl.BlockSpec((D, 128), lambda b, t: (0, 0)),
            pl.BlockSpec((D, 128), lambda b, t: (0, 0)),
        ],
        out_specs=[pl.BlockSpec((NORM_ROWS, D // 128, 128), lambda b, t: (b * nt + t, 0, 0)),
                   pl.BlockSpec((NORM_ROWS, 128), lambda b, t: (b * nt + t, 0))],
        compiler_params=_cparams("parallel", "parallel"),
        name="norm_route",
    )(z, norm_w.reshape(1, D), shift, scale, whi, wlo)


def _repack_kernel(w_ref, o_ref, ab_ref):
    x = w_ref[0]
    o_ref[0, :, :AB_START] = x[:, :AB_START].astype(jnp.bfloat16)
    o_ref[0, :, AB_START:] = x[:, AB_END:].astype(jnp.bfloat16)
    piece = x[:, AB_START:AB_START + 128]
    lane = lax.broadcasted_iota(jnp.int32, piece.shape, 1)
    ab_ref[0] = jnp.where(lane < AB_END - AB_START, piece, 0.0).astype(jnp.bfloat16)


def _repack_in_proj(w_in):
    depth, D, n_in = w_in.shape
    tk = 256
    return pl.pallas_call(
        _repack_kernel,
        out_shape=[jax.ShapeDtypeStruct((depth, D, P_COLS), jnp.bfloat16),
                   jax.ShapeDtypeStruct((depth, D, 128), jnp.bfloat16)],
        grid=(depth, D // tk),
        in_specs=[pl.BlockSpec((1, tk, n_in), lambda l, i: (l, i, 0))],
        out_specs=[pl.BlockSpec((1, tk, P_COLS), lambda l, i: (l, i, 0)),
                   pl.BlockSpec((1, tk, 128), lambda l, i: (l, i, 0))],
        compiler_params=_cparams("parallel", "parallel"),
        name="repack_in_proj",
    )(w_in)


def _final_norm_kernel(z_ref, nw_ref, o_ref):
    z = z_ref[0]
    o_ref[0] = z * lax.rsqrt(jnp.mean(z * z, axis=-1, keepdims=True) + NORM_EPS) * nw_ref[...]


def _final_norm(z, norm_w):
    B, LT, D = z.shape
    skip = CTX_LEN // NORM_ROWS
    return pl.pallas_call(
        _final_norm_kernel,
        out_shape=jax.ShapeDtypeStruct((B, LT - CTX_LEN, D), jnp.float32),
        grid=(B, (LT - CTX_LEN) // NORM_ROWS),
        in_specs=[
            pl.BlockSpec((1, NORM_ROWS, D), lambda b, t: (b, t + skip, 0)),
            pl.BlockSpec((1, D), lambda b, t: (0, 0)),
        ],
        out_specs=pl.BlockSpec((1, NORM_ROWS, D), lambda b, t: (b, t, 0)),
        compiler_params=_cparams("parallel", "parallel"),
        name="final_norm",
    )(z, norm_w.reshape(1, D))


def _mm_kernel(a_ref, w_ref, o_ref):
    o_ref[...] = jnp.dot(a_ref[...], w_ref[...].astype(jnp.bfloat16), preferred_element_type=jnp.float32)


def _matmul(a, w, tm, tn, name):
    M, K = a.shape
    N = w.shape[1]
    return pl.pallas_call(
        _mm_kernel,
        out_shape=jax.ShapeDtypeStruct((M, N), jnp.float32),
        grid=(M // tm, N // tn),
        in_specs=[
            pl.BlockSpec((tm, K), lambda i, j: (i, 0)),
            pl.BlockSpec((K, tn), lambda i, j: (0, j)),
        ],
        out_specs=pl.BlockSpec((tm, tn), lambda i, j: (i, j)),
        compiler_params=_cparams("parallel", "arbitrary"),
        name=name,
    )(a, w)


def _resid_mm_kernel(a_ref, w_ref, x_ref, gl_ref, gc_ref, o_ref):
    acc = jnp.dot(a_ref[...], w_ref[...], preferred_element_type=jnp.float32)
    row = lax.broadcasted_iota(jnp.int32, (acc.shape[0], 1), 0)
    gate = jnp.where(row < CTX_LEN, gc_ref[...], gl_ref[0])
    o_ref[...] = x_ref[...] + gate * acc


def _resid_matmul(a, w, x, gate_lat, gate_ctx, LT):
    M, K = a.shape
    N = w.shape[1]
    B = M // LT
    tn = PROJ_COLS
    return pl.pallas_call(
        _resid_mm_kernel,
        out_shape=jax.ShapeDtypeStruct((M, N), jnp.float32),
        grid=(B, N // tn),
        in_specs=[
            pl.BlockSpec((LT, K), lambda i, j: (i, 0)),
            pl.BlockSpec((K, tn), lambda i, j: (0, j)),
            pl.BlockSpec((LT, tn), lambda i, j: (i, j)),
            pl.BlockSpec((1, 1, tn), lambda i, j: (i, 0, j)),
            pl.BlockSpec((1, tn), lambda i, j: (0, j)),
        ],
        out_specs=pl.BlockSpec((LT, tn), lambda i, j: (i, j)),
        compiler_params=_cparams("parallel", "arbitrary"),
        name="out_proj",
    )(a, w, x, gate_lat.reshape(B, 1, N), gate_ctx.reshape(1, N))


def _gelu_exact(x):
    return 0.5 * x * (1.0 + lax.erf(x * (1.0 / math.sqrt(2.0))))


def _sigmoid(x):
    return 1.0 / (1.0 + jnp.exp(-x))


def _merge_kernel(uv_ref, gg_ref, rg_ref, ml_ref, og_ref, or_ref, gmn_ref, ws_ref, bs_ref, gdn_nw_ref,
                  ret_nw_ref, wgm_ref, wgdn_ref, wret_ref, o_ref):
    bf16 = jnp.bfloat16
    f32 = jnp.float32
    rows = uv_ref.shape[0]
    g = _gelu_exact(uv_ref[...])
    u = g[:, :GM_WIDTH]
    v = g[:, GM_WIDTH:]
    d = v - jnp.mean(v, axis=-1, keepdims=True)
    vn = (d * lax.rsqrt(jnp.mean(d * d, axis=-1, keepdims=True) + NORM_EPS) * gmn_ref[...]).astype(bf16)
    chunks = []
    for c in range(rows // GM_CHUNK):
        groups = []
        for gi in range(GM_GROUPS):
            blk = vn[c * GM_CHUNK:(c + 1) * GM_CHUNK, gi * GM_GROUP_CH:(gi + 1) * GM_GROUP_CH]
            groups.append(jnp.dot(ws_ref[gi], blk, preferred_element_type=f32) + bs_ref[gi])
        chunks.append(jnp.concatenate(groups, axis=-1))
    s = jnp.concatenate(chunks, axis=0)
    y_gm = jnp.dot((u * s).astype(bf16), wgm_ref[...], preferred_element_type=f32)

    og = og_ref[...]
    heads = []
    for h in range(GDN_HEADS):
        oh = og[:, h * GDN_HEAD_DIM:(h + 1) * GDN_HEAD_DIM]
        heads.append(oh * lax.rsqrt(jnp.mean(oh * oh, axis=-1, keepdims=True) + NORM_EPS) * gdn_nw_ref[...])
    gg = gg_ref[...]
    a_gdn = (jnp.concatenate(heads, axis=-1) * (gg * _sigmoid(gg))).astype(bf16)
    y_gdn = jnp.dot(a_gdn, wgdn_ref[...], preferred_element_type=f32)

    orr = or_ref[...]
    heads = []
    for h in range(RET_HEADS):
        oh = orr[:, h * RET_VAL_DIM:(h + 1) * RET_VAL_DIM]
        dh = oh - jnp.mean(oh, axis=-1, keepdims=True)
        heads.append(dh * lax.rsqrt(jnp.mean(dh * dh, axis=-1, keepdims=True) + NORM_EPS))
    rg = rg_ref[...]
    a_ret = (jnp.concatenate(heads, axis=-1) * ret_nw_ref[...] * (rg * _sigmoid(rg))).astype(bf16)
    y_ret = jnp.dot(a_ret, wret_ref[...], preferred_element_type=f32)

    sg = _sigmoid(ml_ref[...])
    merged = (sg[:, :D_MODEL] * y_gm + sg[:, D_MODEL:2 * D_MODEL] * y_gdn + sg[:, 2 * D_MODEL:] * y_ret)
    o_ref[...] = merged.astype(o_ref.dtype)


def _merge(p, o_gdn, o_ret, gm_norm_w, ws, bs, gdn_norm_w, ret_norm_w, wgm, wgdn, wret):
    M = p.shape[0]
    tm = MERGE_ROWS
    const = lambda *shape: pl.BlockSpec(shape, lambda i: (0,) * len(shape))
    return pl.pallas_call(
        _merge_kernel,
        out_shape=jax.ShapeDtypeStruct((M, D_MODEL), jnp.bfloat16),
        grid=(M // tm,),
        in_specs=[
            pl.BlockSpec((tm, GM_UV_COLS), lambda i: (i, C_UV // GM_UV_COLS)),
            pl.BlockSpec((tm, GDN_GATE_COLS), lambda i: (i, C_GG // GDN_GATE_COLS)),
            pl.BlockSpec((tm, RET_V_COLS), lambda i: (i, C_RG // RET_V_COLS)),
            pl.BlockSpec((tm, MERGE_COLS), lambda i: (i, C_ML // MERGE_COLS)),
            pl.BlockSpec((tm, GDN_GATE_COLS), lambda i: (i, 0)),
            pl.BlockSpec((tm, RET_V_COLS), lambda i: (i, 0)),
            const(1, GM_WIDTH),
            const(GM_GROUPS, GM_CHUNK, GM_CHUNK),
            const(GM_GROUPS, GM_CHUNK, GM_GROUP_CH),
            const(1, GDN_HEAD_DIM),
            const(1, RET_V_COLS),
            const(GM_WIDTH, D_MODEL),
            const(GDN_GATE_COLS, D_MODEL),
            const(RET_V_COLS, D_MODEL),
        ],
        out_specs=pl.BlockSpec((tm, D_MODEL), lambda i: (i, 0)),
        compiler_params=_cparams("parallel"),
        name="merge",
    )(p, p, p, p, o_gdn, o_ret, gm_norm_w.reshape(1, -1), ws, bs, gdn_norm_w.reshape(1, -1),
      ret_norm_w.reshape(1, -1), wgm, wgdn, wret)


def _moe_kernel(bexp_ref, bcnt_ref, stok_ref, sdst_ref,
                h_hbm, wg_ref, wu_ref, wd_ref, y_hbm,
                xbuf, ybuf, wgb, wub, wdb, gsem, ssem):
    i = pl.program_id(0)
    n = pl.num_programs(0)
    R = MOE_ROWS
    slot = lax.rem(i, 2)

    def gather_copy(blk, r, s):
        return pltpu.make_async_copy(h_hbm.at[stok_ref[blk * R + r]], xbuf.at[s, r], gsem.at[s])

    def scatter_copy(blk, r, s):
        return pltpu.make_async_copy(ybuf.at[s, r], y_hbm.at[sdst_ref[blk * R + r]], ssem.at[s])

    def for_rows(blk, fn):
        cnt = bcnt_ref[blk]
        full = lax.shift_right_logical(cnt, MOE_DMA_GROUP.bit_length() - 1)

        def group(g, c):
            for j in range(MOE_DMA_GROUP):
                fn(g * MOE_DMA_GROUP + j)
            return c
        lax.fori_loop(0, full, group, 0)
        lax.fori_loop(full * MOE_DMA_GROUP, cnt, lambda r, c: (fn(r), c)[1], 0)

    def start_gather(blk, s):
        for_rows(blk, lambda r: gather_copy(blk, r, s).start())

    def wait_gather(blk, s):
        for_rows(blk, lambda r: gather_copy(blk, r, s).wait())

    def start_scatter(blk, s):
        for_rows(blk, lambda r: scatter_copy(blk, r, s).start())

    def wait_scatter(blk, s):
        for_rows(blk, lambda r: scatter_copy(blk, r, s).wait())

    @pl.when(i == 0)
    def _():
        xbuf[...] = jnp.zeros_like(xbuf)
        start_gather(0, 0)

    @pl.when(i + 1 < n)
    def _():
        start_gather(i + 1, 1 - slot)

    changed = jnp.logical_or(i == 0, bexp_ref[i] != bexp_ref[jnp.maximum(i - 1, 0)])

    @pl.when(changed)
    def _():
        wgb[...] = wg_ref[0].astype(jnp.bfloat16)
        wub[...] = wu_ref[0].astype(jnp.bfloat16)
        wdb[...] = wd_ref[0].astype(jnp.bfloat16)

    wait_gather(i, slot)

    @pl.when(i >= 2)
    def _():
        wait_scatter(i - 2, slot)

    xb = _join_rows(xbuf.at[slot]).astype(jnp.bfloat16)
    hg = jnp.dot(xb, wgb[...], preferred_element_type=jnp.float32)
    hu = jnp.dot(xb, wub[...], preferred_element_type=jnp.float32)
    act = (hg * _sigmoid(hg)) * hu
    _split_rows(ybuf.at[slot], jnp.dot(act.astype(jnp.bfloat16), wdb[...], preferred_element_type=jnp.float32))
    start_scatter(i, slot)

    @pl.when(i == n - 1)
    def _():
        @pl.when(n >= 2)
        def _():
            wait_scatter(i - 1, 1 - slot)
        wait_scatter(i, slot)


def _moe(h, block_expert, block_cnt, slot_tok, slot_dst, w_gate, w_up, w_down):
    T, S, _ = h.shape
    D = S * 128
    n_blocks = block_expert.shape[0]
    R = MOE_ROWS
    wmap = lambda i, be, bc, st, sd: (be[i], 0, 0)
    grid_spec = pltpu.PrefetchScalarGridSpec(
        num_scalar_prefetch=4,
        grid=(n_blocks,),
        in_specs=[
            pl.BlockSpec(memory_space=pl.ANY),
            pl.BlockSpec((1, D, MOE_HIDDEN), wmap),
            pl.BlockSpec((1, D, MOE_HIDDEN), wmap),
            pl.BlockSpec((1, MOE_HIDDEN, D), wmap),
        ],
        out_specs=pl.BlockSpec(memory_space=pl.ANY),
        scratch_shapes=[
            pltpu.VMEM((2, R, S, 128), jnp.float32),
            pltpu.VMEM((2, R, S, 128), jnp.float32),
            pltpu.VMEM((D, MOE_HIDDEN), jnp.bfloat16),
            pltpu.VMEM((D, MOE_HIDDEN), jnp.bfloat16),
            pltpu.VMEM((MOE_HIDDEN, D), jnp.bfloat16),
            pltpu.SemaphoreType.DMA((2,)),
            pltpu.SemaphoreType.DMA((2,)),
        ],
    )
    return pl.pallas_call(
        _moe_kernel,
        out_shape=jax.ShapeDtypeStruct((MOE_TOP_K * T, S, 128), jnp.float32),
        grid_spec=grid_spec,
        compiler_params=_cparams("arbitrary"),
        name="moe_experts",
    )(block_expert, block_cnt, slot_tok, slot_dst, h, w_gate, w_up, w_down)


def _combine_kernel(x_ref, y0_ref, y1_ref, w_ref, g_ref, *rest, with_norm):
    w = w_ref[...]
    z = x_ref[0] + g_ref[0, 0] * (w[:, 0:1] * _join_rows(y0_ref) + w[:, 1:2] * _join_rows(y1_ref))
    if not with_norm:
        rest[0][0] = z
        return
    nw_ref, sh_ref, sc_ref, z_ref, h_ref = rest
    z_ref[0] = z
    y = z * lax.rsqrt(jnp.mean(z * z, axis=-1, keepdims=True) + NORM_EPS)
    h_ref[0] = (y * nw_ref[...] * (1.0 + sc_ref[0, 0]) + sh_ref[0, 0]).astype(h_ref.dtype)


def _combine(z, y, weights, gate, norm=None):
    B, LT, D = z.shape
    S = D // 128
    nt = LT // NORM_ROWS
    seg = lambda b, t: (b, jnp.minimum(t, 1), 0, 0)
    tile = pl.BlockSpec((1, NORM_ROWS, D), lambda b, t: (b, t, 0))
    in_specs = [
        tile,
        pl.BlockSpec((NORM_ROWS, S, 128), lambda b, t: (b * nt + t, 0, 0)),
        pl.BlockSpec((NORM_ROWS, S, 128), lambda b, t: (B * nt + b * nt + t, 0, 0)),
        pl.BlockSpec((NORM_ROWS, MOE_TOP_K), lambda b, t: (b * nt + t, 0)),
        pl.BlockSpec((1, 1, 1, D), seg),
    ]
    args = [z, y, y, weights, gate]
    out_shape = [jax.ShapeDtypeStruct((B, LT, D), jnp.float32)]
    out_specs = [tile]
    if norm is not None:
        norm_w, shift, scale = norm
        in_specs += [pl.BlockSpec((1, D), lambda b, t: (0, 0)), pl.BlockSpec((1, 1, 1, D), seg),
                     pl.BlockSpec((1, 1, 1, D), seg)]
        args += [norm_w.reshape(1, D), shift, scale]
        out_shape.append(jax.ShapeDtypeStruct((B, LT, D), jnp.bfloat16))
        out_specs.append(tile)
    out = pl.pallas_call(
        functools.partial(_combine_kernel, with_norm=norm is not None),
        out_shape=out_shape,
        grid=(B, nt),
        in_specs=in_specs,
        out_specs=out_specs,
        compiler_params=_cparams("parallel", "parallel"),
        name="moe_combine",
    )(*args)
    return (out[0], out[1]) if norm is not None else (out[0], None)


def _route(logits, bg, be):
    T = logits.shape[0]
    p_group = jax.nn.softmax(logits[:, :MOE_GROUPS] + bg, axis=-1)
    pg_top, grp = lax.top_k(p_group, 1)
    logits_e = (logits[:, MOE_GROUPS:MOE_GROUPS + MOE_EXPERTS] + be).reshape(T, MOE_GROUPS, MOE_EXPERTS_PER_GROUP)
    sel = jnp.broadcast_to(grp[:, :, None], (T, 1, MOE_EXPERTS_PER_GROUP))
    p_e = jax.nn.softmax(jnp.take_along_axis(logits_e, sel, axis=1)[:, 0], axis=-1)
    pe_top, e_top = lax.top_k(p_e, MOE_TOP_K)
    weights = pg_top * pe_top / jnp.sum(pe_top, axis=-1, keepdims=True)
    return grp * MOE_EXPERTS_PER_GROUP + e_top, weights


def _dispatch_tables(expert_idx):
    T, K = expert_idx.shape
    E, R = MOE_EXPERTS, MOE_ROWS
    A = T * K
    n_blocks = A // R + E
    flat_e = expert_idx.reshape(A).astype(jnp.int32)
    order = jnp.argsort(flat_e).astype(jnp.int32)
    cnt_end = jnp.searchsorted(flat_e[order], jnp.arange(E, dtype=jnp.int32), side='right').astype(jnp.int32)
    cnt_start = jnp.concatenate([jnp.zeros((1,), jnp.int32), cnt_end[:-1]])
    counts = cnt_end - cnt_start
    nblk = (counts + R - 1) // R
    blk_end = jnp.cumsum(nblk)
    blk_start = blk_end - nblk
    blk = jnp.arange(n_blocks, dtype=jnp.int32)
    block_expert = jnp.minimum(jnp.searchsorted(blk_end, blk, side='right'), E - 1).astype(jnp.int32)
    first = (blk - blk_start[block_expert]) * R
    block_cnt = jnp.clip(counts[block_expert] - first, 0, R).astype(jnp.int32)
    row = jnp.arange(R, dtype=jnp.int32)[None, :]
    src = (cnt_start[block_expert] + first)[:, None] + row
    a = order[jnp.clip(src, 0, A - 1)]
    valid = row < block_cnt[:, None]
    slot_tok = jnp.where(valid, a // K, 0).reshape(-1)
    slot_dst = jnp.where(valid, (a % K) * T + a // K, 0).reshape(-1)
    return block_expert, block_cnt, slot_tok, slot_dst


def _bdot(a, b):
    return jnp.dot(a.astype(jnp.bfloat16), b.astype(jnp.bfloat16), preferred_element_type=jnp.float32)


def _bdot_nt(a, b):
    return lax.dot_general(a.astype(jnp.bfloat16), b.astype(jnp.bfloat16), (((1,), (1,)), ((), ())),
                           preferred_element_type=jnp.float32)


def _bdot_tn(a, b):
    return lax.dot_general(a.astype(jnp.bfloat16), b.astype(jnp.bfloat16), (((0,), (0,)), ((), ())),
                           preferred_element_type=jnp.float32)


def _seq_pos(LT):
    row = lax.broadcasted_iota(jnp.int32, (LT, 1), 0)
    in_ctx = row < CTX_LEN
    return jnp.where(in_ctx, row, row - CTX_LEN), jnp.where(in_ctx, CTX_LEN, LT - CTX_LEN)


def _conv_silu(x, cw, pos, seq_len):
    LT = x.shape[0]
    K = cw.shape[0]
    r = K // 2
    out = x * cw[r:r + 1, :]
    for j in range(K):
        s = r - j
        if s == 0:
            continue
        xs = pltpu.roll(x, s % LT, axis=0)
        valid = (pos >= s) if s > 0 else (pos - s < seq_len)
        out = out + jnp.where(valid, xs, 0.0) * cw[j:j + 1, :]
    return out * _sigmoid(out)


def _l2n(t):
    return t * lax.rsqrt(jnp.sum(t * t, axis=-1, keepdims=True) + NORM_EPS)


def _lane_col(x, idx):
    lane = lax.broadcasted_iota(jnp.int32, x.shape, 1)
    return jnp.sum(jnp.where(lane == idx, x, 0.0), axis=1, keepdims=True)


def _inv_unit_tri(Ns, eye, m16, moffs):
    Dg = [N * m16 for N in Ns]
    D2 = [_bdot(d, d) for d in Dg]
    D4 = [_bdot(d, d) for d in D2]
    D8 = [_bdot(d, d) for d in D4]
    T = [eye - d for d in Dg]
    for P in (D2, D4, D8):
        T = [t + _bdot(t, p) for t, p in zip(T, P)]
    for m in moffs:
        X = [_bdot(t, N * m) for t, N in zip(T, Ns)]
        T = [t - _bdot(x, t) for t, x in zip(T, X)]
    return T


def _gdn_kernel(q_ref, k_ref, v_ref, ab_ref, cwq_ref, cwk_ref, cwv_ref, alog_ref, dtb_ref, o_ref,
                qn_s, kn_s, vn_s, gb_s, u_s, w_s, qs_s, kst_s, qkd_s, el_s, o_s):
    f32 = jnp.float32
    bf16 = jnp.bfloat16
    h = pl.program_id(1)
    LT = q_ref.shape[1]
    C = GDN_ROWS
    nch = LT // C
    nctx = CTX_LEN // C

    pos, seq_len = _seq_pos(LT)
    qn_s[...] = _l2n(_conv_silu(q_ref[0], cwq_ref[...], pos, seq_len)) * (GDN_HEAD_DIM ** -0.5)
    kn_s[...] = _l2n(_conv_silu(k_ref[0], cwk_ref[...], pos, seq_len))
    vn_s[...] = _conv_silu(v_ref[0], cwv_ref[...], pos, seq_len)

    ab = ab_ref[0]
    xa = ab + dtb_ref[...]
    g_all = -jnp.exp(alog_ref[...]) * (jnp.maximum(xa, 0.0) + jnp.log1p(jnp.exp(-jnp.abs(xa))))
    b_all = _sigmoid(ab)
    lane = lax.broadcasted_iota(jnp.int32, (LT, 128), 1)
    gb = jnp.where(lane == 0, _lane_col(g_all, h), 0.0)
    gb = jnp.where(lane == 1, _lane_col(g_all, GDN_HEADS + h), gb)
    gb = jnp.where(lane == 2, _lane_col(b_all, 2 * GDN_HEADS + h), gb)
    gb = jnp.where(lane == 3, _lane_col(b_all, 3 * GDN_HEADS + h), gb)
    gb_s[...] = gb

    ri = lax.broadcasted_iota(jnp.int32, (C, C), 0)
    ci = lax.broadcasted_iota(jnp.int32, (C, C), 1)
    eye = (ri == ci).astype(f32)
    low = (ri >= ci).astype(f32)
    upp = (ri <= ci).astype(f32)
    m16 = ((ri // 16) == (ci // 16)).astype(f32)
    moffs = [(((ri // (2 * s)) == (ci // (2 * s))) & ((ri // s) != (ci // s))).astype(f32) for s in (16, 32, 64)]
    hi = lax.Precision.HIGHEST

    def chunk_group(i, carry):
        probs = []
        for j in range(GDN_CHUNKS_PER_STEP):
            c = i * GDN_CHUNKS_PER_STEP + j
            rows = pl.ds(pl.multiple_of(c * C, C), C)
            q = qn_s[rows, :]
            k = kn_s[rows, :]
            v = vn_s[rows, :]
            gbc = gb_s[rows, :]
            kk = _bdot_nt(k, k)
            qk = _bdot_nt(q, k)
            cs_f = jnp.dot(low, gbc, precision=hi, preferred_element_type=f32)
            cs_b = jnp.dot(upp, gbc, precision=hi, preferred_element_type=f32)
            for d in range(2):
                cs = cs_f if d == 0 else cs_b
                gc = cs[:, d:d + 1]
                gc_row = cs.T[d:d + 1, :]
                beta = gbc[:, 2 + d:3 + d]
                incl = low if d == 0 else upp
                decay = jnp.exp(jnp.where(incl > 0, gc - gc_row, -jnp.inf))
                g_last = gc[C - 1:C, :] if d == 0 else gc[0:1, :]
                probs.append(dict(d=d, c=c, q=q, k=k, v=v, qk=qk, gc=gc, beta=beta, decay=decay, g_last=g_last,
                                  N=(beta * kk) * decay * (1.0 - eye)))
        Ts = _inv_unit_tri([p["N"] for p in probs], eye, m16, moffs)
        for p, T in zip(probs, Ts):
            d, c, gc, beta = p["d"], p["c"], p["gc"], p["beta"]
            eg = jnp.exp(gc)
            u_s[d, c] = _bdot(T, p["v"] * beta)
            w_s[d, c] = _bdot(T, p["k"] * beta * eg).astype(bf16)
            qs_s[d, c] = (p["q"] * eg).astype(bf16)
            kst_s[d, c] = (p["k"] * jnp.exp(p["g_last"] - gc)).T.astype(bf16)
            qkd_s[d, c] = (p["qk"] * p["decay"]).astype(bf16)
            el_s[d, c] = jnp.broadcast_to(jnp.exp(p["g_last"]), (8, 128))
        return carry

    lax.fori_loop(0, nch // GDN_CHUNKS_PER_STEP, chunk_group, 0)

    def step(t, S):
        cb = jnp.where(t < nctx, nctx - 1 - t, nch - 1 + nctx - t)
        outs = []
        for d, (c, Sd) in enumerate(((t, S[0]), (cb, S[1]))):
            Sbf = Sd.astype(bf16)
            v_new = u_s[d, c] - jnp.dot(w_s[d, c], Sbf, preferred_element_type=f32)
            vb = v_new.astype(bf16)
            o_s[d, c] = (jnp.dot(qs_s[d, c], Sbf, preferred_element_type=f32)
                         + jnp.dot(qkd_s[d, c], vb, preferred_element_type=f32))
            outs.append(Sd * el_s[d, c][0:1, :] + jnp.dot(kst_s[d, c], vb, preferred_element_type=f32))
        return tuple(outs)

    z = jnp.zeros((GDN_HEAD_DIM, GDN_HEAD_DIM), f32)
    lax.fori_loop(0, nch, step, (z, z))
    o_ref[0] = (o_s[0] + o_s[1]).reshape(LT, GDN_HEAD_DIM)


def _gdn_mixer(p3, ab, conv_w, a_log, dt_bias):
    B, LT, _ = p3.shape
    H, Dh, C = GDN_HEADS, GDN_HEAD_DIM, GDN_ROWS
    nch = LT // C
    qb = C_QKV // Dh
    alog_row = jnp.zeros((1, 128), jnp.float32).at[0, :2 * H].set(a_log.reshape(-1))
    dtb_row = jnp.zeros((1, 128), jnp.float32).at[0, :2 * H].set(dt_bias.reshape(-1))
    taps = conv_w.shape[0]
    per_chunk = lambda dt: pltpu.VMEM((2, nch, C, Dh), dt)
    return pl.pallas_call(
        _gdn_kernel,
        out_shape=jax.ShapeDtypeStruct((B, LT, H * Dh), jnp.float32),
        grid=(B, H),
        in_specs=[
            pl.BlockSpec((1, LT, Dh), lambda b, h: (b, 0, qb + h)),
            pl.BlockSpec((1, LT, Dh), lambda b, h: (b, 0, qb + H + h)),
            pl.BlockSpec((1, LT, Dh), lambda b, h: (b, 0, qb + 2 * H + h)),
            pl.BlockSpec((1, LT, 128), lambda b, h: (b, 0, 0)),
            pl.BlockSpec((taps, Dh), lambda b, h: (0, h)),
            pl.BlockSpec((taps, Dh), lambda b, h: (0, H + h)),
            pl.BlockSpec((taps, Dh), lambda b, h: (0, 2 * H + h)),
            pl.BlockSpec((1, 128), lambda b, h: (0, 0)),
            pl.BlockSpec((1, 128), lambda b, h: (0, 0)),
        ],
        out_specs=pl.BlockSpec((1, LT, Dh), lambda b, h: (b, 0, h)),
        scratch_shapes=[
            pltpu.VMEM((LT, Dh), jnp.float32), pltpu.VMEM((LT, Dh), jnp.float32), pltpu.VMEM((LT, Dh), jnp.float32),
            pltpu.VMEM((LT, 128), jnp.float32),
            per_chunk(jnp.float32), per_chunk(jnp.bfloat16), per_chunk(jnp.bfloat16), per_chunk(jnp.bfloat16),
            per_chunk(jnp.bfloat16),
            pltpu.VMEM((2, nch, 8, 128), jnp.float32),
            per_chunk(jnp.float32),
        ],
        compiler_params=_cparams("parallel", "parallel"),
        name="gdn_mixer",
    )(p3, p3, p3, ab, conv_w, conv_w, conv_w, alog_row, dtb_row)


def _ret_kernel(lg_ref, q_ref, k_ref, v_ref, cos_ref, sin_ref, o_ref, qr_s, kr_s):
    f32 = jnp.float32
    h = pl.program_id(1)
    LT = q_ref.shape[1]
    C = RET_ROWS
    nch = LT // C
    nctx = CTX_LEN // C
    cos = cos_ref[...]
    sin = sin_ref[...]
    half = RET_KEY_DIM // 2
    q = q_ref[0]
    k = k_ref[0]
    qr_s[...] = (q * cos + pltpu.roll(q, half, axis=1) * sin) * (RET_KEY_DIM ** -0.5)
    kr_s[...] = k * cos + pltpu.roll(k, half, axis=1) * sin

    lg_f = lg_ref[0, h]
    lg_b = lg_ref[1, h]
    ri = lax.broadcasted_iota(jnp.int32, (C, C), 0)
    ci = lax.broadcasted_iota(jnp.int32, (C, C), 1)
    diff = (ri - ci).astype(f32)
    dm = (jnp.exp(jnp.where(diff >= 0, lg_f * diff, -jnp.inf))
          + jnp.exp(jnp.where(diff <= 0, -lg_b * diff, -jnp.inf)))
    pcol = lax.broadcasted_iota(jnp.int32, (C, 1), 0).astype(f32)
    cross_f = jnp.exp(lg_f * (pcol + 1.0))
    in_f = jnp.exp(lg_f * (C - 1.0 - pcol))
    cross_b = jnp.exp(lg_b * (C - pcol))
    in_b = jnp.exp(lg_b * pcol)
    cd_f = jnp.exp(lg_f * C)
    cd_b = jnp.exp(lg_b * C)

    def inner(c, carry):
        rows = pl.ds(pl.multiple_of(c * C, C), C)
        s = _bdot_nt(qr_s[rows, :], kr_s[rows, :])
        o_ref[0, rows, :] = _bdot(s * dm, v_ref[0, rows, :])
        return carry

    lax.fori_loop(0, nch, inner, 0)

    def step(t, S):
        cb = jnp.where(t < nctx, nctx - 1 - t, nch - 1 + nctx - t)
        outs = []
        for c, Sd, cr, ind, cd in ((t, S[0], cross_f, in_f, cd_f), (cb, S[1], cross_b, in_b, cd_b)):
            rows = pl.ds(pl.multiple_of(c * C, C), C)
            o_ref[0, rows, :] += _bdot(qr_s[rows, :], Sd) * cr
            outs.append(Sd * cd + _bdot_tn(kr_s[rows, :] * ind, v_ref[0, rows, :]))
        return tuple(outs)

    z = jnp.zeros((RET_KEY_DIM, RET_VAL_DIM), f32)
    lax.fori_loop(0, nch, step, (z, z))


def _ret_mixer(p3, log_gamma, cos_t, sin_t):
    B, LT, _ = p3.shape
    H, Dk, Dv = RET_HEADS, RET_KEY_DIM, RET_VAL_DIM
    grid_spec = pltpu.PrefetchScalarGridSpec(
        num_scalar_prefetch=1,
        grid=(B, H),
        in_specs=[
            pl.BlockSpec((1, LT, Dk), lambda b, h, lg: (b, 0, C_RQ // Dk + h)),
            pl.BlockSpec((1, LT, Dk), lambda b, h, lg: (b, 0, C_RK // Dk + h)),
            pl.BlockSpec((1, LT, Dv), lambda b, h, lg: (b, 0, C_RV // Dv + h)),
            pl.BlockSpec((LT, Dk), lambda b, h, lg: (0, 0)),
            pl.BlockSpec((LT, Dk), lambda b, h, lg: (0, 0)),
        ],
        out_specs=pl.BlockSpec((1, LT, Dv), lambda b, h, lg: (b, 0, h)),
        scratch_shapes=[pltpu.VMEM((LT, Dk), jnp.float32), pltpu.VMEM((LT, Dk), jnp.float32)],
    )
    return pl.pallas_call(
        _ret_kernel,
        out_shape=jax.ShapeDtypeStruct((B, LT, H * Dv), jnp.float32),
        grid_spec=grid_spec,
        compiler_params=_cparams("parallel", "parallel"),
        name="ret_mixer",
    )(log_gamma, p3, p3, p3, cos_t, sin_t)


def _axial_rope(rows, dim):
    n = dim // 4
    freq = ROPE_BASE ** (-jnp.arange(n, dtype=jnp.float32) / n)
    row = jnp.repeat(jnp.arange(rows, dtype=jnp.float32), GRID_W)
    col = (jnp.arange(rows * GRID_W) % GRID_W).astype(jnp.float32)
    ang = jnp.concatenate([row[:, None] * freq, col[:, None] * freq], axis=-1)
    ang = jnp.concatenate([ang, ang], axis=-1)[:, None, :]
    return jnp.cos(ang), jnp.sin(ang)


def _rope_tables(L):
    cos, sin = _axial_rope(L // GRID_W, RET_KEY_DIM)
    half = RET_KEY_DIM // 2
    sign = jnp.concatenate([-jnp.ones((half,), jnp.float32), jnp.ones((half,), jnp.float32)])
    cos_t = jnp.concatenate([jnp.ones((CTX_LEN, RET_KEY_DIM), jnp.float32), cos[:, 0, :]], axis=0)
    sin_t = jnp.concatenate([jnp.zeros((CTX_LEN, RET_KEY_DIM), jnp.float32), sin[:, 0, :] * sign], axis=0)
    return cos_t, sin_t


def _seg_table(ctx_vec, lat_vec):
    B, D = lat_vec.shape
    return jnp.stack([jnp.broadcast_to(ctx_vec, (B, D)), lat_vec], axis=1).reshape(B, 2, 1, D)


def kernel(x, c, ctx, c_ctx, mod_w, mod_b, norm1_w, w_in, gm_norm_w, gm_spatial_w, gm_spatial_b, gdn_conv_w, gdn_a_log, gdn_dt_bias, gdn_norm_w, ret_decay_logit, ret_norm_w, w_br_gm, w_br_gdn, w_br_ret, w_out, norm2_w, router_group_w, router_group_b, router_expert_w, router_expert_b, moe_w_gate, moe_w_up, moe_w_down, final_norm_w):
    B, L, D = x.shape
    depth = mod_w.shape[0]
    LT = CTX_LEN + L
    T = B * LT
    bf16 = jnp.bfloat16
    cos_t, sin_t = _rope_tables(L)

    w_main, w_ab = _repack_in_proj(w_in)
    wgm_b, wgdn_b, wret_b, wout_b = (t.astype(bf16) for t in (w_br_gm, w_br_gdn, w_br_ret, w_out))
    ws_b = gm_spatial_w.astype(bf16)
    bs_full = jnp.broadcast_to(gm_spatial_b[..., None], gm_spatial_b.shape + (GM_GROUP_CH,))
    w_router = jnp.concatenate([router_group_w, router_expert_w,
                                jnp.zeros((depth, D, 128 - MOE_GROUPS - MOE_EXPERTS), jnp.float32)], axis=-1)

    cc = jnp.concatenate([jax.nn.silu(c), jax.nn.silu(c_ctx)[None], jnp.zeros((16 - B - 1, D), c.dtype)], axis=0)
    cc = cc.astype(bf16)
    mods = []
    for l in range(depth):
        mod = _matmul(cc, mod_w[l], 16, 1024, "adaln_mod") + mod_b[l]
        mods.append([_seg_table(mod[B, i * D:(i + 1) * D], mod[:B, i * D:(i + 1) * D]) for i in range(6)])

    z = jnp.concatenate([ctx, x], axis=1)
    h = _normmod(z, norm1_w[0], mods[0][0], mods[0][1], bf16)
    for l in range(depth):
        sh1, sc1, gt1, sh2, sc2, gt2 = mods[l]
        log_gamma = jax.nn.log_sigmoid(ret_decay_logit[l])

        h = h.reshape(T, D)
        p = _matmul(h, w_main[l], LT, PROJ_COLS, "in_proj")
        pab = _matmul(h, w_ab[l], LT, 128, "in_proj_ab")
        p3 = p.reshape(B, LT, P_COLS)
        o_gdn = _gdn_mixer(p3, pab.reshape(B, LT, 128), gdn_conv_w[l], gdn_a_log[l], gdn_dt_bias[l])
        o_ret = _ret_mixer(p3, log_gamma, cos_t, sin_t)
        o_gdn = o_gdn.reshape(T, GDN_GATE_COLS)
        o_ret = o_ret.reshape(T, RET_V_COLS)

        merged = _merge(p, o_gdn, o_ret, gm_norm_w[l], ws_b[l], bs_full[l], gdn_norm_w[l], ret_norm_w[l],
                        wgm_b[l], wgdn_b[l], wret_b[l])
        z = _resid_matmul(merged, wout_b[l], z.reshape(T, D), gt1[:, 1, 0], gt1[0, 0, 0], LT).reshape(B, LT, D)

        h2, logits = _norm_route(z, norm2_w[l], sh2, sc2, w_router[l])
        expert_idx, weights = _route(logits, router_group_b[l], router_expert_b[l])
        y = _moe(h2, *_dispatch_tables(expert_idx), moe_w_gate[l], moe_w_up[l], moe_w_down[l])
        nxt = (norm1_w[l + 1], mods[l + 1][0], mods[l + 1][1]) if l + 1 < depth else None
        z, h = _combine(z, y, weights, gt2, nxt)
    return _final_norm(z, final_norm_w)
```

```python
import functools
import math

import jax
import jax.numpy as jnp
from jax import lax
from jax.experimental import pallas as pl
from jax.experimental.pallas import tpu as pltpu

D_MODEL = 2048
CTX_LEN = 256
GRID_W = 64
NORM_EPS = 1e-6

GM_CHUNK = 128
GM_GROUPS = 4
GM_GROUP_CH = 128
GM_WIDTH = GM_GROUPS * GM_GROUP_CH
GDN_HEADS = 4
GDN_HEAD_DIM = 128
GDN_ROWS = 128
GDN_CHUNKS_PER_STEP = 6
RET_HEADS = 4
RET_KEY_DIM = 128
RET_VAL_DIM = 256
RET_ROWS = 256
ROPE_BASE = 10000.0
MOE_GROUPS = 4
MOE_EXPERTS_PER_GROUP = 8
MOE_EXPERTS = MOE_GROUPS * MOE_EXPERTS_PER_GROUP
MOE_TOP_K = 2
MOE_HIDDEN = 512

GM_UV_COLS = 2 * GM_WIDTH
GDN_QKV_COLS = 3 * GDN_HEADS * GDN_HEAD_DIM
GDN_AB_COLS = 2 * GDN_HEADS
GDN_GATE_COLS = GDN_HEADS * GDN_HEAD_DIM
RET_QK_COLS = RET_HEADS * RET_KEY_DIM
RET_V_COLS = RET_HEADS * RET_VAL_DIM
MERGE_COLS = 3 * D_MODEL
AB_START = GM_UV_COLS + GDN_QKV_COLS
AB_END = AB_START + 2 * GDN_AB_COLS

C_UV = 0
C_QKV = C_UV + GM_UV_COLS
C_RQ = C_QKV + GDN_QKV_COLS
C_RK = C_RQ + RET_QK_COLS
C_RV = C_RK + RET_QK_COLS
C_GG = C_RV + RET_V_COLS
C_RG = C_GG + GDN_GATE_COLS
C_ML = C_RG + RET_V_COLS
P_COLS = C_ML + MERGE_COLS

V7X_VMEM_LIMIT_BYTES = 56 * 1024 * 1024
MOE_ROWS = 256
MOE_DMA_GROUP = 8
MERGE_ROWS = 256
NORM_ROWS = 256
PROJ_COLS = 512


def _cparams(*sem):
    return pltpu.CompilerParams(dimension_semantics=sem, vmem_limit_bytes=V7X_VMEM_LIMIT_BYTES)


def _normmod_kernel(z_ref, nw_ref, sh_ref, sc_ref, o_ref):
    z = z_ref[0]
    y = z * lax.rsqrt(jnp.mean(z * z, axis=-1, keepdims=True) + NORM_EPS)
    y = y * nw_ref[...]
    o_ref[0] = (y * (1.0 + sc_ref[0, 0]) + sh_ref[0, 0]).astype(o_ref.dtype)


def _normmod(z, norm_w, shift, scale, out_dtype):
    B, LT, D = z.shape
    seg = lambda b, t: (b, jnp.minimum(t, 1), 0, 0)
    assert CTX_LEN == NORM_ROWS
    return pl.pallas_call(
        _normmod_kernel,
        out_shape=jax.ShapeDtypeStruct((B, LT, D), out_dtype),
        grid=(B, LT // NORM_ROWS),
        in_specs=[
            pl.BlockSpec((1, NORM_ROWS, D), lambda b, t: (b, t, 0)),
            pl.BlockSpec((1, D), lambda b, t: (0, 0)),
            pl.BlockSpec((1, 1, 1, D), seg),
            pl.BlockSpec((1, 1, 1, D), seg),
        ],
        out_specs=pl.BlockSpec((1, NORM_ROWS, D), lambda b, t: (b, t, 0)),
        compiler_params=_cparams("parallel", "parallel"),
        name="normmod",
    )(z, norm_w.reshape(1, D), shift, scale)


def _split_rows(ref, x):
    for c in range(x.shape[1] // 128):
        ref[:, c, :] = x[:, c * 128:(c + 1) * 128]


def _join_rows(ref):
    return jnp.concatenate([ref[:, c, :] for c in range(ref.shape[1])], axis=-1)


def _route_rows(logits):
    neg = jnp.float32(-3.0e38)
    lane = lax.broadcasted_iota(jnp.int32, logits.shape, 1)
    lane_f = lane.astype(jnp.float32)
    first_max = lambda v: jnp.min(jnp.where(v == jnp.max(v, axis=1, keepdims=True), lane_f, 128.0), axis=1,
                                  keepdims=True)
    is_g = lane < MOE_GROUPS
    gl = jnp.where(is_g, logits, neg)
    g_max = jnp.max(gl, axis=1, keepdims=True)
    grp = first_max(gl).astype(jnp.int32)
    p_top = 1.0 / jnp.sum(jnp.where(is_g, jnp.exp(logits - g_max), 0.0), axis=1, keepdims=True)
    e_lane = lane - MOE_GROUPS
    in_grp = (e_lane >= 0) & (e_lane < MOE_EXPERTS) & ((e_lane // MOE_EXPERTS_PER_GROUP) == grp)
    el = jnp.where(in_grp, logits, neg)
    m1 = jnp.max(el, axis=1, keepdims=True)
    i1 = first_max(el)
    el2 = jnp.where(lane_f == i1, neg, el)
    m2 = jnp.max(el2, axis=1, keepdims=True)
    i2 = first_max(el2)
    r = jnp.exp(m2 - m1)
    w1 = p_top / (1.0 + r)
    out = jnp.where(lane == 0, i1 - MOE_GROUPS, 0.0)
    out = jnp.where(lane == 1, i2 - MOE_GROUPS, out)
    out = jnp.where(lane == 2, w1, out)
    return jnp.where(lane == 3, w1 * r, out)


def _norm_route_kernel(z_ref, nw_ref, sh_ref, sc_ref, whi_ref, wlo_ref, rb_ref, h_ref, rt_ref):
    z = z_ref[0]
    y = z * lax.rsqrt(jnp.mean(z * z, axis=-1, keepdims=True) + NORM_EPS)
    h = y * nw_ref[...] * (1.0 + sc_ref[0, 0]) + sh_ref[0, 0]
    _split_rows(h_ref, h)
    hi = h.astype(jnp.bfloat16)
    lo = (h - hi.astype(jnp.float32)).astype(jnp.bfloat16)
    whi = whi_ref[...]
    logits = (jnp.dot(hi, whi, preferred_element_type=jnp.float32)
              + jnp.dot(lo, whi, preferred_element_type=jnp.float32)
              + jnp.dot(hi, wlo_ref[...], preferred_element_type=jnp.float32))
    rt_ref[...] = _route_rows(logits + rb_ref[...])


def _norm_route(z, norm_w, shift, scale, w_router, b_router):
    B, LT, D = z.shape
    nt = LT // NORM_ROWS
    seg = lambda b, t: (b, jnp.minimum(t, 1), 0, 0)
    whi = w_router.astype(jnp.bfloat16)
    wlo = (w_router - whi.astype(jnp.float32)).astype(jnp.bfloat16)
    return pl.pallas_call(
        _norm_route_kernel,
        out_shape=[jax.ShapeDtypeStruct((B * LT, D // 128, 128), jnp.float32),
                   jax.ShapeDtypeStruct((B * LT, 128), jnp.float32)],
        grid=(B, nt),
        in_specs=[
            pl.BlockSpec((1, NORM_ROWS, D), lambda b, t: (b, t, 0)),
            pl.BlockSpec((1, D), lambda b, t: (0, 0)),
            pl.BlockSpec((1, 1, 1, D), seg),
            pl.BlockSpec((1, 1, 1, D), seg),
            pl.BlockSpec((D, 128), lambda b, t: (0, 0)),
            pl.BlockSpec((D, 128), lambda b, t: (0, 0)),
            pl.BlockSpec((1, 128), lambda b, t: (0, 0)),
        ],
        out_specs=[pl.BlockSpec((NORM_ROWS, D // 128, 128), lambda b, t: (b * nt + t, 0, 0)),
                   pl.BlockSpec((NORM_ROWS, 128), lambda b, t: (b * nt + t, 0))],
        compiler_params=_cparams("parallel", "parallel"),
        name="norm_route",
    )(z, norm_w.reshape(1, D), shift, scale, whi, wlo, b_router.reshape(1, 128))


def _final_norm_kernel(z_ref, nw_ref, o_ref):
    z = z_ref[0]
    o_ref[0] = z * lax.rsqrt(jnp.mean(z * z, axis=-1, keepdims=True) + NORM_EPS) * nw_ref[...]


def _final_norm(z, norm_w):
    B, LT, D = z.shape
    skip = CTX_LEN // NORM_ROWS
    return pl.pallas_call(
        _final_norm_kernel,
        out_shape=jax.ShapeDtypeStruct((B, LT - CTX_LEN, D), jnp.float32),
        grid=(B, (LT - CTX_LEN) // NORM_ROWS),
        in_specs=[
            pl.BlockSpec((1, NORM_ROWS, D), lambda b, t: (b, t + skip, 0)),
            pl.BlockSpec((1, D), lambda b, t: (0, 0)),
        ],
        out_specs=pl.BlockSpec((1, NORM_ROWS, D), lambda b, t: (b, t, 0)),
        compiler_params=_cparams("parallel", "parallel"),
        name="final_norm",
    )(z, norm_w.reshape(1, D))


def _mm_kernel(a_ref, w_ref, o_ref):
    o_ref[...] = jnp.dot(a_ref[...], w_ref[...].astype(jnp.bfloat16), preferred_element_type=jnp.float32)


def _matmul(a, w, layer, tm, tn, name):
    M, K = a.shape
    N = w.shape[2]
    return pl.pallas_call(
        _mm_kernel,
        out_shape=jax.ShapeDtypeStruct((M, N), jnp.float32),
        grid=(M // tm, N // tn),
        in_specs=[
            pl.BlockSpec((tm, K), lambda i, j: (i, 0)),
            pl.BlockSpec((None, K, tn), lambda i, j: (layer, 0, j)),
        ],
        out_specs=pl.BlockSpec((tm, tn), lambda i, j: (i, j)),
        compiler_params=_cparams("parallel", "arbitrary"),
        name=name,
    )(a, w)


def _mm_nt_kernel(a_ref, w_ref, o_ref):
    w = w_ref[0].astype(jnp.bfloat16)
    o_ref[...] = lax.dot_general(a_ref[...], w, (((1,), (1,)), ((), ())), preferred_element_type=jnp.float32)


def _in_proj(a, w_t, layer, tm, tn, n_out, first_row, name):
    M, K = a.shape
    wmap = lambda i, j: (layer, pl.multiple_of(first_row(j), 8), 0)
    return pl.pallas_call(
        _mm_nt_kernel,
        out_shape=jax.ShapeDtypeStruct((M, n_out), jnp.float32),
        grid=(M // tm, n_out // tn),
        in_specs=[
            pl.BlockSpec((tm, K), lambda i, j: (i, 0)),
            pl.BlockSpec((pl.Element(1), pl.Element(tn), pl.Element(K)), wmap),
        ],
        out_specs=pl.BlockSpec((tm, tn), lambda i, j: (i, j)),
        compiler_params=_cparams("parallel", "arbitrary"),
        name=name,
    )(a, w_t)


def _resid_mm_kernel(a_ref, w_ref, x_ref, gl_ref, gc_ref, o_ref):
    acc = jnp.dot(a_ref[...], w_ref[...], preferred_element_type=jnp.float32)
    row = lax.broadcasted_iota(jnp.int32, (acc.shape[0], 1), 0)
    gate = jnp.where(row < CTX_LEN, gc_ref[...], gl_ref[0])
    o_ref[...] = x_ref[...] + gate * acc


def _resid_matmul(a, w, x, gate_lat, gate_ctx, LT):
    M, K = a.shape
    N = w.shape[1]
    B = M // LT
    tn = PROJ_COLS
    return pl.pallas_call(
        _resid_mm_kernel,
        out_shape=jax.ShapeDtypeStruct((M, N), jnp.float32),
        grid=(B, N // tn),
        in_specs=[
            pl.BlockSpec((LT, K), lambda i, j: (i, 0)),
            pl.BlockSpec((K, tn), lambda i, j: (0, j)),
            pl.BlockSpec((LT, tn), lambda i, j: (i, j)),
            pl.BlockSpec((1, 1, tn), lambda i, j: (i, 0, j)),
            pl.BlockSpec((1, tn), lambda i, j: (0, j)),
        ],
        out_specs=pl.BlockSpec((LT, tn), lambda i, j: (i, j)),
        compiler_params=_cparams("parallel", "arbitrary"),
        name="out_proj",
    )(a, w, x, gate_lat.reshape(B, 1, N), gate_ctx.reshape(1, N))


def _gelu_exact(x):
    return 0.5 * x * (1.0 + lax.erf(x * (1.0 / math.sqrt(2.0))))


def _sigmoid(x):
    return 1.0 / (1.0 + jnp.exp(-x))


def _merge_kernel(uv_ref, gg_ref, rg_ref, ml_ref, og_ref, or_ref, gmn_ref, ws_ref, bs_ref, gdn_nw_ref,
                  ret_nw_ref, wgm_ref, wgdn_ref, wret_ref, o_ref):
    bf16 = jnp.bfloat16
    f32 = jnp.float32
    rows = uv_ref.shape[0]
    g = _gelu_exact(uv_ref[...])
    u = g[:, :GM_WIDTH]
    v = g[:, GM_WIDTH:]
    d = v - jnp.mean(v, axis=-1, keepdims=True)
    vn = (d * lax.rsqrt(jnp.mean(d * d, axis=-1, keepdims=True) + NORM_EPS) * gmn_ref[...]).astype(bf16)
    chunks = []
    for c in range(rows // GM_CHUNK):
        groups = []
        for gi in range(GM_GROUPS):
            blk = vn[c * GM_CHUNK:(c + 1) * GM_CHUNK, gi * GM_GROUP_CH:(gi + 1) * GM_GROUP_CH]
            groups.append(jnp.dot(ws_ref[gi], blk, preferred_element_type=f32) + bs_ref[gi])
        chunks.append(jnp.concatenate(groups, axis=-1))
    s = jnp.concatenate(chunks, axis=0)
    y_gm = jnp.dot((u * s).astype(bf16), wgm_ref[...], preferred_element_type=f32)

    og = og_ref[...]
    heads = []
    for h in range(GDN_HEADS):
        oh = og[:, h * GDN_HEAD_DIM:(h + 1) * GDN_HEAD_DIM]
        heads.append(oh * lax.rsqrt(jnp.mean(oh * oh, axis=-1, keepdims=True) + NORM_EPS) * gdn_nw_ref[...])
    gg = gg_ref[...]
    a_gdn = (jnp.concatenate(heads, axis=-1) * (gg * _sigmoid(gg))).astype(bf16)
    y_gdn = jnp.dot(a_gdn, wgdn_ref[...], preferred_element_type=f32)

    orr = or_ref[...]
    heads = []
    for h in range(RET_HEADS):
        oh = orr[:, h * RET_VAL_DIM:(h + 1) * RET_VAL_DIM]
        dh = oh - jnp.mean(oh, axis=-1, keepdims=True)
        heads.append(dh * lax.rsqrt(jnp.mean(dh * dh, axis=-1, keepdims=True) + NORM_EPS))
    rg = rg_ref[...]
    a_ret = (jnp.concatenate(heads, axis=-1) * ret_nw_ref[...] * (rg * _sigmoid(rg))).astype(bf16)
    y_ret = jnp.dot(a_ret, wret_ref[...], preferred_element_type=f32)

    sg = _sigmoid(ml_ref[...])
    merged = (sg[:, :D_MODEL] * y_gm + sg[:, D_MODEL:2 * D_MODEL] * y_gdn + sg[:, 2 * D_MODEL:] * y_ret)
    o_ref[...] = merged.astype(o_ref.dtype)


def _merge(p, o_gdn, o_ret, gm_norm_w, ws, bs, gdn_norm_w, ret_norm_w, wgm, wgdn, wret):
    M = p.shape[0]
    tm = MERGE_ROWS
    const = lambda *shape: pl.BlockSpec(shape, lambda i: (0,) * len(shape))
    return pl.pallas_call(
        _merge_kernel,
        out_shape=jax.ShapeDtypeStruct((M, D_MODEL), jnp.bfloat16),
        grid=(M // tm,),
        in_specs=[
            pl.BlockSpec((tm, GM_UV_COLS), lambda i: (i, C_UV // GM_UV_COLS)),
            pl.BlockSpec((tm, GDN_GATE_COLS), lambda i: (i, C_GG // GDN_GATE_COLS)),
            pl.BlockSpec((tm, RET_V_COLS), lambda i: (i, C_RG // RET_V_COLS)),
            pl.BlockSpec((tm, MERGE_COLS), lambda i: (i, C_ML // MERGE_COLS)),
            pl.BlockSpec((tm, GDN_GATE_COLS), lambda i: (i, 0)),
            pl.BlockSpec((tm, RET_V_COLS), lambda i: (i, 0)),
            const(1, GM_WIDTH),
            const(GM_GROUPS, GM_CHUNK, GM_CHUNK),
            const(GM_GROUPS, GM_CHUNK, GM_GROUP_CH),
            const(1, GDN_HEAD_DIM),
            const(1, RET_V_COLS),
            const(GM_WIDTH, D_MODEL),
            const(GDN_GATE_COLS, D_MODEL),
            const(RET_V_COLS, D_MODEL),
        ],
        out_specs=pl.BlockSpec((tm, D_MODEL), lambda i: (i, 0)),
        compiler_params=_cparams("parallel"),
        name="merge",
    )(p, p, p, p, o_gdn, o_ret, gm_norm_w.reshape(1, -1), ws, bs, gdn_norm_w.reshape(1, -1),
      ret_norm_w.reshape(1, -1), wgm, wgdn, wret)


def _moe_kernel(bexp_ref, bcnt_ref, stok_ref, sdst_ref,
                h_hbm, wg_ref, wu_ref, wd_ref, y_hbm,
                xbuf, ybuf, wgb, wub, wdb, gsem, ssem):
    i = pl.program_id(0)
    n = pl.num_programs(0)
    R = MOE_ROWS
    slot = lax.rem(i, 2)

    def gather_copy(blk, r, s):
        return pltpu.make_async_copy(h_hbm.at[stok_ref[blk * R + r]], xbuf.at[s, :, r, :], gsem.at[s])

    def scatter_copy(blk, r, s):
        return pltpu.make_async_copy(ybuf.at[s, :, r, :], y_hbm.at[sdst_ref[blk * R + r]], ssem.at[s])

    def for_rows(blk, fn):
        cnt = bcnt_ref[blk]
        full = lax.shift_right_logical(cnt, MOE_DMA_GROUP.bit_length() - 1)

        def group(g, c):
            for j in range(MOE_DMA_GROUP):
                fn(g * MOE_DMA_GROUP + j)
            return c
        lax.fori_loop(0, full, group, 0)
        lax.fori_loop(full * MOE_DMA_GROUP, cnt, lambda r, c: (fn(r), c)[1], 0)

    def start_gather(blk, s):
        for_rows(blk, lambda r: gather_copy(blk, r, s).start())

    def wait_gather(blk, s):
        for_rows(blk, lambda r: gather_copy(blk, r, s).wait())

    def start_scatter(blk, s):
        for_rows(blk, lambda r: scatter_copy(blk, r, s).start())

    def wait_scatter(blk, s):
        for_rows(blk, lambda r: scatter_copy(blk, r, s).wait())

    @pl.when(i == 0)
    def _():
        xbuf[...] = jnp.zeros_like(xbuf)
        start_gather(0, 0)

    @pl.when(i + 1 < n)
    def _():
        start_gather(i + 1, 1 - slot)

    changed = jnp.logical_or(i == 0, bexp_ref[i] != bexp_ref[jnp.maximum(i - 1, 0)])

    @pl.when(changed)
    def _():
        wgb[...] = wg_ref[0].astype(jnp.bfloat16)
        wub[...] = wu_ref[0].astype(jnp.bfloat16)
        wdb[...] = wd_ref[0].astype(jnp.bfloat16)

    wait_gather(i, slot)

    @pl.when(i >= 2)
    def _():
        wait_scatter(i - 2, slot)

    ncol = xbuf.shape[1]
    xb = jnp.concatenate([xbuf[slot, c] for c in range(ncol)], axis=-1).astype(jnp.bfloat16)
    hg = jnp.dot(xb, wgb[...], preferred_element_type=jnp.float32)
    hu = jnp.dot(xb, wub[...], preferred_element_type=jnp.float32)
    act = (hg * _sigmoid(hg)) * hu
    y = jnp.dot(act.astype(jnp.bfloat16), wdb[...], preferred_element_type=jnp.float32)
    for c in range(ncol):
        ybuf[slot, c] = y[:, c * 128:(c + 1) * 128]
    start_scatter(i, slot)

    @pl.when(i == n - 1)
    def _():
        @pl.when(n >= 2)
        def _():
            wait_scatter(i - 1, 1 - slot)
        wait_scatter(i, slot)


def _moe(h, block_expert, block_cnt, slot_tok, slot_dst, w_gate, w_up, w_down):
    T, S, _ = h.shape
    D = S * 128
    n_blocks = block_expert.shape[0]
    R = MOE_ROWS
    wmap = lambda i, be, bc, st, sd: (be[i], 0, 0)
    grid_spec = pltpu.PrefetchScalarGridSpec(
        num_scalar_prefetch=4,
        grid=(n_blocks,),
        in_specs=[
            pl.BlockSpec(memory_space=pl.ANY),
            pl.BlockSpec((1, D, MOE_HIDDEN), wmap),
            pl.BlockSpec((1, D, MOE_HIDDEN), wmap),
            pl.BlockSpec((1, MOE_HIDDEN, D), wmap),
        ],
        out_specs=pl.BlockSpec(memory_space=pl.ANY),
        scratch_shapes=[
            pltpu.VMEM((2, S, R, 128), jnp.float32),
            pltpu.VMEM((2, S, R, 128), jnp.float32),
            pltpu.VMEM((D, MOE_HIDDEN), jnp.bfloat16),
            pltpu.VMEM((D, MOE_HIDDEN), jnp.bfloat16),
            pltpu.VMEM((MOE_HIDDEN, D), jnp.bfloat16),
            pltpu.SemaphoreType.DMA((2,)),
            pltpu.SemaphoreType.DMA((2,)),
        ],
    )
    return pl.pallas_call(
        _moe_kernel,
        out_shape=jax.ShapeDtypeStruct((MOE_TOP_K * T, S, 128), jnp.float32),
        grid_spec=grid_spec,
        compiler_params=_cparams("arbitrary"),
        name="moe_experts",
    )(block_expert, block_cnt, slot_tok, slot_dst, h, w_gate, w_up, w_down)


def _combine_kernel(x_ref, y0_ref, y1_ref, w_ref, g_ref, *rest, with_norm):
    w = w_ref[...]
    z = x_ref[0] + g_ref[0, 0] * (w[:, 0:1] * _join_rows(y0_ref) + w[:, 1:2] * _join_rows(y1_ref))
    if not with_norm:
        rest[0][0] = z
        return
    nw_ref, sh_ref, sc_ref, z_ref, h_ref = rest
    z_ref[0] = z
    y = z * lax.rsqrt(jnp.mean(z * z, axis=-1, keepdims=True) + NORM_EPS)
    h_ref[0] = (y * nw_ref[...] * (1.0 + sc_ref[0, 0]) + sh_ref[0, 0]).astype(h_ref.dtype)


def _combine(z, y, weights, gate, norm=None):
    B, LT, D = z.shape
    S = D // 128
    nt = LT // NORM_ROWS
    seg = lambda b, t: (b, jnp.minimum(t, 1), 0, 0)
    tile = pl.BlockSpec((1, NORM_ROWS, D), lambda b, t: (b, t, 0))
    in_specs = [
        tile,
        pl.BlockSpec((NORM_ROWS, S, 128), lambda b, t: (b * nt + t, 0, 0)),
        pl.BlockSpec((NORM_ROWS, S, 128), lambda b, t: (B * nt + b * nt + t, 0, 0)),
        pl.BlockSpec((NORM_ROWS, MOE_TOP_K), lambda b, t: (b * nt + t, 0)),
        pl.BlockSpec((1, 1, 1, D), seg),
    ]
    args = [z, y, y, weights, gate]
    out_shape = [jax.ShapeDtypeStruct((B, LT, D), jnp.float32)]
    out_specs = [tile]
    if norm is not None:
        norm_w, shift, scale = norm
        in_specs += [pl.BlockSpec((1, D), lambda b, t: (0, 0)), pl.BlockSpec((1, 1, 1, D), seg),
                     pl.BlockSpec((1, 1, 1, D), seg)]
        args += [norm_w.reshape(1, D), shift, scale]
        out_shape.append(jax.ShapeDtypeStruct((B, LT, D), jnp.bfloat16))
        out_specs.append(tile)
    out = pl.pallas_call(
        functools.partial(_combine_kernel, with_norm=norm is not None),
        out_shape=out_shape,
        grid=(B, nt),
        in_specs=in_specs,
        out_specs=out_specs,
        compiler_params=_cparams("parallel", "parallel"),
        name="moe_combine",
    )(*args)
    return (out[0], out[1]) if norm is not None else (out[0], None)


def _dispatch_tables(expert_idx):
    T, K = expert_idx.shape
    E, R = MOE_EXPERTS, MOE_ROWS
    A = T * K
    n_blocks = A // R + E
    flat_e = expert_idx.reshape(A).astype(jnp.int32)
    order = jnp.argsort(flat_e).astype(jnp.int32)
    cnt_end = jnp.sum(flat_e[None, :] <= jnp.arange(E, dtype=jnp.int32)[:, None], axis=1, dtype=jnp.int32)
    cnt_start = jnp.concatenate([jnp.zeros((1,), jnp.int32), cnt_end[:-1]])
    counts = cnt_end - cnt_start
    nblk = (counts + R - 1) // R
    blk_end = jnp.cumsum(nblk)
    blk_start = blk_end - nblk
    blk = jnp.arange(n_blocks, dtype=jnp.int32)
    block_expert = jnp.minimum(jnp.sum(blk_end[None, :] <= blk[:, None], axis=1, dtype=jnp.int32), E - 1)
    first = (blk - blk_start[block_expert]) * R
    block_cnt = jnp.clip(counts[block_expert] - first, 0, R).astype(jnp.int32)
    row = jnp.arange(R, dtype=jnp.int32)[None, :]
    src = (cnt_start[block_expert] + first)[:, None] + row
    a = order[jnp.clip(src, 0, A - 1)]
    valid = row < block_cnt[:, None]
    slot_tok = jnp.where(valid, a // K, 0).reshape(-1)
    slot_dst = jnp.where(valid, (a % K) * T + a // K, 0).reshape(-1)
    return block_expert, block_cnt, slot_tok, slot_dst


def _bdot(a, b):
    return jnp.dot(a.astype(jnp.bfloat16), b.astype(jnp.bfloat16), preferred_element_type=jnp.float32)


def _bdot_nt(a, b):
    return lax.dot_general(a.astype(jnp.bfloat16), b.astype(jnp.bfloat16), (((1,), (1,)), ((), ())),
                           preferred_element_type=jnp.float32)


def _bdot_tn(a, b):
    return lax.dot_general(a.astype(jnp.bfloat16), b.astype(jnp.bfloat16), (((0,), (0,)), ((), ())),
                           preferred_element_type=jnp.float32)


def _seq_pos(LT):
    row = lax.broadcasted_iota(jnp.int32, (LT, 1), 0)
    in_ctx = row < CTX_LEN
    return jnp.where(in_ctx, row, row - CTX_LEN), jnp.where(in_ctx, CTX_LEN, LT - CTX_LEN)


def _conv_silu(x, cw, pos, seq_len):
    LT = x.shape[0]
    K = cw.shape[0]
    r = K // 2
    out = x * cw[r:r + 1, :]
    for j in range(K):
        s = r - j
        if s == 0:
            continue
        xs = pltpu.roll(x, s % LT, axis=0)
        valid = (pos >= s) if s > 0 else (pos - s < seq_len)
        out = out + jnp.where(valid, xs, 0.0) * cw[j:j + 1, :]
    return out * _sigmoid(out)


def _l2n(t):
    return t * lax.rsqrt(jnp.sum(t * t, axis=-1, keepdims=True) + NORM_EPS)


def _lane_col(x, idx):
    lane = lax.broadcasted_iota(jnp.int32, x.shape, 1)
    return jnp.sum(jnp.where(lane == idx, x, 0.0), axis=1, keepdims=True)


def _inv_unit_tri(Ns, eye, m16, moffs):
    Dg = [N * m16 for N in Ns]
    D2 = [_bdot(d, d) for d in Dg]
    D4 = [_bdot(d, d) for d in D2]
    D8 = [_bdot(d, d) for d in D4]
    T = [eye - d for d in Dg]
    for P in (D2, D4, D8):
        T = [t + _bdot(t, p) for t, p in zip(T, P)]
    for m in moffs:
        X = [_bdot(t, N * m) for t, N in zip(T, Ns)]
        T = [t - _bdot(x, t) for t, x in zip(T, X)]
    return T


def _gdn_kernel(q_ref, k_ref, v_ref, ab_ref, cwq_ref, cwk_ref, cwv_ref, alog_ref, dtb_ref, o_ref,
                qn_s, kn_s, vn_s, gb_s, u_s, w_s, qs_s, kst_s, qkd_s, el_s, o_s):
    f32 = jnp.float32
    bf16 = jnp.bfloat16
    h = pl.program_id(1)
    LT = q_ref.shape[1]
    C = GDN_ROWS
    nch = LT // C
    nctx = CTX_LEN // C

    pos, seq_len = _seq_pos(LT)
    qn_s[...] = _l2n(_conv_silu(q_ref[0], cwq_ref[...], pos, seq_len)) * (GDN_HEAD_DIM ** -0.5)
    kn_s[...] = _l2n(_conv_silu(k_ref[0], cwk_ref[...], pos, seq_len))
    vn_s[...] = _conv_silu(v_ref[0], cwv_ref[...], pos, seq_len)

    ab = ab_ref[0]
    xa = ab + dtb_ref[...]
    g_all = -jnp.exp(alog_ref[...]) * (jnp.maximum(xa, 0.0) + jnp.log1p(jnp.exp(-jnp.abs(xa))))
    b_all = _sigmoid(ab)
    lane = lax.broadcasted_iota(jnp.int32, (LT, 128), 1)
    gb = jnp.where(lane == 0, _lane_col(g_all, h), 0.0)
    gb = jnp.where(lane == 1, _lane_col(g_all, GDN_HEADS + h), gb)
    gb = jnp.where(lane == 2, _lane_col(b_all, 2 * GDN_HEADS + h), gb)
    gb = jnp.where(lane == 3, _lane_col(b_all, 3 * GDN_HEADS + h), gb)
    gb_s[...] = gb

    ri = lax.broadcasted_iota(jnp.int32, (C, C), 0)
    ci = lax.broadcasted_iota(jnp.int32, (C, C), 1)
    eye = (ri == ci).astype(f32)
    low = (ri >= ci).astype(f32)
    upp = (ri <= ci).astype(f32)
    m16 = ((ri // 16) == (ci // 16)).astype(f32)
    moffs = [(((ri // (2 * s)) == (ci // (2 * s))) & ((ri // s) != (ci // s))).astype(f32) for s in (16, 32, 64)]
    hi = lax.Precision.HIGHEST

    def chunk_group(i, carry):
        probs = []
        for j in range(GDN_CHUNKS_PER_STEP):
            c = i * GDN_CHUNKS_PER_STEP + j
            rows = pl.ds(pl.multiple_of(c * C, C), C)
            q = qn_s[rows, :]
            k = kn_s[rows, :]
            v = vn_s[rows, :]
            gbc = gb_s[rows, :]
            kk = _bdot_nt(k, k)
            qk = _bdot_nt(q, k)
            cs_f = jnp.dot(low, gbc, precision=hi, preferred_element_type=f32)
            cs_b = jnp.dot(upp, gbc, precision=hi, preferred_element_type=f32)
            for d in range(2):
                cs = cs_f if d == 0 else cs_b
                gc = cs[:, d:d + 1]
                gc_row = cs.T[d:d + 1, :]
                beta = gbc[:, 2 + d:3 + d]
                incl = low if d == 0 else upp
                decay = jnp.exp(jnp.where(incl > 0, gc - gc_row, -jnp.inf))
                g_last = gc[C - 1:C, :] if d == 0 else gc[0:1, :]
                probs.append(dict(d=d, c=c, q=q, k=k, v=v, qk=qk, gc=gc, beta=beta, decay=decay, g_last=g_last,
                                  N=(beta * kk) * decay * (1.0 - eye)))
        Ts = _inv_unit_tri([p["N"] for p in probs], eye, m16, moffs)
        for p, T in zip(probs, Ts):
            d, c, gc, beta = p["d"], p["c"], p["gc"], p["beta"]
            eg = jnp.exp(gc)
            u_s[d, c] = _bdot(T, p["v"] * beta)
            w_s[d, c] = _bdot(T, p["k"] * beta * eg).astype(bf16)
            qs_s[d, c] = (p["q"] * eg).astype(bf16)
            kst_s[d, c] = (p["k"] * jnp.exp(p["g_last"] - gc)).T.astype(bf16)
            qkd_s[d, c] = (p["qk"] * p["decay"]).astype(bf16)
            el_s[d, c] = jnp.broadcast_to(jnp.exp(p["g_last"]), (8, 128))
        return carry

    lax.fori_loop(0, nch // GDN_CHUNKS_PER_STEP, chunk_group, 0)

    def step(t, S):
        cb = jnp.where(t < nctx, nctx - 1 - t, nch - 1 + nctx - t)
        outs = []
        for d, (c, Sd) in enumerate(((t, S[0]), (cb, S[1]))):
            Sbf = Sd.astype(bf16)
            v_new = u_s[d, c] - jnp.dot(w_s[d, c], Sbf, preferred_element_type=f32)
            vb = v_new.astype(bf16)
            o_s[d, c] = (jnp.dot(qs_s[d, c], Sbf, preferred_element_type=f32)
                         + jnp.dot(qkd_s[d, c], vb, preferred_element_type=f32))
            outs.append(Sd * el_s[d, c][0:1, :] + jnp.dot(kst_s[d, c], vb, preferred_element_type=f32))
        return tuple(outs)

    z = jnp.zeros((GDN_HEAD_DIM, GDN_HEAD_DIM), f32)
    lax.fori_loop(0, nch, step, (z, z))
    o_ref[0] = (o_s[0] + o_s[1]).reshape(LT, GDN_HEAD_DIM)


def _gdn_mixer(p3, ab, conv_w, a_log, dt_bias):
    B, LT, _ = p3.shape
    H, Dh, C = GDN_HEADS, GDN_HEAD_DIM, GDN_ROWS
    nch = LT // C
    qb = C_QKV // Dh
    alog_row = jnp.zeros((1, 128), jnp.float32).at[0, :2 * H].set(a_log.reshape(-1))
    dtb_row = jnp.zeros((1, 128), jnp.float32).at[0, :2 * H].set(dt_bias.reshape(-1))
    taps = conv_w.shape[0]
    per_chunk = lambda dt: pltpu.VMEM((2, nch, C, Dh), dt)
    return pl.pallas_call(
        _gdn_kernel,
        out_shape=jax.ShapeDtypeStruct((B, LT, H * Dh), jnp.float32),
        grid=(B, H),
        in_specs=[
            pl.BlockSpec((1, LT, Dh), lambda b, h: (b, 0, qb + h)),
            pl.BlockSpec((1, LT, Dh), lambda b, h: (b, 0, qb + H + h)),
            pl.BlockSpec((1, LT, Dh), lambda b, h: (b, 0, qb + 2 * H + h)),
            pl.BlockSpec((1, LT, 128), lambda b, h: (b, 0, 0)),
            pl.BlockSpec((taps, Dh), lambda b, h: (0, h)),
            pl.BlockSpec((taps, Dh), lambda b, h: (0, H + h)),
            pl.BlockSpec((taps, Dh), lambda b, h: (0, 2 * H + h)),
            pl.BlockSpec((1, 128), lambda b, h: (0, 0)),
            pl.BlockSpec((1, 128), lambda b, h: (0, 0)),
        ],
        out_specs=pl.BlockSpec((1, LT, Dh), lambda b, h: (b, 0, h)),
        scratch_shapes=[
            pltpu.VMEM((LT, Dh), jnp.float32), pltpu.VMEM((LT, Dh), jnp.float32), pltpu.VMEM((LT, Dh), jnp.float32),
            pltpu.VMEM((LT, 128), jnp.float32),
            per_chunk(jnp.float32), per_chunk(jnp.bfloat16), per_chunk(jnp.bfloat16), per_chunk(jnp.bfloat16),
            per_chunk(jnp.bfloat16),
            pltpu.VMEM((2, nch, 8, 128), jnp.float32),
            per_chunk(jnp.float32),
        ],
        compiler_params=_cparams("parallel", "parallel"),
        name="gdn_mixer",
    )(p3, p3, p3, ab, conv_w, conv_w, conv_w, alog_row, dtb_row)


def _ret_kernel(lg_ref, q_ref, k_ref, v_ref, cos_ref, sin_ref, o_ref, qr_s, kr_s):
    f32 = jnp.float32
    h = pl.program_id(1)
    LT = q_ref.shape[1]
    C = RET_ROWS
    nch = LT // C
    nctx = CTX_LEN // C
    cos = cos_ref[...]
    sin = sin_ref[...]
    half = RET_KEY_DIM // 2
    q = q_ref[0]
    k = k_ref[0]
    qr_s[...] = (q * cos + pltpu.roll(q, half, axis=1) * sin) * (RET_KEY_DIM ** -0.5)
    kr_s[...] = k * cos + pltpu.roll(k, half, axis=1) * sin

    lg_f = lg_ref[0, h]
    lg_b = lg_ref[1, h]
    ri = lax.broadcasted_iota(jnp.int32, (C, C), 0)
    ci = lax.broadcasted_iota(jnp.int32, (C, C), 1)
    diff = (ri - ci).astype(f32)
    dm = (jnp.exp(jnp.where(diff >= 0, lg_f * diff, -jnp.inf))
          + jnp.exp(jnp.where(diff <= 0, -lg_b * diff, -jnp.inf)))
    pcol = lax.broadcasted_iota(jnp.int32, (C, 1), 0).astype(f32)
    cross_f = jnp.exp(lg_f * (pcol + 1.0))
    in_f = jnp.exp(lg_f * (C - 1.0 - pcol))
    cross_b = jnp.exp(lg_b * (C - pcol))
    in_b = jnp.exp(lg_b * pcol)
    cd_f = jnp.exp(lg_f * C)
    cd_b = jnp.exp(lg_b * C)

    def inner(c, carry):
        rows = pl.ds(pl.multiple_of(c * C, C), C)
        s = _bdot_nt(qr_s[rows, :], kr_s[rows, :])
        o_ref[0, rows, :] = _bdot(s * dm, v_ref[0, rows, :])
        return carry

    lax.fori_loop(0, nch, inner, 0)

    def step(t, S):
        cb = jnp.where(t < nctx, nctx - 1 - t, nch - 1 + nctx - t)
        outs = []
        for c, Sd, cr, ind, cd in ((t, S[0], cross_f, in_f, cd_f), (cb, S[1], cross_b, in_b, cd_b)):
            rows = pl.ds(pl.multiple_of(c * C, C), C)
            o_ref[0, rows, :] += _bdot(qr_s[rows, :], Sd) * cr
            outs.append(Sd * cd + _bdot_tn(kr_s[rows, :] * ind, v_ref[0, rows, :]))
        return tuple(outs)

    z = jnp.zeros((RET_KEY_DIM, RET_VAL_DIM), f32)
    lax.fori_loop(0, nch, step, (z, z))


def _ret_mixer(p3, log_gamma, cos_t, sin_t):
    B, LT, _ = p3.shape
    H, Dk, Dv = RET_HEADS, RET_KEY_DIM, RET_VAL_DIM
    grid_spec = pltpu.PrefetchScalarGridSpec(
        num_scalar_prefetch=1,
        grid=(B, H),
        in_specs=[
            pl.BlockSpec((1, LT, Dk), lambda b, h, lg: (b, 0, C_RQ // Dk + h)),
            pl.BlockSpec((1, LT, Dk), lambda b, h, lg: (b, 0, C_RK // Dk + h)),
            pl.BlockSpec((1, LT, Dv), lambda b, h, lg: (b, 0, C_RV // Dv + h)),
            pl.BlockSpec((LT, Dk), lambda b, h, lg: (0, 0)),
            pl.BlockSpec((LT, Dk), lambda b, h, lg: (0, 0)),
        ],
        out_specs=pl.BlockSpec((1, LT, Dv), lambda b, h, lg: (b, 0, h)),
        scratch_shapes=[pltpu.VMEM((LT, Dk), jnp.float32), pltpu.VMEM((LT, Dk), jnp.float32)],
    )
    return pl.pallas_call(
        _ret_kernel,
        out_shape=jax.ShapeDtypeStruct((B, LT, H * Dv), jnp.float32),
        grid_spec=grid_spec,
        compiler_params=_cparams("parallel", "parallel"),
        name="ret_mixer",
    )(log_gamma, p3, p3, p3, cos_t, sin_t)


def _axial_rope(rows, dim):
    n = dim // 4
    freq = ROPE_BASE ** (-jnp.arange(n, dtype=jnp.float32) / n)
    row = jnp.repeat(jnp.arange(rows, dtype=jnp.float32), GRID_W)
    col = (jnp.arange(rows * GRID_W) % GRID_W).astype(jnp.float32)
    ang = jnp.concatenate([row[:, None] * freq, col[:, None] * freq], axis=-1)
    ang = jnp.concatenate([ang, ang], axis=-1)[:, None, :]
    return jnp.cos(ang), jnp.sin(ang)


def _rope_tables(L):
    cos, sin = _axial_rope(L // GRID_W, RET_KEY_DIM)
    half = RET_KEY_DIM // 2
    sign = jnp.concatenate([-jnp.ones((half,), jnp.float32), jnp.ones((half,), jnp.float32)])
    cos_t = jnp.concatenate([jnp.ones((CTX_LEN, RET_KEY_DIM), jnp.float32), cos[:, 0, :]], axis=0)
    sin_t = jnp.concatenate([jnp.zeros((CTX_LEN, RET_KEY_DIM), jnp.float32), sin[:, 0, :] * sign], axis=0)
    return cos_t, sin_t


def _seg_table(ctx_vec, lat_vec):
    B, D = lat_vec.shape
    return jnp.stack([jnp.broadcast_to(ctx_vec, (B, D)), lat_vec], axis=1).reshape(B, 2, 1, D)


def kernel(x, c, ctx, c_ctx, mod_w, mod_b, norm1_w, w_in, gm_norm_w, gm_spatial_w, gm_spatial_b, gdn_conv_w, gdn_a_log, gdn_dt_bias, gdn_norm_w, ret_decay_logit, ret_norm_w, w_br_gm, w_br_gdn, w_br_ret, w_out, norm2_w, router_group_w, router_group_b, router_expert_w, router_expert_b, moe_w_gate, moe_w_up, moe_w_down, final_norm_w):
    B, L, D = x.shape
    depth = mod_w.shape[0]
    LT = CTX_LEN + L
    T = B * LT
    bf16 = jnp.bfloat16
    cos_t, sin_t = _rope_tables(L)

    w_in_t = jnp.swapaxes(w_in, 1, 2)
    wgm_b, wgdn_b, wret_b, wout_b = (t.astype(bf16) for t in (w_br_gm, w_br_gdn, w_br_ret, w_out))
    ws_b = gm_spatial_w.astype(bf16)
    bs_full = jnp.broadcast_to(gm_spatial_b[..., None], gm_spatial_b.shape + (GM_GROUP_CH,))
    pad = 128 - MOE_GROUPS - MOE_EXPERTS
    w_router = jnp.concatenate([router_group_w, router_expert_w, jnp.zeros((depth, D, pad), jnp.float32)], axis=-1)
    b_router = jnp.concatenate([router_group_b, router_expert_b, jnp.zeros((depth, pad), jnp.float32)], axis=-1)
    moe_wg, moe_wu = (t.reshape(depth * MOE_EXPERTS, D, MOE_HIDDEN) for t in (moe_w_gate, moe_w_up))
    moe_wd = moe_w_down.reshape(depth * MOE_EXPERTS, MOE_HIDDEN, D)
    main_rows = lambda j: j * PROJ_COLS + jnp.where(j * PROJ_COLS < AB_START, 0, AB_END - AB_START)

    cc = jnp.concatenate([jax.nn.silu(c), jax.nn.silu(c_ctx)[None], jnp.zeros((16 - B - 1, D), c.dtype)], axis=0)
    cc = cc.astype(bf16)
    mods = []
    for l in range(depth):
        mod = _matmul(cc, mod_w, l, 16, 1024, "adaln_mod") + mod_b[l]
        mods.append([_seg_table(mod[B, i * D:(i + 1) * D], mod[:B, i * D:(i + 1) * D]) for i in range(6)])

    z = jnp.concatenate([ctx, x], axis=1)
    h = _normmod(z, norm1_w[0], mods[0][0], mods[0][1], bf16)
    for l in range(depth):
        sh1, sc1, gt1, sh2, sc2, gt2 = mods[l]
        log_gamma = jax.nn.log_sigmoid(ret_decay_logit[l])

        h = h.reshape(T, D)
        p = _in_proj(h, w_in_t, l, LT, PROJ_COLS, P_COLS, main_rows, "in_proj")
        pab = _in_proj(h, w_in_t, l, LT, 128, 128, lambda j: j * 0 + AB_START, "in_proj_ab")
        p3 = p.reshape(B, LT, P_COLS)
        o_gdn = _gdn_mixer(p3, pab.reshape(B, LT, 128), gdn_conv_w[l], gdn_a_log[l], gdn_dt_bias[l])
        o_ret = _ret_mixer(p3, log_gamma, cos_t, sin_t)
        o_gdn = o_gdn.reshape(T, GDN_GATE_COLS)
        o_ret = o_ret.reshape(T, RET_V_COLS)

        merged = _merge(p, o_gdn, o_ret, gm_norm_w[l], ws_b[l], bs_full[l], gdn_norm_w[l], ret_norm_w[l],
                        wgm_b[l], wgdn_b[l], wret_b[l])
        z = _resid_matmul(merged, wout_b[l], z.reshape(T, D), gt1[:, 1, 0], gt1[0, 0, 0], LT).reshape(B, LT, D)

        h2, route = _norm_route(z, norm2_w[l], sh2, sc2, w_router[l], b_router[l])
        expert_idx = route[:, :MOE_TOP_K].astype(jnp.int32)
        weights = route[:, MOE_TOP_K:2 * MOE_TOP_K]
        block_expert, block_cnt, slot_tok, slot_dst = _dispatch_tables(expert_idx)
        y = _moe(h2, block_expert + l * MOE_EXPERTS, block_cnt, slot_tok, slot_dst, moe_wg, moe_wu, moe_wd)
        nxt = (norm1_w[l + 1], mods[l + 1][0], mods[l + 1][1]) if l + 1 < depth else None
        z, h = _combine(z, y, weights, gt2, nxt)
    return _final_norm(z, final_norm_w)
```

```python
import functools
import math

import jax
import jax.numpy as jnp
from jax import lax
from jax.experimental import pallas as pl
from jax.experimental.pallas import tpu as pltpu

D_MODEL = 2048
CTX_LEN = 256
GRID_W = 64
NORM_EPS = 1e-6

GM_CHUNK = 128
GM_GROUPS = 4
GM_GROUP_CH = 128
GM_WIDTH = GM_GROUPS * GM_GROUP_CH
GDN_HEADS = 4
GDN_HEAD_DIM = 128
GDN_ROWS = 128
GDN_CHUNKS_PER_STEP = 6
RET_HEADS = 4
RET_KEY_DIM = 128
RET_VAL_DIM = 256
RET_ROWS = 256
ROPE_BASE = 10000.0
MOE_GROUPS = 4
MOE_EXPERTS_PER_GROUP = 8
MOE_EXPERTS = MOE_GROUPS * MOE_EXPERTS_PER_GROUP
MOE_TOP_K = 2
MOE_HIDDEN = 512

GM_UV_COLS = 2 * GM_WIDTH
GDN_QKV_COLS = 3 * GDN_HEADS * GDN_HEAD_DIM
GDN_AB_COLS = 2 * GDN_HEADS
GDN_GATE_COLS = GDN_HEADS * GDN_HEAD_DIM
RET_QK_COLS = RET_HEADS * RET_KEY_DIM
RET_V_COLS = RET_HEADS * RET_VAL_DIM
MERGE_COLS = 3 * D_MODEL
AB_START = GM_UV_COLS + GDN_QKV_COLS
AB_END = AB_START + 2 * GDN_AB_COLS

C_UV = 0
C_QKV = C_UV + GM_UV_COLS
C_RQ = C_QKV + GDN_QKV_COLS
C_RK = C_RQ + RET_QK_COLS
C_RV = C_RK + RET_QK_COLS
C_GG = C_RV + RET_V_COLS
C_RG = C_GG + GDN_GATE_COLS
C_ML = C_RG + RET_V_COLS
P_COLS = C_ML + MERGE_COLS

V7X_VMEM_LIMIT_BYTES = 56 * 1024 * 1024
MOE_ROWS = 256
MOE_DMA_GROUP = 8
MERGE_ROWS = 256
NORM_ROWS = 256
PROJ_COLS = 512


def _cparams(*sem):
    return pltpu.CompilerParams(dimension_semantics=sem, vmem_limit_bytes=V7X_VMEM_LIMIT_BYTES)


def _normmod_kernel(z_ref, nw_ref, sh_ref, sc_ref, o_ref):
    z = z_ref[0]
    y = z * lax.rsqrt(jnp.mean(z * z, axis=-1, keepdims=True) + NORM_EPS)
    y = y * nw_ref[...]
    o_ref[0] = (y * (1.0 + sc_ref[0, 0]) + sh_ref[0, 0]).astype(o_ref.dtype)


def _normmod(z, norm_w, shift, scale, out_dtype):
    B, LT, D = z.shape
    seg = lambda b, t: (b, jnp.minimum(t, 1), 0, 0)
    assert CTX_LEN == NORM_ROWS
    return pl.pallas_call(
        _normmod_kernel,
        out_shape=jax.ShapeDtypeStruct((B, LT, D), out_dtype),
        grid=(B, LT // NORM_ROWS),
        in_specs=[
            pl.BlockSpec((1, NORM_ROWS, D), lambda b, t: (b, t, 0)),
            pl.BlockSpec((1, D), lambda b, t: (0, 0)),
            pl.BlockSpec((1, 1, 1, D), seg),
            pl.BlockSpec((1, 1, 1, D), seg),
        ],
        out_specs=pl.BlockSpec((1, NORM_ROWS, D), lambda b, t: (b, t, 0)),
        compiler_params=_cparams("parallel", "parallel"),
        name="normmod",
    )(z, norm_w.reshape(1, D), shift, scale)


def _split_cols(ref, x):
    for c in range(x.shape[1] // 128):
        ref[c] = x[:, c * 128:(c + 1) * 128]


def _join_cols(ref):
    return jnp.concatenate([ref[c] for c in range(ref.shape[0])], axis=-1)


def _route_rows(logits):
    neg = jnp.float32(-3.0e38)
    lane = lax.broadcasted_iota(jnp.int32, logits.shape, 1)
    lane_f = lane.astype(jnp.float32)
    first_max = lambda v: jnp.min(jnp.where(v == jnp.max(v, axis=1, keepdims=True), lane_f, 128.0), axis=1,
                                  keepdims=True)
    is_g = lane < MOE_GROUPS
    gl = jnp.where(is_g, logits, neg)
    g_max = jnp.max(gl, axis=1, keepdims=True)
    grp = first_max(gl).astype(jnp.int32)
    p_top = 1.0 / jnp.sum(jnp.where(is_g, jnp.exp(logits - g_max), 0.0), axis=1, keepdims=True)
    e_lane = lane - MOE_GROUPS
    in_grp = (e_lane >= 0) & (e_lane < MOE_EXPERTS) & ((e_lane // MOE_EXPERTS_PER_GROUP) == grp)
    el = jnp.where(in_grp, logits, neg)
    m1 = jnp.max(el, axis=1, keepdims=True)
    i1 = first_max(el)
    el2 = jnp.where(lane_f == i1, neg, el)
    m2 = jnp.max(el2, axis=1, keepdims=True)
    i2 = first_max(el2)
    r = jnp.exp(m2 - m1)
    w1 = p_top / (1.0 + r)
    out = jnp.where(lane == 0, i1 - MOE_GROUPS, 0.0)
    out = jnp.where(lane == 1, i2 - MOE_GROUPS, out)
    out = jnp.where(lane == 2, w1, out)
    return jnp.where(lane == 3, w1 * r, out)


def _out_route_kernel(a_ref, w_ref, x_ref, g_ref, nw_ref, sh_ref, sc_ref, whi_ref, wlo_ref, rb_ref,
                      z_ref, h_ref, rt_ref):
    z = x_ref[0] + g_ref[0, 0] * jnp.dot(a_ref[...], w_ref[...], preferred_element_type=jnp.float32)
    z_ref[0] = z
    y = z * lax.rsqrt(jnp.mean(z * z, axis=-1, keepdims=True) + NORM_EPS)
    h = y * nw_ref[...] * (1.0 + sc_ref[0, 0]) + sh_ref[0, 0]
    _split_cols(h_ref, h)
    hi = h.astype(jnp.bfloat16)
    lo = (h - hi.astype(jnp.float32)).astype(jnp.bfloat16)
    whi = whi_ref[...]
    logits = (jnp.dot(hi, whi, preferred_element_type=jnp.float32)
              + jnp.dot(lo, whi, preferred_element_type=jnp.float32)
              + jnp.dot(hi, wlo_ref[...], preferred_element_type=jnp.float32))
    rt_ref[...] = _route_rows(logits + rb_ref[...])


def _out_proj_route(merged, w_out, z, gate, norm_w, shift, scale, w_router, b_router):
    B, LT, D = z.shape
    nt = LT // NORM_ROWS
    seg = lambda b, t: (b, jnp.minimum(t, 1), 0, 0)
    tile = pl.BlockSpec((1, NORM_ROWS, D), lambda b, t: (b, t, 0))
    const = lambda r, c: pl.BlockSpec((r, c), lambda b, t: (0, 0))
    whi = w_router.astype(jnp.bfloat16)
    wlo = (w_router - whi.astype(jnp.float32)).astype(jnp.bfloat16)
    return pl.pallas_call(
        _out_route_kernel,
        out_shape=[jax.ShapeDtypeStruct((B, LT, D), jnp.float32),
                   jax.ShapeDtypeStruct((D // 128, B * LT, 128), jnp.float32),
                   jax.ShapeDtypeStruct((B * LT, 128), jnp.float32)],
        grid=(B, nt),
        in_specs=[
            pl.BlockSpec((NORM_ROWS, D), lambda b, t: (b * nt + t, 0)),
            const(D, D),
            tile,
            pl.BlockSpec((1, 1, 1, D), seg),
            const(1, D),
            pl.BlockSpec((1, 1, 1, D), seg),
            pl.BlockSpec((1, 1, 1, D), seg),
            const(D, 128),
            const(D, 128),
            const(1, 128),
        ],
        out_specs=[tile,
                   pl.BlockSpec((D // 128, NORM_ROWS, 128), lambda b, t: (0, b * nt + t, 0)),
                   pl.BlockSpec((NORM_ROWS, 128), lambda b, t: (b * nt + t, 0))],
        compiler_params=_cparams("parallel", "parallel"),
        name="out_proj_route",
    )(merged, w_out, z, gate, norm_w.reshape(1, D), shift, scale, whi, wlo, b_router.reshape(1, 128))


def _final_norm_kernel(z_ref, nw_ref, o_ref):
    z = z_ref[0]
    o_ref[0] = z * lax.rsqrt(jnp.mean(z * z, axis=-1, keepdims=True) + NORM_EPS) * nw_ref[...]


def _final_norm(z, norm_w):
    B, LT, D = z.shape
    skip = CTX_LEN // NORM_ROWS
    return pl.pallas_call(
        _final_norm_kernel,
        out_shape=jax.ShapeDtypeStruct((B, LT - CTX_LEN, D), jnp.float32),
        grid=(B, (LT - CTX_LEN) // NORM_ROWS),
        in_specs=[
            pl.BlockSpec((1, NORM_ROWS, D), lambda b, t: (b, t + skip, 0)),
            pl.BlockSpec((1, D), lambda b, t: (0, 0)),
        ],
        out_specs=pl.BlockSpec((1, NORM_ROWS, D), lambda b, t: (b, t, 0)),
        compiler_params=_cparams("parallel", "parallel"),
        name="final_norm",
    )(z, norm_w.reshape(1, D))


def _mm_kernel(a_ref, w_ref, o_ref):
    o_ref[...] = jnp.dot(a_ref[...], w_ref[...].astype(jnp.bfloat16), preferred_element_type=jnp.float32)


def _matmul(a, w, layer, tm, tn, name):
    M, K = a.shape
    N = w.shape[2]
    return pl.pallas_call(
        _mm_kernel,
        out_shape=jax.ShapeDtypeStruct((M, N), jnp.float32),
        grid=(M // tm, N // tn),
        in_specs=[
            pl.BlockSpec((tm, K), lambda i, j: (i, 0)),
            pl.BlockSpec((None, K, tn), lambda i, j: (layer, 0, j)),
        ],
        out_specs=pl.BlockSpec((tm, tn), lambda i, j: (i, j)),
        compiler_params=_cparams("parallel", "arbitrary"),
        name=name,
    )(a, w)


def _mm_nt_kernel(a_ref, w_ref, o_ref):
    w = w_ref[0].astype(jnp.bfloat16)
    o_ref[...] = lax.dot_general(a_ref[...], w, (((1,), (1,)), ((), ())), preferred_element_type=jnp.float32)


def _in_proj(a, w_t, layer, tm, tn, n_out, first_row, name):
    M, K = a.shape
    wmap = lambda i, j: (layer, pl.multiple_of(first_row(j), 8), 0)
    return pl.pallas_call(
        _mm_nt_kernel,
        out_shape=jax.ShapeDtypeStruct((M, n_out), jnp.float32),
        grid=(M // tm, n_out // tn),
        in_specs=[
            pl.BlockSpec((tm, K), lambda i, j: (i, 0)),
            pl.BlockSpec((pl.Element(1), pl.Element(tn), pl.Element(K)), wmap),
        ],
        out_specs=pl.BlockSpec((tm, tn), lambda i, j: (i, j)),
        compiler_params=_cparams("parallel", "arbitrary"),
        name=name,
    )(a, w_t)


def _gelu_exact(x):
    return 0.5 * x * (1.0 + lax.erf(x * (1.0 / math.sqrt(2.0))))


def _sigmoid(x):
    return 1.0 / (1.0 + jnp.exp(-x))


def _merge_kernel(uv_ref, gg_ref, rg_ref, ml_ref, og_ref, or_ref, gmn_ref, ws_ref, bs_ref, gdn_nw_ref,
                  ret_nw_ref, wgm_ref, wgdn_ref, wret_ref, o_ref):
    bf16 = jnp.bfloat16
    f32 = jnp.float32
    rows = uv_ref.shape[0]
    g = _gelu_exact(uv_ref[...])
    u = g[:, :GM_WIDTH]
    v = g[:, GM_WIDTH:]
    d = v - jnp.mean(v, axis=-1, keepdims=True)
    vn = (d * lax.rsqrt(jnp.mean(d * d, axis=-1, keepdims=True) + NORM_EPS) * gmn_ref[...]).astype(bf16)
    chunks = []
    for c in range(rows // GM_CHUNK):
        groups = []
        for gi in range(GM_GROUPS):
            blk = vn[c * GM_CHUNK:(c + 1) * GM_CHUNK, gi * GM_GROUP_CH:(gi + 1) * GM_GROUP_CH]
            groups.append(jnp.dot(ws_ref[gi], blk, preferred_element_type=f32) + bs_ref[gi])
        chunks.append(jnp.concatenate(groups, axis=-1))
    s = jnp.concatenate(chunks, axis=0)
    y_gm = jnp.dot((u * s).astype(bf16), wgm_ref[...], preferred_element_type=f32)

    og = og_ref[...]
    heads = []
    for h in range(GDN_HEADS):
        oh = og[:, h * GDN_HEAD_DIM:(h + 1) * GDN_HEAD_DIM]
        heads.append(oh * lax.rsqrt(jnp.mean(oh * oh, axis=-1, keepdims=True) + NORM_EPS) * gdn_nw_ref[...])
    gg = gg_ref[...]
    a_gdn = (jnp.concatenate(heads, axis=-1) * (gg * _sigmoid(gg))).astype(bf16)
    y_gdn = jnp.dot(a_gdn, wgdn_ref[...], preferred_element_type=f32)

    orr = or_ref[...]
    heads = []
    for h in range(RET_HEADS):
        oh = orr[:, h * RET_VAL_DIM:(h + 1) * RET_VAL_DIM]
        dh = oh - jnp.mean(oh, axis=-1, keepdims=True)
        heads.append(dh * lax.rsqrt(jnp.mean(dh * dh, axis=-1, keepdims=True) + NORM_EPS))
    rg = rg_ref[...]
    a_ret = (jnp.concatenate(heads, axis=-1) * ret_nw_ref[...] * (rg * _sigmoid(rg))).astype(bf16)
    y_ret = jnp.dot(a_ret, wret_ref[...], preferred_element_type=f32)

    sg = _sigmoid(ml_ref[...])
    merged = (sg[:, :D_MODEL] * y_gm + sg[:, D_MODEL:2 * D_MODEL] * y_gdn + sg[:, 2 * D_MODEL:] * y_ret)
    o_ref[...] = merged.astype(o_ref.dtype)


def _merge(p, o_gdn, o_ret, gm_norm_w, ws, bs, gdn_norm_w, ret_norm_w, wgm, wgdn, wret):
    M = p.shape[0]
    tm = MERGE_ROWS
    const = lambda *shape: pl.BlockSpec(shape, lambda i: (0,) * len(shape))
    return pl.pallas_call(
        _merge_kernel,
        out_shape=jax.ShapeDtypeStruct((M, D_MODEL), jnp.bfloat16),
        grid=(M // tm,),
        in_specs=[
            pl.BlockSpec((tm, GM_UV_COLS), lambda i: (i, C_UV // GM_UV_COLS)),
            pl.BlockSpec((tm, GDN_GATE_COLS), lambda i: (i, C_GG // GDN_GATE_COLS)),
            pl.BlockSpec((tm, RET_V_COLS), lambda i: (i, C_RG // RET_V_COLS)),
            pl.BlockSpec((tm, MERGE_COLS), lambda i: (i, C_ML // MERGE_COLS)),
            pl.BlockSpec((tm, GDN_GATE_COLS), lambda i: (i, 0)),
            pl.BlockSpec((tm, RET_V_COLS), lambda i: (i, 0)),
            const(1, GM_WIDTH),
            const(GM_GROUPS, GM_CHUNK, GM_CHUNK),
            const(GM_GROUPS, GM_CHUNK, GM_GROUP_CH),
            const(1, GDN_HEAD_DIM),
            const(1, RET_V_COLS),
            const(GM_WIDTH, D_MODEL),
            const(GDN_GATE_COLS, D_MODEL),
            const(RET_V_COLS, D_MODEL),
        ],
        out_specs=pl.BlockSpec((tm, D_MODEL), lambda i: (i, 0)),
        compiler_params=_cparams("parallel"),
        name="merge",
    )(p, p, p, p, o_gdn, o_ret, gm_norm_w.reshape(1, -1), ws, bs, gdn_norm_w.reshape(1, -1),
      ret_norm_w.reshape(1, -1), wgm, wgdn, wret)


def _moe_kernel(bexp_ref, bcnt_ref, stok_ref, sdst_ref,
                h_hbm, wg_ref, wu_ref, wd_ref, y_hbm,
                xbuf, ybuf, wgb, wub, wdb, gsem, ssem):
    i = pl.program_id(0)
    n = pl.num_programs(0)
    R = MOE_ROWS
    slot = lax.rem(i, 2)

    def gather_copy(blk, r, s):
        return pltpu.make_async_copy(h_hbm.at[:, stok_ref[blk * R + r], :], xbuf.at[s, :, r, :], gsem.at[s])

    def scatter_copy(blk, r, s):
        return pltpu.make_async_copy(ybuf.at[s, :, r, :], y_hbm.at[:, sdst_ref[blk * R + r], :], ssem.at[s])

    def for_rows(blk, fn):
        cnt = bcnt_ref[blk]
        full = lax.shift_right_logical(cnt, MOE_DMA_GROUP.bit_length() - 1)

        def group(g, c):
            for j in range(MOE_DMA_GROUP):
                fn(g * MOE_DMA_GROUP + j)
            return c
        lax.fori_loop(0, full, group, 0)
        lax.fori_loop(full * MOE_DMA_GROUP, cnt, lambda r, c: (fn(r), c)[1], 0)

    def start_gather(blk, s):
        for_rows(blk, lambda r: gather_copy(blk, r, s).start())

    def wait_gather(blk, s):
        for_rows(blk, lambda r: gather_copy(blk, r, s).wait())

    def start_scatter(blk, s):
        for_rows(blk, lambda r: scatter_copy(blk, r, s).start())

    def wait_scatter(blk, s):
        for_rows(blk, lambda r: scatter_copy(blk, r, s).wait())

    @pl.when(i == 0)
    def _():
        xbuf[...] = jnp.zeros_like(xbuf)
        start_gather(0, 0)

    @pl.when(i + 1 < n)
    def _():
        start_gather(i + 1, 1 - slot)

    changed = jnp.logical_or(i == 0, bexp_ref[i] != bexp_ref[jnp.maximum(i - 1, 0)])

    @pl.when(changed)
    def _():
        wgb[...] = wg_ref[0].astype(jnp.bfloat16)
        wub[...] = wu_ref[0].astype(jnp.bfloat16)
        wdb[...] = wd_ref[0].astype(jnp.bfloat16)

    wait_gather(i, slot)

    @pl.when(i >= 2)
    def _():
        wait_scatter(i - 2, slot)

    ncol = xbuf.shape[1]
    xb = jnp.concatenate([xbuf[slot, c] for c in range(ncol)], axis=-1).astype(jnp.bfloat16)
    hg = jnp.dot(xb, wgb[...], preferred_element_type=jnp.float32)
    hu = jnp.dot(xb, wub[...], preferred_element_type=jnp.float32)
    act = (hg * _sigmoid(hg)) * hu
    y = jnp.dot(act.astype(jnp.bfloat16), wdb[...], preferred_element_type=jnp.float32)
    for c in range(ncol):
        ybuf[slot, c] = y[:, c * 128:(c + 1) * 128]
    start_scatter(i, slot)

    @pl.when(i == n - 1)
    def _():
        @pl.when(n >= 2)
        def _():
            wait_scatter(i - 1, 1 - slot)
        wait_scatter(i, slot)


def _moe(h, block_expert, block_cnt, slot_tok, slot_dst, w_gate, w_up, w_down):
    S, T, _ = h.shape
    D = S * 128
    n_blocks = block_expert.shape[0]
    R = MOE_ROWS
    wmap = lambda i, be, bc, st, sd: (be[i], 0, 0)
    grid_spec = pltpu.PrefetchScalarGridSpec(
        num_scalar_prefetch=4,
        grid=(n_blocks,),
        in_specs=[
            pl.BlockSpec(memory_space=pl.ANY),
            pl.BlockSpec((1, D, MOE_HIDDEN), wmap),
            pl.BlockSpec((1, D, MOE_HIDDEN), wmap),
            pl.BlockSpec((1, MOE_HIDDEN, D), wmap),
        ],
        out_specs=pl.BlockSpec(memory_space=pl.ANY),
        scratch_shapes=[
            pltpu.VMEM((2, S, R, 128), jnp.float32),
            pltpu.VMEM((2, S, R, 128), jnp.float32),
            pltpu.VMEM((D, MOE_HIDDEN), jnp.bfloat16),
            pltpu.VMEM((D, MOE_HIDDEN), jnp.bfloat16),
            pltpu.VMEM((MOE_HIDDEN, D), jnp.bfloat16),
            pltpu.SemaphoreType.DMA((2,)),
            pltpu.SemaphoreType.DMA((2,)),
        ],
    )
    return pl.pallas_call(
        _moe_kernel,
        out_shape=jax.ShapeDtypeStruct((S, MOE_TOP_K * T, 128), jnp.float32),
        grid_spec=grid_spec,
        compiler_params=_cparams("arbitrary"),
        name="moe_experts",
    )(block_expert, block_cnt, slot_tok, slot_dst, h, w_gate, w_up, w_down)


def _combine_kernel(x_ref, y0_ref, y1_ref, w_ref, g_ref, *rest, with_norm):
    w = w_ref[...]
    z = x_ref[0] + g_ref[0, 0] * (w[:, 0:1] * _join_cols(y0_ref) + w[:, 1:2] * _join_cols(y1_ref))
    if not with_norm:
        rest[0][0] = z
        return
    nw_ref, sh_ref, sc_ref, z_ref, h_ref = rest
    z_ref[0] = z
    y = z * lax.rsqrt(jnp.mean(z * z, axis=-1, keepdims=True) + NORM_EPS)
    h_ref[0] = (y * nw_ref[...] * (1.0 + sc_ref[0, 0]) + sh_ref[0, 0]).astype(h_ref.dtype)


def _combine(z, y, weights, gate, norm=None):
    B, LT, D = z.shape
    S = D // 128
    nt = LT // NORM_ROWS
    seg = lambda b, t: (b, jnp.minimum(t, 1), 0, 0)
    tile = pl.BlockSpec((1, NORM_ROWS, D), lambda b, t: (b, t, 0))
    in_specs = [
        tile,
        pl.BlockSpec((S, NORM_ROWS, 128), lambda b, t: (0, b * nt + t, 0)),
        pl.BlockSpec((S, NORM_ROWS, 128), lambda b, t: (0, B * nt + b * nt + t, 0)),
        pl.BlockSpec((NORM_ROWS, MOE_TOP_K), lambda b, t: (b * nt + t, 0)),
        pl.BlockSpec((1, 1, 1, D), seg),
    ]
    args = [z, y, y, weights, gate]
    out_shape = [jax.ShapeDtypeStruct((B, LT, D), jnp.float32)]
    out_specs = [tile]
    if norm is not None:
        norm_w, shift, scale = norm
        in_specs += [pl.BlockSpec((1, D), lambda b, t: (0, 0)), pl.BlockSpec((1, 1, 1, D), seg),
                     pl.BlockSpec((1, 1, 1, D), seg)]
        args += [norm_w.reshape(1, D), shift, scale]
        out_shape.append(jax.ShapeDtypeStruct((B, LT, D), jnp.bfloat16))
        out_specs.append(tile)
    out = pl.pallas_call(
        functools.partial(_combine_kernel, with_norm=norm is not None),
        out_shape=out_shape,
        grid=(B, nt),
        in_specs=in_specs,
        out_specs=out_specs,
        compiler_params=_cparams("parallel", "parallel"),
        name="moe_combine",
    )(*args)
    return (out[0], out[1]) if norm is not None else (out[0], None)


def _dispatch_tables(expert_idx):
    T, K = expert_idx.shape
    E, R = MOE_EXPERTS, MOE_ROWS
    A = T * K
    n_blocks = A // R + E
    flat_e = expert_idx.reshape(A).astype(jnp.int32)
    order = jnp.argsort(flat_e).astype(jnp.int32)
    cnt_end = jnp.sum(flat_e[None, :] <= jnp.arange(E, dtype=jnp.int32)[:, None], axis=1, dtype=jnp.int32)
    cnt_start = jnp.concatenate([jnp.zeros((1,), jnp.int32), cnt_end[:-1]])
    counts = cnt_end - cnt_start
    nblk = (counts + R - 1) // R
    blk_end = jnp.cumsum(nblk)
    blk_start = blk_end - nblk
    blk = jnp.arange(n_blocks, dtype=jnp.int32)
    block_expert = jnp.minimum(jnp.sum(blk_end[None, :] <= blk[:, None], axis=1, dtype=jnp.int32), E - 1)
    first = (blk - blk_start[block_expert]) * R
    block_cnt = jnp.clip(counts[block_expert] - first, 0, R).astype(jnp.int32)
    row = jnp.arange(R, dtype=jnp.int32)[None, :]
    src = (cnt_start[block_expert] + first)[:, None] + row
    a = order[jnp.clip(src, 0, A - 1)]
    valid = row < block_cnt[:, None]
    slot_tok = jnp.where(valid, a // K, 0).reshape(-1)
    slot_dst = jnp.where(valid, (a % K) * T + a // K, 0).reshape(-1)
    return block_expert, block_cnt, slot_tok, slot_dst


def _bdot(a, b):
    return jnp.dot(a.astype(jnp.bfloat16), b.astype(jnp.bfloat16), preferred_element_type=jnp.float32)


def _bdot_nt(a, b):
    return lax.dot_general(a.astype(jnp.bfloat16), b.astype(jnp.bfloat16), (((1,), (1,)), ((), ())),
                           preferred_element_type=jnp.float32)


def _bdot_tn(a, b):
    return lax.dot_general(a.astype(jnp.bfloat16), b.astype(jnp.bfloat16), (((0,), (0,)), ((), ())),
                           preferred_element_type=jnp.float32)


def _seq_pos(LT):
    row = lax.broadcasted_iota(jnp.int32, (LT, 1), 0)
    in_ctx = row < CTX_LEN
    return jnp.where(in_ctx, row, row - CTX_LEN), jnp.where(in_ctx, CTX_LEN, LT - CTX_LEN)


def _conv_silu(x, cw, pos, seq_len):
    LT = x.shape[0]
    K = cw.shape[0]
    r = K // 2
    out = x * cw[r:r + 1, :]
    for j in range(K):
        s = r - j
        if s == 0:
            continue
        xs = pltpu.roll(x, s % LT, axis=0)
        valid = (pos >= s) if s > 0 else (pos - s < seq_len)
        out = out + jnp.where(valid, xs, 0.0) * cw[j:j + 1, :]
    return out * _sigmoid(out)


def _l2n(t):
    return t * lax.rsqrt(jnp.sum(t * t, axis=-1, keepdims=True) + NORM_EPS)


def _lane_col(x, idx):
    lane = lax.broadcasted_iota(jnp.int32, x.shape, 1)
    return jnp.sum(jnp.where(lane == idx, x, 0.0), axis=1, keepdims=True)


def _inv_unit_tri(Ns, eye, m16, moffs):
    Dg = [N * m16 for N in Ns]
    D2 = [_bdot(d, d) for d in Dg]
    D4 = [_bdot(d, d) for d in D2]
    D8 = [_bdot(d, d) for d in D4]
    T = [eye - d for d in Dg]
    for P in (D2, D4, D8):
        T = [t + _bdot(t, p) for t, p in zip(T, P)]
    for m in moffs:
        X = [_bdot(t, N * m) for t, N in zip(T, Ns)]
        T = [t - _bdot(x, t) for t, x in zip(T, X)]
    return T


def _gdn_kernel(q_ref, k_ref, v_ref, ab_ref, cwq_ref, cwk_ref, cwv_ref, alog_ref, dtb_ref, o_ref,
                qn_s, kn_s, vn_s, gb_s, u_s, w_s, qs_s, kst_s, qkd_s, el_s, o_s):
    f32 = jnp.float32
    bf16 = jnp.bfloat16
    h = pl.program_id(1)
    LT = q_ref.shape[1]
    C = GDN_ROWS
    nch = LT // C
    nctx = CTX_LEN // C

    pos, seq_len = _seq_pos(LT)
    qn_s[...] = _l2n(_conv_silu(q_ref[0], cwq_ref[...], pos, seq_len)) * (GDN_HEAD_DIM ** -0.5)
    kn_s[...] = _l2n(_conv_silu(k_ref[0], cwk_ref[...], pos, seq_len))
    vn_s[...] = _conv_silu(v_ref[0], cwv_ref[...], pos, seq_len)

    ab = ab_ref[0]
    xa = ab + dtb_ref[...]
    g_all = -jnp.exp(alog_ref[...]) * (jnp.maximum(xa, 0.0) + jnp.log1p(jnp.exp(-jnp.abs(xa))))
    b_all = _sigmoid(ab)
    lane = lax.broadcasted_iota(jnp.int32, (LT, 128), 1)
    gb = jnp.where(lane == 0, _lane_col(g_all, h), 0.0)
    gb = jnp.where(lane == 1, _lane_col(g_all, GDN_HEADS + h), gb)
    gb = jnp.where(lane == 2, _lane_col(b_all, 2 * GDN_HEADS + h), gb)
    gb = jnp.where(lane == 3, _lane_col(b_all, 3 * GDN_HEADS + h), gb)
    gb_s[...] = gb

    ri = lax.broadcasted_iota(jnp.int32, (C, C), 0)
    ci = lax.broadcasted_iota(jnp.int32, (C, C), 1)
    eye = (ri == ci).astype(f32)
    low = (ri >= ci).astype(f32)
    upp = (ri <= ci).astype(f32)
    m16 = ((ri // 16) == (ci // 16)).astype(f32)
    moffs = [(((ri // (2 * s)) == (ci // (2 * s))) & ((ri // s) != (ci // s))).astype(f32) for s in (16, 32, 64)]
    hi = lax.Precision.HIGHEST

    def chunk_group(i, carry):
        probs = []
        for j in range(GDN_CHUNKS_PER_STEP):
            c = i * GDN_CHUNKS_PER_STEP + j
            rows = pl.ds(pl.multiple_of(c * C, C), C)
            q = qn_s[rows, :]
            k = kn_s[rows, :]
            v = vn_s[rows, :]
            gbc = gb_s[rows, :]
            kk = _bdot_nt(k, k)
            qk = _bdot_nt(q, k)
            cs_f = jnp.dot(low, gbc, precision=hi, preferred_element_type=f32)
            cs_b = jnp.dot(upp, gbc, precision=hi, preferred_element_type=f32)
            for d in range(2):
                cs = cs_f if d == 0 else cs_b
                gc = cs[:, d:d + 1]
                gc_row = cs.T[d:d + 1, :]
                beta = gbc[:, 2 + d:3 + d]
                incl = low if d == 0 else upp
                decay = jnp.exp(jnp.where(incl > 0, gc - gc_row, -jnp.inf))
                g_last = gc[C - 1:C, :] if d == 0 else gc[0:1, :]
                probs.append(dict(d=d, c=c, q=q, k=k, v=v, qk=qk, gc=gc, beta=beta, decay=decay, g_last=g_last,
                                  N=(beta * kk) * decay * (1.0 - eye)))
        Ts = _inv_unit_tri([p["N"] for p in probs], eye, m16, moffs)
        for p, T in zip(probs, Ts):
            d, c, gc, beta = p["d"], p["c"], p["gc"], p["beta"]
            eg = jnp.exp(gc)
            u_s[d, c] = _bdot(T, p["v"] * beta)
            w_s[d, c] = _bdot(T, p["k"] * beta * eg).astype(bf16)
            qs_s[d, c] = (p["q"] * eg).astype(bf16)
            kst_s[d, c] = (p["k"] * jnp.exp(p["g_last"] - gc)).T.astype(bf16)
            qkd_s[d, c] = (p["qk"] * p["decay"]).astype(bf16)
            el_s[d, c] = jnp.broadcast_to(jnp.exp(p["g_last"]), (8, 128))
        return carry

    lax.fori_loop(0, nch // GDN_CHUNKS_PER_STEP, chunk_group, 0)

    def step(t, S):
        cb = jnp.where(t < nctx, nctx - 1 - t, nch - 1 + nctx - t)
        outs = []
        for d, (c, Sd) in enumerate(((t, S[0]), (cb, S[1]))):
            Sbf = Sd.astype(bf16)
            v_new = u_s[d, c] - jnp.dot(w_s[d, c], Sbf, preferred_element_type=f32)
            vb = v_new.astype(bf16)
            o_s[d, c] = (jnp.dot(qs_s[d, c], Sbf, preferred_element_type=f32)
                         + jnp.dot(qkd_s[d, c], vb, preferred_element_type=f32))
            outs.append(Sd * el_s[d, c][0:1, :] + jnp.dot(kst_s[d, c], vb, preferred_element_type=f32))
        return tuple(outs)

    z = jnp.zeros((GDN_HEAD_DIM, GDN_HEAD_DIM), f32)
    lax.fori_loop(0, nch, step, (z, z))
    o_ref[0] = (o_s[0] + o_s[1]).reshape(LT, GDN_HEAD_DIM)


def _gdn_mixer(p3, ab, conv_w, a_log, dt_bias):
    B, LT, _ = p3.shape
    H, Dh, C = GDN_HEADS, GDN_HEAD_DIM, GDN_ROWS
    nch = LT // C
    qb = C_QKV // Dh
    alog_row = jnp.zeros((1, 128), jnp.float32).at[0, :2 * H].set(a_log.reshape(-1))
    dtb_row = jnp.zeros((1, 128), jnp.float32).at[0, :2 * H].set(dt_bias.reshape(-1))
    taps = conv_w.shape[0]
    per_chunk = lambda dt: pltpu.VMEM((2, nch, C, Dh), dt)
    return pl.pallas_call(
        _gdn_kernel,
        out_shape=jax.ShapeDtypeStruct((B, LT, H * Dh), jnp.float32),
        grid=(B, H),
        in_specs=[
            pl.BlockSpec((1, LT, Dh), lambda b, h: (b, 0, qb + h)),
            pl.BlockSpec((1, LT, Dh), lambda b, h: (b, 0, qb + H + h)),
            pl.BlockSpec((1, LT, Dh), lambda b, h: (b, 0, qb + 2 * H + h)),
            pl.BlockSpec((1, LT, 128), lambda b, h: (b, 0, 0)),
            pl.BlockSpec((taps, Dh), lambda b, h: (0, h)),
            pl.BlockSpec((taps, Dh), lambda b, h: (0, H + h)),
            pl.BlockSpec((taps, Dh), lambda b, h: (0, 2 * H + h)),
            pl.BlockSpec((1, 128), lambda b, h: (0, 0)),
            pl.BlockSpec((1, 128), lambda b, h: (0, 0)),
        ],
        out_specs=pl.BlockSpec((1, LT, Dh), lambda b, h: (b, 0, h)),
        scratch_shapes=[
            pltpu.VMEM((LT, Dh), jnp.float32), pltpu.VMEM((LT, Dh), jnp.float32), pltpu.VMEM((LT, Dh), jnp.float32),
            pltpu.VMEM((LT, 128), jnp.float32),
            per_chunk(jnp.float32), per_chunk(jnp.bfloat16), per_chunk(jnp.bfloat16), per_chunk(jnp.bfloat16),
            per_chunk(jnp.bfloat16),
            pltpu.VMEM((2, nch, 8, 128), jnp.float32),
            per_chunk(jnp.float32),
        ],
        compiler_params=_cparams("parallel", "parallel"),
        name="gdn_mixer",
    )(p3, p3, p3, ab, conv_w, conv_w, conv_w, alog_row, dtb_row)


def _ret_kernel(lg_ref, q_ref, k_ref, v_ref, cos_ref, sin_ref, o_ref, qr_s, kr_s):
    f32 = jnp.float32
    h = pl.program_id(1)
    LT = q_ref.shape[1]
    C = RET_ROWS
    nch = LT // C
    nctx = CTX_LEN // C
    cos = cos_ref[...]
    sin = sin_ref[...]
    half = RET_KEY_DIM // 2
    q = q_ref[0]
    k = k_ref[0]
    qr_s[...] = (q * cos + pltpu.roll(q, half, axis=1) * sin) * (RET_KEY_DIM ** -0.5)
    kr_s[...] = k * cos + pltpu.roll(k, half, axis=1) * sin

    lg_f = lg_ref[0, h]
    lg_b = lg_ref[1, h]
    ri = lax.broadcasted_iota(jnp.int32, (C, C), 0)
    ci = lax.broadcasted_iota(jnp.int32, (C, C), 1)
    diff = (ri - ci).astype(f32)
    dm = (jnp.exp(jnp.where(diff >= 0, lg_f * diff, -jnp.inf))
          + jnp.exp(jnp.where(diff <= 0, -lg_b * diff, -jnp.inf)))
    pcol = lax.broadcasted_iota(jnp.int32, (C, 1), 0).astype(f32)
    cross_f = jnp.exp(lg_f * (pcol + 1.0))
    in_f = jnp.exp(lg_f * (C - 1.0 - pcol))
    cross_b = jnp.exp(lg_b * (C - pcol))
    in_b = jnp.exp(lg_b * pcol)
    cd_f = jnp.exp(lg_f * C)
    cd_b = jnp.exp(lg_b * C)

    def inner(c, carry):
        rows = pl.ds(pl.multiple_of(c * C, C), C)
        s = _bdot_nt(qr_s[rows, :], kr_s[rows, :])
        o_ref[0, rows, :] = _bdot(s * dm, v_ref[0, rows, :])
        return carry

    lax.fori_loop(0, nch, inner, 0)

    def step(t, S):
        cb = jnp.where(t < nctx, nctx - 1 - t, nch - 1 + nctx - t)
        outs = []
        for c, Sd, cr, ind, cd in ((t, S[0], cross_f, in_f, cd_f), (cb, S[1], cross_b, in_b, cd_b)):
            rows = pl.ds(pl.multiple_of(c * C, C), C)
            o_ref[0, rows, :] += _bdot(qr_s[rows, :], Sd) * cr
            outs.append(Sd * cd + _bdot_tn(kr_s[rows, :] * ind, v_ref[0, rows, :]))
        return tuple(outs)

    z = jnp.zeros((RET_KEY_DIM, RET_VAL_DIM), f32)
    lax.fori_loop(0, nch, step, (z, z))


def _ret_mixer(p3, log_gamma, cos_t, sin_t):
    B, LT, _ = p3.shape
    H, Dk, Dv = RET_HEADS, RET_KEY_DIM, RET_VAL_DIM
    grid_spec = pltpu.PrefetchScalarGridSpec(
        num_scalar_prefetch=1,
        grid=(B, H),
        in_specs=[
            pl.BlockSpec((1, LT, Dk), lambda b, h, lg: (b, 0, C_RQ // Dk + h)),
            pl.BlockSpec((1, LT, Dk), lambda b, h, lg: (b, 0, C_RK // Dk + h)),
            pl.BlockSpec((1, LT, Dv), lambda b, h, lg: (b, 0, C_RV // Dv + h)),
            pl.BlockSpec((LT, Dk), lambda b, h, lg: (0, 0)),
            pl.BlockSpec((LT, Dk), lambda b, h, lg: (0, 0)),
        ],
        out_specs=pl.BlockSpec((1, LT, Dv), lambda b, h, lg: (b, 0, h)),
        scratch_shapes=[pltpu.VMEM((LT, Dk), jnp.float32), pltpu.VMEM((LT, Dk), jnp.float32)],
    )
    return pl.pallas_call(
        _ret_kernel,
        out_shape=jax.ShapeDtypeStruct((B, LT, H * Dv), jnp.float32),
        grid_spec=grid_spec,
        compiler_params=_cparams("parallel", "parallel"),
        name="ret_mixer",
    )(log_gamma, p3, p3, p3, cos_t, sin_t)


def _axial_rope(rows, dim):
    n = dim // 4
    freq = ROPE_BASE ** (-jnp.arange(n, dtype=jnp.float32) / n)
    row = jnp.repeat(jnp.arange(rows, dtype=jnp.float32), GRID_W)
    col = (jnp.arange(rows * GRID_W) % GRID_W).astype(jnp.float32)
    ang = jnp.concatenate([row[:, None] * freq, col[:, None] * freq], axis=-1)
    ang = jnp.concatenate([ang, ang], axis=-1)[:, None, :]
    return jnp.cos(ang), jnp.sin(ang)


def _rope_tables(L):
    cos, sin = _axial_rope(L // GRID_W, RET_KEY_DIM)
    half = RET_KEY_DIM // 2
    sign = jnp.concatenate([-jnp.ones((half,), jnp.float32), jnp.ones((half,), jnp.float32)])
    cos_t = jnp.concatenate([jnp.ones((CTX_LEN, RET_KEY_DIM), jnp.float32), cos[:, 0, :]], axis=0)
    sin_t = jnp.concatenate([jnp.zeros((CTX_LEN, RET_KEY_DIM), jnp.float32), sin[:, 0, :] * sign], axis=0)
    return cos_t, sin_t


def _seg_table(ctx_vec, lat_vec):
    B, D = lat_vec.shape
    return jnp.stack([jnp.broadcast_to(ctx_vec, (B, D)), lat_vec], axis=1).reshape(B, 2, 1, D)


def kernel(x, c, ctx, c_ctx, mod_w, mod_b, norm1_w, w_in, gm_norm_w, gm_spatial_w, gm_spatial_b, gdn_conv_w, gdn_a_log, gdn_dt_bias, gdn_norm_w, ret_decay_logit, ret_norm_w, w_br_gm, w_br_gdn, w_br_ret, w_out, norm2_w, router_group_w, router_group_b, router_expert_w, router_expert_b, moe_w_gate, moe_w_up, moe_w_down, final_norm_w):
    B, L, D = x.shape
    depth = mod_w.shape[0]
    LT = CTX_LEN + L
    T = B * LT
    bf16 = jnp.bfloat16
    cos_t, sin_t = _rope_tables(L)

    w_in_t = jnp.swapaxes(w_in, 1, 2)
    wgm_b, wgdn_b, wret_b, wout_b = (t.astype(bf16) for t in (w_br_gm, w_br_gdn, w_br_ret, w_out))
    ws_b = gm_spatial_w.astype(bf16)
    bs_full = jnp.broadcast_to(gm_spatial_b[..., None], gm_spatial_b.shape + (GM_GROUP_CH,))
    pad = 128 - MOE_GROUPS - MOE_EXPERTS
    w_router = jnp.concatenate([router_group_w, router_expert_w, jnp.zeros((depth, D, pad), jnp.float32)], axis=-1)
    b_router = jnp.concatenate([router_group_b, router_expert_b, jnp.zeros((depth, pad), jnp.float32)], axis=-1)
    moe_wg, moe_wu = (t.reshape(depth * MOE_EXPERTS, D, MOE_HIDDEN) for t in (moe_w_gate, moe_w_up))
    moe_wd = moe_w_down.reshape(depth * MOE_EXPERTS, MOE_HIDDEN, D)
    main_rows = lambda j: j * PROJ_COLS + jnp.where(j * PROJ_COLS < AB_START, 0, AB_END - AB_START)

    cc = jnp.concatenate([jax.nn.silu(c), jax.nn.silu(c_ctx)[None], jnp.zeros((16 - B - 1, D), c.dtype)], axis=0)
    cc = cc.astype(bf16)
    mods = []
    for l in range(depth):
        mod = _matmul(cc, mod_w, l, 16, 1024, "adaln_mod") + mod_b[l]
        mods.append([_seg_table(mod[B, i * D:(i + 1) * D], mod[:B, i * D:(i + 1) * D]) for i in range(6)])

    z = jnp.concatenate([ctx, x], axis=1)
    h = _normmod(z, norm1_w[0], mods[0][0], mods[0][1], bf16)
    for l in range(depth):
        sh1, sc1, gt1, sh2, sc2, gt2 = mods[l]
        log_gamma = jax.nn.log_sigmoid(ret_decay_logit[l])

        h = h.reshape(T, D)
        p = _in_proj(h, w_in_t, l, LT, PROJ_COLS, P_COLS, main_rows, "in_proj")
        pab = _in_proj(h, w_in_t, l, LT, 128, 128, lambda j: j * 0 + AB_START, "in_proj_ab")
        p3 = p.reshape(B, LT, P_COLS)
        o_gdn = _gdn_mixer(p3, pab.reshape(B, LT, 128), gdn_conv_w[l], gdn_a_log[l], gdn_dt_bias[l])
        o_ret = _ret_mixer(p3, log_gamma, cos_t, sin_t)
        o_gdn = o_gdn.reshape(T, GDN_GATE_COLS)
        o_ret = o_ret.reshape(T, RET_V_COLS)

        merged = _merge(p, o_gdn, o_ret, gm_norm_w[l], ws_b[l], bs_full[l], gdn_norm_w[l], ret_norm_w[l],
                        wgm_b[l], wgdn_b[l], wret_b[l])
        z, h2, route = _out_proj_route(merged, wout_b[l], z, gt1, norm2_w[l], sh2, sc2, w_router[l], b_router[l])
        expert_idx = route[:, :MOE_TOP_K].astype(jnp.int32)
        weights = route[:, MOE_TOP_K:2 * MOE_TOP_K]
        block_expert, block_cnt, slot_tok, slot_dst = _dispatch_tables(expert_idx)
        y = _moe(h2, block_expert + l * MOE_EXPERTS, block_cnt, slot_tok, slot_dst, moe_wg, moe_wu, moe_wd)
        nxt = (norm1_w[l + 1], mods[l + 1][0], mods[l + 1][1]) if l + 1 < depth else None
        z, h = _combine(z, y, weights, gt2, nxt)
    return _final_norm(z, final_norm_w)
```

```python
import functools
import math

import jax
import jax.numpy as jnp
from jax import lax
from jax.experimental import pallas as pl
from jax.experimental.pallas import tpu as pltpu

D_MODEL = 2048
CTX_LEN = 256
GRID_W = 64
NORM_EPS = 1e-6

GM_CHUNK = 128
GM_GROUPS = 4
GM_GROUP_CH = 128
GM_WIDTH = GM_GROUPS * GM_GROUP_CH
GDN_HEADS = 4
GDN_HEAD_DIM = 128
GDN_ROWS = 128
GDN_CHUNKS_PER_STEP = 6
RET_HEADS = 4
RET_KEY_DIM = 128
RET_VAL_DIM = 256
RET_ROWS = 256
RET_CHUNKS_PER_STEP = 3
ROPE_BASE = 10000.0
MOE_GROUPS = 4
MOE_EXPERTS_PER_GROUP = 8
MOE_EXPERTS = MOE_GROUPS * MOE_EXPERTS_PER_GROUP
MOE_TOP_K = 2
MOE_HIDDEN = 512

GM_UV_COLS = 2 * GM_WIDTH
GDN_QKV_COLS = 3 * GDN_HEADS * GDN_HEAD_DIM
GDN_AB_COLS = 2 * GDN_HEADS
GDN_GATE_COLS = GDN_HEADS * GDN_HEAD_DIM
RET_QK_COLS = RET_HEADS * RET_KEY_DIM
RET_V_COLS = RET_HEADS * RET_VAL_DIM
MERGE_COLS = 3 * D_MODEL
AB_START = GM_UV_COLS + GDN_QKV_COLS
AB_END = AB_START + 2 * GDN_AB_COLS

C_UV = 0
C_QKV = C_UV + GM_UV_COLS
C_RQ = C_QKV + GDN_QKV_COLS
C_RK = C_RQ + RET_QK_COLS
C_RV = C_RK + RET_QK_COLS
C_GG = C_RV + RET_V_COLS
C_RG = C_GG + GDN_GATE_COLS
C_ML = C_RG + RET_V_COLS
P_COLS = C_ML + MERGE_COLS

V7X_VMEM_LIMIT_BYTES = 56 * 1024 * 1024
MOE_ROWS = 256
MOE_DMA_GROUP = 8
MERGE_ROWS = 256
NORM_ROWS = 256
PROJ_COLS = 512


def _cparams(*sem):
    return pltpu.CompilerParams(dimension_semantics=sem, vmem_limit_bytes=V7X_VMEM_LIMIT_BYTES)


def _normmod_kernel(z_ref, nw_ref, sh_ref, sc_ref, o_ref):
    z = z_ref[0]
    y = z * lax.rsqrt(jnp.mean(z * z, axis=-1, keepdims=True) + NORM_EPS)
    y = y * nw_ref[...]
    o_ref[0] = (y * (1.0 + sc_ref[0, 0]) + sh_ref[0, 0]).astype(o_ref.dtype)


def _normmod(z, norm_w, shift, scale, out_dtype):
    B, LT, D = z.shape
    seg = lambda b, t: (b, jnp.minimum(t, 1), 0, 0)
    assert CTX_LEN == NORM_ROWS
    return pl.pallas_call(
        _normmod_kernel,
        out_shape=jax.ShapeDtypeStruct((B, LT, D), out_dtype),
        grid=(B, LT // NORM_ROWS),
        in_specs=[
            pl.BlockSpec((1, NORM_ROWS, D), lambda b, t: (b, t, 0)),
            pl.BlockSpec((1, D), lambda b, t: (0, 0)),
            pl.BlockSpec((1, 1, 1, D), seg),
            pl.BlockSpec((1, 1, 1, D), seg),
        ],
        out_specs=pl.BlockSpec((1, NORM_ROWS, D), lambda b, t: (b, t, 0)),
        compiler_params=_cparams("parallel", "parallel"),
        name="normmod",
    )(z, norm_w.reshape(1, D), shift, scale)


def _split_cols(ref, x):
    for c in range(x.shape[1] // 128):
        ref[c] = x[:, c * 128:(c + 1) * 128]


def _join_cols(ref):
    return jnp.concatenate([ref[c] for c in range(ref.shape[0])], axis=-1)


def _route_rows(logits):
    neg = jnp.float32(-3.0e38)
    lane = lax.broadcasted_iota(jnp.int32, logits.shape, 1)
    lane_f = lane.astype(jnp.float32)
    first_max = lambda v: jnp.min(jnp.where(v == jnp.max(v, axis=1, keepdims=True), lane_f, 128.0), axis=1,
                                  keepdims=True)
    is_g = lane < MOE_GROUPS
    gl = jnp.where(is_g, logits, neg)
    g_max = jnp.max(gl, axis=1, keepdims=True)
    grp = first_max(gl).astype(jnp.int32)
    p_top = 1.0 / jnp.sum(jnp.where(is_g, jnp.exp(logits - g_max), 0.0), axis=1, keepdims=True)
    e_lane = lane - MOE_GROUPS
    in_grp = (e_lane >= 0) & (e_lane < MOE_EXPERTS) & ((e_lane // MOE_EXPERTS_PER_GROUP) == grp)
    el = jnp.where(in_grp, logits, neg)
    m1 = jnp.max(el, axis=1, keepdims=True)
    i1 = first_max(el)
    el2 = jnp.where(lane_f == i1, neg, el)
    m2 = jnp.max(el2, axis=1, keepdims=True)
    i2 = first_max(el2)
    r = jnp.exp(m2 - m1)
    w1 = p_top / (1.0 + r)
    out = jnp.where(lane == 0, i1 - MOE_GROUPS, 0.0)
    out = jnp.where(lane == 1, i2 - MOE_GROUPS, out)
    out = jnp.where(lane == 2, w1, out)
    return jnp.where(lane == 3, w1 * r, out)


def _out_route_kernel(a_ref, w_ref, x_ref, g_ref, nw_ref, sh_ref, sc_ref, whi_ref, wlo_ref, rb_ref,
                      z_ref, h_ref, rt_ref):
    z = x_ref[0] + g_ref[0, 0] * jnp.dot(a_ref[...], w_ref[...], preferred_element_type=jnp.float32)
    z_ref[0] = z
    y = z * lax.rsqrt(jnp.mean(z * z, axis=-1, keepdims=True) + NORM_EPS)
    h = y * nw_ref[...] * (1.0 + sc_ref[0, 0]) + sh_ref[0, 0]
    _split_cols(h_ref, h)
    hi = h.astype(jnp.bfloat16)
    lo = (h - hi.astype(jnp.float32)).astype(jnp.bfloat16)
    whi = whi_ref[...]
    logits = (jnp.dot(hi, whi, preferred_element_type=jnp.float32)
              + jnp.dot(lo, whi, preferred_element_type=jnp.float32)
              + jnp.dot(hi, wlo_ref[...], preferred_element_type=jnp.float32))
    rt_ref[...] = _route_rows(logits + rb_ref[...])


def _out_proj_route(merged, w_out, z, gate, norm_w, shift, scale, w_router, b_router):
    B, LT, D = z.shape
    nt = LT // NORM_ROWS
    seg = lambda b, t: (b, jnp.minimum(t, 1), 0, 0)
    tile = pl.BlockSpec((1, NORM_ROWS, D), lambda b, t: (b, t, 0))
    const = lambda r, c: pl.BlockSpec((r, c), lambda b, t: (0, 0))
    whi = w_router.astype(jnp.bfloat16)
    wlo = (w_router - whi.astype(jnp.float32)).astype(jnp.bfloat16)
    return pl.pallas_call(
        _out_route_kernel,
        out_shape=[jax.ShapeDtypeStruct((B, LT, D), jnp.float32),
                   jax.ShapeDtypeStruct((D // 128, B * LT, 128), jnp.float32),
                   jax.ShapeDtypeStruct((B * LT, 128), jnp.float32)],
        grid=(B, nt),
        in_specs=[
            pl.BlockSpec((NORM_ROWS, D), lambda b, t: (b * nt + t, 0)),
            const(D, D),
            tile,
            pl.BlockSpec((1, 1, 1, D), seg),
            const(1, D),
            pl.BlockSpec((1, 1, 1, D), seg),
            pl.BlockSpec((1, 1, 1, D), seg),
            const(D, 128),
            const(D, 128),
            const(1, 128),
        ],
        out_specs=[tile,
                   pl.BlockSpec((D // 128, NORM_ROWS, 128), lambda b, t: (0, b * nt + t, 0)),
                   pl.BlockSpec((NORM_ROWS, 128), lambda b, t: (b * nt + t, 0))],
        compiler_params=_cparams("parallel", "parallel"),
        name="out_proj_route",
    )(merged, w_out, z, gate, norm_w.reshape(1, D), shift, scale, whi, wlo, b_router.reshape(1, 128))


def _final_norm_kernel(z_ref, nw_ref, o_ref):
    z = z_ref[0]
    o_ref[0] = z * lax.rsqrt(jnp.mean(z * z, axis=-1, keepdims=True) + NORM_EPS) * nw_ref[...]


def _final_norm(z, norm_w):
    B, LT, D = z.shape
    skip = CTX_LEN // NORM_ROWS
    return pl.pallas_call(
        _final_norm_kernel,
        out_shape=jax.ShapeDtypeStruct((B, LT - CTX_LEN, D), jnp.float32),
        grid=(B, (LT - CTX_LEN) // NORM_ROWS),
        in_specs=[
            pl.BlockSpec((1, NORM_ROWS, D), lambda b, t: (b, t + skip, 0)),
            pl.BlockSpec((1, D), lambda b, t: (0, 0)),
        ],
        out_specs=pl.BlockSpec((1, NORM_ROWS, D), lambda b, t: (b, t, 0)),
        compiler_params=_cparams("parallel", "parallel"),
        name="final_norm",
    )(z, norm_w.reshape(1, D))


def _mm_kernel(a_ref, w_ref, o_ref):
    o_ref[...] = jnp.dot(a_ref[...], w_ref[...].astype(jnp.bfloat16), preferred_element_type=jnp.float32)


def _matmul(a, w, layer, tm, tn, name):
    M, K = a.shape
    N = w.shape[2]
    return pl.pallas_call(
        _mm_kernel,
        out_shape=jax.ShapeDtypeStruct((M, N), jnp.float32),
        grid=(M // tm, N // tn),
        in_specs=[
            pl.BlockSpec((tm, K), lambda i, j: (i, 0)),
            pl.BlockSpec((None, K, tn), lambda i, j: (layer, 0, j)),
        ],
        out_specs=pl.BlockSpec((tm, tn), lambda i, j: (i, j)),
        compiler_params=_cparams("parallel", "arbitrary"),
        name=name,
    )(a, w)


def _mm_nt_kernel(a_ref, w_ref, o_ref):
    w = w_ref[0].astype(jnp.bfloat16)
    o_ref[...] = lax.dot_general(a_ref[...], w, (((1,), (1,)), ((), ())), preferred_element_type=jnp.float32)


def _in_proj(a, w_t, layer, tm, tn, n_out, first_row, name):
    M, K = a.shape
    wmap = lambda i, j: (layer, pl.multiple_of(first_row(j), 8), 0)
    return pl.pallas_call(
        _mm_nt_kernel,
        out_shape=jax.ShapeDtypeStruct((M, n_out), jnp.float32),
        grid=(M // tm, n_out // tn),
        in_specs=[
            pl.BlockSpec((tm, K), lambda i, j: (i, 0)),
            pl.BlockSpec((pl.Element(1), pl.Element(tn), pl.Element(K)), wmap),
        ],
        out_specs=pl.BlockSpec((tm, tn), lambda i, j: (i, j)),
        compiler_params=_cparams("parallel", "arbitrary"),
        name=name,
    )(a, w_t)


def _gelu_exact(x):
    return 0.5 * x * (1.0 + lax.erf(x * (1.0 / math.sqrt(2.0))))


def _sigmoid(x):
    return 1.0 / (1.0 + jnp.exp2(x * (-1.0 / math.log(2.0))))


def _merge_kernel(uv_ref, gg_ref, rg_ref, ml_ref, og_ref, or_ref, gmn_ref, ws_ref, bs_ref, gdn_nw_ref,
                  ret_nw_ref, wgm_ref, wgdn_ref, wret_ref, o_ref):
    bf16 = jnp.bfloat16
    f32 = jnp.float32
    rows = uv_ref.shape[0]
    g = _gelu_exact(uv_ref[...])
    u = g[:, :GM_WIDTH]
    v = g[:, GM_WIDTH:]
    d = v - jnp.mean(v, axis=-1, keepdims=True)
    vn = (d * lax.rsqrt(jnp.mean(d * d, axis=-1, keepdims=True) + NORM_EPS) * gmn_ref[...]).astype(bf16)
    chunks = []
    for c in range(rows // GM_CHUNK):
        groups = []
        for gi in range(GM_GROUPS):
            blk = vn[c * GM_CHUNK:(c + 1) * GM_CHUNK, gi * GM_GROUP_CH:(gi + 1) * GM_GROUP_CH]
            groups.append(jnp.dot(ws_ref[gi], blk, preferred_element_type=f32) + bs_ref[gi])
        chunks.append(jnp.concatenate(groups, axis=-1))
    s = jnp.concatenate(chunks, axis=0)
    y_gm = jnp.dot((u * s).astype(bf16), wgm_ref[...], preferred_element_type=f32)

    og = og_ref[...]
    heads = []
    for h in range(GDN_HEADS):
        oh = og[:, h * GDN_HEAD_DIM:(h + 1) * GDN_HEAD_DIM]
        heads.append(oh * lax.rsqrt(jnp.mean(oh * oh, axis=-1, keepdims=True) + NORM_EPS) * gdn_nw_ref[...])
    gg = gg_ref[...]
    a_gdn = (jnp.concatenate(heads, axis=-1) * (gg * _sigmoid(gg))).astype(bf16)
    y_gdn = jnp.dot(a_gdn, wgdn_ref[...], preferred_element_type=f32)

    orr = or_ref[...]
    heads = []
    for h in range(RET_HEADS):
        oh = orr[:, h * RET_VAL_DIM:(h + 1) * RET_VAL_DIM]
        dh = oh - jnp.mean(oh, axis=-1, keepdims=True)
        heads.append(dh * lax.rsqrt(jnp.mean(dh * dh, axis=-1, keepdims=True) + NORM_EPS))
    rg = rg_ref[...]
    a_ret = (jnp.concatenate(heads, axis=-1) * ret_nw_ref[...] * (rg * _sigmoid(rg))).astype(bf16)
    y_ret = jnp.dot(a_ret, wret_ref[...], preferred_element_type=f32)

    sg = _sigmoid(ml_ref[...])
    merged = (sg[:, :D_MODEL] * y_gm + sg[:, D_MODEL:2 * D_MODEL] * y_gdn + sg[:, 2 * D_MODEL:] * y_ret)
    o_ref[...] = merged.astype(o_ref.dtype)


def _merge(p, o_gdn, o_ret, gm_norm_w, ws, bs, gdn_norm_w, ret_norm_w, wgm, wgdn, wret):
    M = p.shape[0]
    tm = MERGE_ROWS
    const = lambda *shape: pl.BlockSpec(shape, lambda i: (0,) * len(shape))
    return pl.pallas_call(
        _merge_kernel,
        out_shape=jax.ShapeDtypeStruct((M, D_MODEL), jnp.bfloat16),
        grid=(M // tm,),
        in_specs=[
            pl.BlockSpec((tm, GM_UV_COLS), lambda i: (i, C_UV // GM_UV_COLS)),
            pl.BlockSpec((tm, GDN_GATE_COLS), lambda i: (i, C_GG // GDN_GATE_COLS)),
            pl.BlockSpec((tm, RET_V_COLS), lambda i: (i, C_RG // RET_V_COLS)),
            pl.BlockSpec((tm, MERGE_COLS), lambda i: (i, C_ML // MERGE_COLS)),
            pl.BlockSpec((tm, GDN_GATE_COLS), lambda i: (i, 0)),
            pl.BlockSpec((tm, RET_V_COLS), lambda i: (i, 0)),
            const(1, GM_WIDTH),
            const(GM_GROUPS, GM_CHUNK, GM_CHUNK),
            const(GM_GROUPS, GM_CHUNK, GM_GROUP_CH),
            const(1, GDN_HEAD_DIM),
            const(1, RET_V_COLS),
            const(GM_WIDTH, D_MODEL),
            const(GDN_GATE_COLS, D_MODEL),
            const(RET_V_COLS, D_MODEL),
        ],
        out_specs=pl.BlockSpec((tm, D_MODEL), lambda i: (i, 0)),
        compiler_params=_cparams("parallel"),
        name="merge",
    )(p, p, p, p, o_gdn, o_ret, gm_norm_w.reshape(1, -1), ws, bs, gdn_norm_w.reshape(1, -1),
      ret_norm_w.reshape(1, -1), wgm, wgdn, wret)


def _moe_kernel(bexp_ref, bcnt_ref, stok_ref, sdst_ref,
                h_hbm, wg_ref, wu_ref, wd_ref, y_hbm,
                xbuf, ybuf, wgb, wub, wdb, gsem, ssem):
    i = pl.program_id(0)
    n = pl.num_programs(0)
    R = MOE_ROWS
    slot = lax.rem(i, 2)

    def gather_copy(blk, r, s):
        return pltpu.make_async_copy(h_hbm.at[:, stok_ref[blk * R + r], :], xbuf.at[s, :, r, :], gsem.at[s])

    def scatter_copy(blk, r, s):
        return pltpu.make_async_copy(ybuf.at[s, :, r, :], y_hbm.at[:, sdst_ref[blk * R + r], :], ssem.at[s])

    def for_rows(blk, fn):
        cnt = bcnt_ref[blk]
        full = lax.shift_right_logical(cnt, MOE_DMA_GROUP.bit_length() - 1)

        def group(g, c):
            for j in range(MOE_DMA_GROUP):
                fn(g * MOE_DMA_GROUP + j)
            return c
        lax.fori_loop(0, full, group, 0)
        lax.fori_loop(full * MOE_DMA_GROUP, cnt, lambda r, c: (fn(r), c)[1], 0)

    def start_gather(blk, s):
        for_rows(blk, lambda r: gather_copy(blk, r, s).start())

    def wait_gather(blk, s):
        for_rows(blk, lambda r: gather_copy(blk, r, s).wait())

    def start_scatter(blk, s):
        for_rows(blk, lambda r: scatter_copy(blk, r, s).start())

    def wait_scatter(blk, s):
        for_rows(blk, lambda r: scatter_copy(blk, r, s).wait())

    @pl.when(i == 0)
    def _():
        xbuf[...] = jnp.zeros_like(xbuf)
        start_gather(0, 0)

    @pl.when(i + 1 < n)
    def _():
        start_gather(i + 1, 1 - slot)

    changed = jnp.logical_or(i == 0, bexp_ref[i] != bexp_ref[jnp.maximum(i - 1, 0)])

    @pl.when(changed)
    def _():
        wgb[...] = wg_ref[0].astype(jnp.bfloat16)
        wub[...] = wu_ref[0].astype(jnp.bfloat16)
        wdb[...] = wd_ref[0].astype(jnp.bfloat16)

    wait_gather(i, slot)

    @pl.when(i >= 2)
    def _():
        wait_scatter(i - 2, slot)

    ncol = xbuf.shape[1]
    xb = jnp.concatenate([xbuf[slot, c] for c in range(ncol)], axis=-1).astype(jnp.bfloat16)
    hg = jnp.dot(xb, wgb[...], preferred_element_type=jnp.float32)
    hu = jnp.dot(xb, wub[...], preferred_element_type=jnp.float32)
    act = (hg * _sigmoid(hg)) * hu
    y = jnp.dot(act.astype(jnp.bfloat16), wdb[...], preferred_element_type=jnp.float32)
    for c in range(ncol):
        ybuf[slot, c] = y[:, c * 128:(c + 1) * 128]
    start_scatter(i, slot)

    @pl.when(i == n - 1)
    def _():
        @pl.when(n >= 2)
        def _():
            wait_scatter(i - 1, 1 - slot)
        wait_scatter(i, slot)


def _moe(h, block_expert, block_cnt, slot_tok, slot_dst, w_gate, w_up, w_down):
    S, T, _ = h.shape
    D = S * 128
    n_blocks = block_expert.shape[0]
    R = MOE_ROWS
    wmap = lambda i, be, bc, st, sd: (be[i], 0, 0)
    grid_spec = pltpu.PrefetchScalarGridSpec(
        num_scalar_prefetch=4,
        grid=(n_blocks,),
        in_specs=[
            pl.BlockSpec(memory_space=pl.ANY),
            pl.BlockSpec((1, D, MOE_HIDDEN), wmap),
            pl.BlockSpec((1, D, MOE_HIDDEN), wmap),
            pl.BlockSpec((1, MOE_HIDDEN, D), wmap),
        ],
        out_specs=pl.BlockSpec(memory_space=pl.ANY),
        scratch_shapes=[
            pltpu.VMEM((2, S, R, 128), jnp.float32),
            pltpu.VMEM((2, S, R, 128), jnp.float32),
            pltpu.VMEM((D, MOE_HIDDEN), jnp.bfloat16),
            pltpu.VMEM((D, MOE_HIDDEN), jnp.bfloat16),
            pltpu.VMEM((MOE_HIDDEN, D), jnp.bfloat16),
            pltpu.SemaphoreType.DMA((2,)),
            pltpu.SemaphoreType.DMA((2,)),
        ],
    )
    return pl.pallas_call(
        _moe_kernel,
        out_shape=jax.ShapeDtypeStruct((S, MOE_TOP_K * T, 128), jnp.float32),
        grid_spec=grid_spec,
        compiler_params=_cparams("arbitrary"),
        name="moe_experts",
    )(block_expert, block_cnt, slot_tok, slot_dst, h, w_gate, w_up, w_down)


def _combine_kernel(x_ref, y0_ref, y1_ref, w_ref, g_ref, *rest, with_norm):
    w = w_ref[...]
    z = x_ref[0] + g_ref[0, 0] * (w[:, 0:1] * _join_cols(y0_ref) + w[:, 1:2] * _join_cols(y1_ref))
    if not with_norm:
        rest[0][0] = z
        return
    nw_ref, sh_ref, sc_ref, z_ref, h_ref = rest
    z_ref[0] = z
    y = z * lax.rsqrt(jnp.mean(z * z, axis=-1, keepdims=True) + NORM_EPS)
    h_ref[0] = (y * nw_ref[...] * (1.0 + sc_ref[0, 0]) + sh_ref[0, 0]).astype(h_ref.dtype)


def _combine(z, y, weights, gate, norm=None):
    B, LT, D = z.shape
    S = D // 128
    nt = LT // NORM_ROWS
    seg = lambda b, t: (b, jnp.minimum(t, 1), 0, 0)
    tile = pl.BlockSpec((1, NORM_ROWS, D), lambda b, t: (b, t, 0))
    in_specs = [
        tile,
        pl.BlockSpec((S, NORM_ROWS, 128), lambda b, t: (0, b * nt + t, 0)),
        pl.BlockSpec((S, NORM_ROWS, 128), lambda b, t: (0, B * nt + b * nt + t, 0)),
        pl.BlockSpec((NORM_ROWS, MOE_TOP_K), lambda b, t: (b * nt + t, 0)),
        pl.BlockSpec((1, 1, 1, D), seg),
    ]
    args = [z, y, y, weights, gate]
    out_shape = [jax.ShapeDtypeStruct((B, LT, D), jnp.float32)]
    out_specs = [tile]
    if norm is not None:
        norm_w, shift, scale = norm
        in_specs += [pl.BlockSpec((1, D), lambda b, t: (0, 0)), pl.BlockSpec((1, 1, 1, D), seg),
                     pl.BlockSpec((1, 1, 1, D), seg)]
        args += [norm_w.reshape(1, D), shift, scale]
        out_shape.append(jax.ShapeDtypeStruct((B, LT, D), jnp.bfloat16))
        out_specs.append(tile)
    out = pl.pallas_call(
        functools.partial(_combine_kernel, with_norm=norm is not None),
        out_shape=out_shape,
        grid=(B, nt),
        in_specs=in_specs,
        out_specs=out_specs,
        compiler_params=_cparams("parallel", "parallel"),
        name="moe_combine",
    )(*args)
    return (out[0], out[1]) if norm is not None else (out[0], None)


def _dispatch_tables(expert_idx):
    T, K = expert_idx.shape
    E, R = MOE_EXPERTS, MOE_ROWS
    A = T * K
    n_blocks = A // R + E
    flat_e = expert_idx.reshape(A).astype(jnp.int32)
    order = jnp.argsort(flat_e).astype(jnp.int32)
    cnt_end = jnp.sum(flat_e[None, :] <= jnp.arange(E, dtype=jnp.int32)[:, None], axis=1, dtype=jnp.int32)
    cnt_start = jnp.concatenate([jnp.zeros((1,), jnp.int32), cnt_end[:-1]])
    counts = cnt_end - cnt_start
    nblk = (counts + R - 1) // R
    blk_end = jnp.cumsum(nblk)
    blk_start = blk_end - nblk
    blk = jnp.arange(n_blocks, dtype=jnp.int32)
    block_expert = jnp.minimum(jnp.sum(blk_end[None, :] <= blk[:, None], axis=1, dtype=jnp.int32), E - 1)
    first = (blk - blk_start[block_expert]) * R
    block_cnt = jnp.clip(counts[block_expert] - first, 0, R).astype(jnp.int32)
    row = jnp.arange(R, dtype=jnp.int32)[None, :]
    src = (cnt_start[block_expert] + first)[:, None] + row
    a = order[jnp.clip(src, 0, A - 1)]
    valid = row < block_cnt[:, None]
    slot_tok = jnp.where(valid, a // K, 0).reshape(-1)
    slot_dst = jnp.where(valid, (a % K) * T + a // K, 0).reshape(-1)
    return block_expert, block_cnt, slot_tok, slot_dst


def _bdot(a, b):
    return jnp.dot(a.astype(jnp.bfloat16), b.astype(jnp.bfloat16), preferred_element_type=jnp.float32)


def _bdot_nt(a, b):
    return lax.dot_general(a.astype(jnp.bfloat16), b.astype(jnp.bfloat16), (((1,), (1,)), ((), ())),
                           preferred_element_type=jnp.float32)


def _bdot_tn(a, b):
    return lax.dot_general(a.astype(jnp.bfloat16), b.astype(jnp.bfloat16), (((0,), (0,)), ((), ())),
                           preferred_element_type=jnp.float32)


def _seq_pos(LT):
    row = lax.broadcasted_iota(jnp.int32, (LT, 1), 0)
    in_ctx = row < CTX_LEN
    return jnp.where(in_ctx, row, row - CTX_LEN), jnp.where(in_ctx, CTX_LEN, LT - CTX_LEN)


def _conv_silu(x, cw, pos, seq_len):
    LT = x.shape[0]
    K = cw.shape[0]
    r = K // 2
    out = x * cw[r:r + 1, :]
    for j in range(K):
        s = r - j
        if s == 0:
            continue
        xs = pltpu.roll(x, s % LT, axis=0)
        valid = (pos >= s) if s > 0 else (pos - s < seq_len)
        out = out + jnp.where(valid, xs, 0.0) * cw[j:j + 1, :]
    return out * _sigmoid(out)


def _l2n(t):
    return t * lax.rsqrt(jnp.sum(t * t, axis=-1, keepdims=True) + NORM_EPS)


def _lane_col(x, idx):
    lane = lax.broadcasted_iota(jnp.int32, x.shape, 1)
    return jnp.sum(jnp.where(lane == idx, x, 0.0), axis=1, keepdims=True)


def _inv_unit_tri(Ns, eye, m16, moffs):
    Dg = [N * m16 for N in Ns]
    D2 = [_bdot(d, d) for d in Dg]
    D4 = [_bdot(d, d) for d in D2]
    D8 = [_bdot(d, d) for d in D4]
    T = [eye - d for d in Dg]
    for P in (D2, D4, D8):
        T = [t + _bdot(t, p) for t, p in zip(T, P)]
    for m in moffs:
        X = [_bdot(t, N * m) for t, N in zip(T, Ns)]
        T = [t - _bdot(x, t) for t, x in zip(T, X)]
    return T


def _gdn_kernel(q_ref, k_ref, v_ref, ab_ref, cwq_ref, cwk_ref, cwv_ref, alog_ref, dtb_ref, o_ref,
                qn_s, kn_s, vn_s, gb_s, gall_s, u_s, w_s, qs_s, kst_s, qkd_s, el_s, o_s):
    f32 = jnp.float32
    bf16 = jnp.bfloat16
    h = pl.program_id(1)
    LT = q_ref.shape[1]
    C = GDN_ROWS
    nch = LT // C
    nctx = CTX_LEN // C

    pos, seq_len = _seq_pos(LT)
    qn_s[...] = _l2n(_conv_silu(q_ref[0], cwq_ref[...], pos, seq_len)) * (GDN_HEAD_DIM ** -0.5)
    kn_s[...] = _l2n(_conv_silu(k_ref[0], cwk_ref[...], pos, seq_len))
    vn_s[...] = _conv_silu(v_ref[0], cwv_ref[...], pos, seq_len)

    lane = lax.broadcasted_iota(jnp.int32, (LT, 128), 1)

    @pl.when(h == 0)
    def _():
        ab = ab_ref[0]
        xa = ab + dtb_ref[...]
        g_all = -jnp.exp(alog_ref[...]) * (jnp.maximum(xa, 0.0) + jnp.log1p(jnp.exp(-jnp.abs(xa))))
        gall_s[...] = jnp.where(lane < 2 * GDN_HEADS, g_all, _sigmoid(ab))

    gb_all = gall_s[...]
    gb = jnp.where(lane == 0, _lane_col(gb_all, h), 0.0)
    for j in range(1, 4):
        gb = jnp.where(lane == j, _lane_col(gb_all, j * GDN_HEADS + h), gb)
    gb_s[...] = gb

    ri = lax.broadcasted_iota(jnp.int32, (C, C), 0)
    ci = lax.broadcasted_iota(jnp.int32, (C, C), 1)
    eye = (ri == ci).astype(f32)
    low = (ri >= ci).astype(f32)
    upp = (ri <= ci).astype(f32)
    m16 = ((ri // 16) == (ci // 16)).astype(f32)
    moffs = [(((ri // (2 * s)) == (ci // (2 * s))) & ((ri // s) != (ci // s))).astype(f32) for s in (16, 32, 64)]
    hi = lax.Precision.HIGHEST

    def chunk_group(i, carry):
        probs = []
        for j in range(GDN_CHUNKS_PER_STEP):
            c = i * GDN_CHUNKS_PER_STEP + j
            rows = pl.ds(pl.multiple_of(c * C, C), C)
            q = qn_s[rows, :]
            k = kn_s[rows, :]
            v = vn_s[rows, :]
            gbc = gb_s[rows, :]
            kk = _bdot_nt(k, k)
            qk = _bdot_nt(q, k)
            cs_f = jnp.dot(low, gbc, precision=hi, preferred_element_type=f32)
            cs_b = jnp.dot(upp, gbc, precision=hi, preferred_element_type=f32)
            for d in range(2):
                cs = cs_f if d == 0 else cs_b
                gc = cs[:, d:d + 1]
                gc_row = cs.T[d:d + 1, :]
                beta = gbc[:, 2 + d:3 + d]
                incl = low if d == 0 else upp
                decay = jnp.exp(jnp.where(incl > 0, gc - gc_row, -jnp.inf))
                g_last = gc[C - 1:C, :] if d == 0 else gc[0:1, :]
                probs.append(dict(d=d, c=c, q=q, k=k, v=v, qk=qk, gc=gc, beta=beta, decay=decay, g_last=g_last,
                                  N=(beta * kk) * decay * (1.0 - eye)))
        Ts = _inv_unit_tri([p["N"] for p in probs], eye, m16, moffs)
        for p, T in zip(probs, Ts):
            d, c, gc, beta = p["d"], p["c"], p["gc"], p["beta"]
            eg = jnp.exp(gc)
            u_s[d, c] = _bdot(T, p["v"] * beta)
            w_s[d, c] = _bdot(T, p["k"] * beta * eg).astype(bf16)
            qs_s[d, c] = (p["q"] * eg).astype(bf16)
            kst_s[d, c] = (p["k"] * jnp.exp(p["g_last"] - gc)).T.astype(bf16)
            qkd_s[d, c] = (p["qk"] * p["decay"]).astype(bf16)
            el_s[d, c] = jnp.broadcast_to(jnp.exp(p["g_last"]), (8, 128))
        return carry

    lax.fori_loop(0, nch // GDN_CHUNKS_PER_STEP, chunk_group, 0)

    def step(t, S):
        cb = jnp.where(t < nctx, nctx - 1 - t, nch - 1 + nctx - t)
        outs = []
        for d, (c, Sd) in enumerate(((t, S[0]), (cb, S[1]))):
            Sbf = Sd.astype(bf16)
            v_new = u_s[d, c] - jnp.dot(w_s[d, c], Sbf, preferred_element_type=f32)
            vb = v_new.astype(bf16)
            o_s[d, c] = (jnp.dot(qs_s[d, c], Sbf, preferred_element_type=f32)
                         + jnp.dot(qkd_s[d, c], vb, preferred_element_type=f32))
            outs.append(Sd * el_s[d, c][0:1, :] + jnp.dot(kst_s[d, c], vb, preferred_element_type=f32))
        return tuple(outs)

    z = jnp.zeros((GDN_HEAD_DIM, GDN_HEAD_DIM), f32)
    lax.fori_loop(0, nch, step, (z, z))
    o_ref[0] = (o_s[0] + o_s[1]).reshape(LT, GDN_HEAD_DIM)


def _gdn_mixer(p3, ab, conv_w, a_log, dt_bias):
    B, LT, _ = p3.shape
    H, Dh, C = GDN_HEADS, GDN_HEAD_DIM, GDN_ROWS
    nch = LT // C
    qb = C_QKV // Dh
    alog_row = jnp.zeros((1, 128), jnp.float32).at[0, :2 * H].set(a_log.reshape(-1))
    dtb_row = jnp.zeros((1, 128), jnp.float32).at[0, :2 * H].set(dt_bias.reshape(-1))
    taps = conv_w.shape[0]
    per_chunk = lambda dt: pltpu.VMEM((2, nch, C, Dh), dt)
    return pl.pallas_call(
        _gdn_kernel,
        out_shape=jax.ShapeDtypeStruct((B, LT, H * Dh), jnp.float32),
        grid=(B, H),
        in_specs=[
            pl.BlockSpec((1, LT, Dh), lambda b, h: (b, 0, qb + h)),
            pl.BlockSpec((1, LT, Dh), lambda b, h: (b, 0, qb + H + h)),
            pl.BlockSpec((1, LT, Dh), lambda b, h: (b, 0, qb + 2 * H + h)),
            pl.BlockSpec((1, LT, 128), lambda b, h: (b, 0, 0)),
            pl.BlockSpec((taps, Dh), lambda b, h: (0, h)),
            pl.BlockSpec((taps, Dh), lambda b, h: (0, H + h)),
            pl.BlockSpec((taps, Dh), lambda b, h: (0, 2 * H + h)),
            pl.BlockSpec((1, 128), lambda b, h: (0, 0)),
            pl.BlockSpec((1, 128), lambda b, h: (0, 0)),
        ],
        out_specs=pl.BlockSpec((1, LT, Dh), lambda b, h: (b, 0, h)),
        scratch_shapes=[
            pltpu.VMEM((LT, Dh), jnp.float32), pltpu.VMEM((LT, Dh), jnp.float32), pltpu.VMEM((LT, Dh), jnp.float32),
            pltpu.VMEM((LT, 128), jnp.float32), pltpu.VMEM((LT, 128), jnp.float32),
            per_chunk(jnp.float32), per_chunk(jnp.bfloat16), per_chunk(jnp.bfloat16), per_chunk(jnp.bfloat16),
            per_chunk(jnp.bfloat16),
            pltpu.VMEM((2, nch, 8, 128), jnp.float32),
            per_chunk(jnp.float32),
        ],
        compiler_params=_cparams("parallel", "arbitrary"),
        name="gdn_mixer",
    )(p3, p3, p3, ab, conv_w, conv_w, conv_w, alog_row, dtb_row)


def _ret_kernel(lg_ref, q_ref, k_ref, v_ref, cos_ref, sin_ref, o_ref, qr_s, kr_s, st_s):
    f32 = jnp.float32
    h = pl.program_id(1)
    LT = q_ref.shape[1]
    C = RET_ROWS
    nch = LT // C
    nctx = CTX_LEN // C
    cos = cos_ref[...]
    sin = sin_ref[...]
    half = RET_KEY_DIM // 2
    q = q_ref[0]
    k = k_ref[0]
    qr_s[...] = (q * cos + pltpu.roll(q, half, axis=1) * sin) * (RET_KEY_DIM ** -0.5)
    kr_s[...] = k * cos + pltpu.roll(k, half, axis=1) * sin

    lg_f = lg_ref[0, h]
    lg_b = lg_ref[1, h]
    ri = lax.broadcasted_iota(jnp.int32, (C, C), 0)
    ci = lax.broadcasted_iota(jnp.int32, (C, C), 1)
    diff = (ri - ci).astype(f32)
    dm = (jnp.exp(jnp.where(diff >= 0, lg_f * diff, -jnp.inf))
          + jnp.exp(jnp.where(diff <= 0, -lg_b * diff, -jnp.inf)))
    pcol = lax.broadcasted_iota(jnp.int32, (C, 1), 0).astype(f32)
    cross_f = jnp.exp(lg_f * (pcol + 1.0))
    in_f = jnp.exp(lg_f * (C - 1.0 - pcol))
    cross_b = jnp.exp(lg_b * (C - pcol))
    in_b = jnp.exp(lg_b * pcol)
    cd_f = jnp.exp(lg_f * C)
    cd_b = jnp.exp(lg_b * C)

    G = RET_CHUNKS_PER_STEP
    chunk_rows = lambda i, j: pl.ds(pl.multiple_of((i * G + j) * C, C), C)

    def local(i, carry):
        rs = [chunk_rows(i, j) for j in range(G)]
        qs = [qr_s[r, :] for r in rs]
        ks = [kr_s[r, :] for r in rs]
        vs = [v_ref[0, r, :] for r in rs]
        ss = [_bdot_nt(q, k) for q, k in zip(qs, ks)]
        os_ = [_bdot(s * dm, v) for s, v in zip(ss, vs)]
        mf = [_bdot_tn(k * in_f, v) for k, v in zip(ks, vs)]
        mb = [_bdot_tn(k * in_b, v) for k, v in zip(ks, vs)]
        for j in range(G):
            o_ref[0, rs[j], :] = os_[j]
            st_s[0, i * G + j] = mf[j]
            st_s[1, i * G + j] = mb[j]
        return carry

    lax.fori_loop(0, nch // G, local, 0)

    fwd_order = list(range(nch))
    bwd_order = list(range(nctx - 1, -1, -1)) + list(range(nch - 1, nctx - 1, -1))
    for d, order, cd in ((0, fwd_order, cd_f), (1, bwd_order, cd_b)):
        S = jnp.zeros((RET_KEY_DIM, RET_VAL_DIM), f32)
        for c in order:
            contrib = st_s[d, c]
            st_s[d, c] = S
            S = S * cd + contrib

    def cross(i, carry):
        rs = [chunk_rows(i, j) for j in range(G)]
        qs = [qr_s[r, :] for r in rs]
        cf = [_bdot(q, st_s[0, i * G + j]) for j, q in enumerate(qs)]
        cb = [_bdot(q, st_s[1, i * G + j]) for j, q in enumerate(qs)]
        for j in range(G):
            o_ref[0, rs[j], :] += cf[j] * cross_f + cb[j] * cross_b
        return carry

    lax.fori_loop(0, nch // G, cross, 0)


def _ret_mixer(p3, log_gamma, cos_t, sin_t):
    B, LT, _ = p3.shape
    H, Dk, Dv = RET_HEADS, RET_KEY_DIM, RET_VAL_DIM
    grid_spec = pltpu.PrefetchScalarGridSpec(
        num_scalar_prefetch=1,
        grid=(B, H),
        in_specs=[
            pl.BlockSpec((1, LT, Dk), lambda b, h, lg: (b, 0, C_RQ // Dk + h)),
            pl.BlockSpec((1, LT, Dk), lambda b, h, lg: (b, 0, C_RK // Dk + h)),
            pl.BlockSpec((1, LT, Dv), lambda b, h, lg: (b, 0, C_RV // Dv + h)),
            pl.BlockSpec((LT, Dk), lambda b, h, lg: (0, 0)),
            pl.BlockSpec((LT, Dk), lambda b, h, lg: (0, 0)),
        ],
        out_specs=pl.BlockSpec((1, LT, Dv), lambda b, h, lg: (b, 0, h)),
        scratch_shapes=[pltpu.VMEM((LT, Dk), jnp.float32), pltpu.VMEM((LT, Dk), jnp.float32),
                        pltpu.VMEM((2, LT // RET_ROWS, Dk, Dv), jnp.float32)],
    )
    return pl.pallas_call(
        _ret_kernel,
        out_shape=jax.ShapeDtypeStruct((B, LT, H * Dv), jnp.float32),
        grid_spec=grid_spec,
        compiler_params=_cparams("parallel", "parallel"),
        name="ret_mixer",
    )(log_gamma, p3, p3, p3, cos_t, sin_t)


def _axial_rope(rows, dim):
    n = dim // 4
    freq = ROPE_BASE ** (-jnp.arange(n, dtype=jnp.float32) / n)
    row = jnp.repeat(jnp.arange(rows, dtype=jnp.float32), GRID_W)
    col = (jnp.arange(rows * GRID_W) % GRID_W).astype(jnp.float32)
    ang = jnp.concatenate([row[:, None] * freq, col[:, None] * freq], axis=-1)
    ang = jnp.concatenate([ang, ang], axis=-1)[:, None, :]
    return jnp.cos(ang), jnp.sin(ang)


def _rope_tables(L):
    cos, sin = _axial_rope(L // GRID_W, RET_KEY_DIM)
    half = RET_KEY_DIM // 2
    sign = jnp.concatenate([-jnp.ones((half,), jnp.float32), jnp.ones((half,), jnp.float32)])
    cos_t = jnp.concatenate([jnp.ones((CTX_LEN, RET_KEY_DIM), jnp.float32), cos[:, 0, :]], axis=0)
    sin_t = jnp.concatenate([jnp.zeros((CTX_LEN, RET_KEY_DIM), jnp.float32), sin[:, 0, :] * sign], axis=0)
    return cos_t, sin_t


def _seg_table(ctx_vec, lat_vec):
    B, D = lat_vec.shape
    return jnp.stack([jnp.broadcast_to(ctx_vec, (B, D)), lat_vec], axis=1).reshape(B, 2, 1, D)


def kernel(x, c, ctx, c_ctx, mod_w, mod_b, norm1_w, w_in, gm_norm_w, gm_spatial_w, gm_spatial_b, gdn_conv_w, gdn_a_log, gdn_dt_bias, gdn_norm_w, ret_decay_logit, ret_norm_w, w_br_gm, w_br_gdn, w_br_ret, w_out, norm2_w, router_group_w, router_group_b, router_expert_w, router_expert_b, moe_w_gate, moe_w_up, moe_w_down, final_norm_w):
    B, L, D = x.shape
    depth = mod_w.shape[0]
    LT = CTX_LEN + L
    T = B * LT
    bf16 = jnp.bfloat16
    cos_t, sin_t = _rope_tables(L)

    w_in_t = jnp.swapaxes(w_in, 1, 2)
    wgm_b, wgdn_b, wret_b, wout_b = (t.astype(bf16) for t in (w_br_gm, w_br_gdn, w_br_ret, w_out))
    ws_b = gm_spatial_w.astype(bf16)
    bs_full = jnp.broadcast_to(gm_spatial_b[..., None], gm_spatial_b.shape + (GM_GROUP_CH,))
    pad = 128 - MOE_GROUPS - MOE_EXPERTS
    w_router = jnp.concatenate([router_group_w, router_expert_w, jnp.zeros((depth, D, pad), jnp.float32)], axis=-1)
    b_router = jnp.concatenate([router_group_b, router_expert_b, jnp.zeros((depth, pad), jnp.float32)], axis=-1)
    moe_wg, moe_wu = (t.reshape(depth * MOE_EXPERTS, D, MOE_HIDDEN) for t in (moe_w_gate, moe_w_up))
    moe_wd = moe_w_down.reshape(depth * MOE_EXPERTS, MOE_HIDDEN, D)
    main_rows = lambda j: j * PROJ_COLS + jnp.where(j * PROJ_COLS < AB_START, 0, AB_END - AB_START)

    cc = jnp.concatenate([jax.nn.silu(c), jax.nn.silu(c_ctx)[None], jnp.zeros((16 - B - 1, D), c.dtype)], axis=0)
    cc = cc.astype(bf16)
    mods = []
    for l in range(depth):
        mod = _matmul(cc, mod_w, l, 16, 1024, "adaln_mod") + mod_b[l]
        mods.append([_seg_table(mod[B, i * D:(i + 1) * D], mod[:B, i * D:(i + 1) * D]) for i in range(6)])

    z = jnp.concatenate([ctx, x], axis=1)
    h = _normmod(z, norm1_w[0], mods[0][0], mods[0][1], bf16)
    for l in range(depth):
        sh1, sc1, gt1, sh2, sc2, gt2 = mods[l]
        log_gamma = jax.nn.log_sigmoid(ret_decay_logit[l])

        h = h.reshape(T, D)
        p = _in_proj(h, w_in_t, l, LT, PROJ_COLS, P_COLS, main_rows, "in_proj")
        pab = _in_proj(h, w_in_t, l, LT, 128, 128, lambda j: j * 0 + AB_START, "in_proj_ab")
        p3 = p.reshape(B, LT, P_COLS)
        o_gdn = _gdn_mixer(p3, pab.reshape(B, LT, 128), gdn_conv_w[l], gdn_a_log[l], gdn_dt_bias[l])
        o_ret = _ret_mixer(p3, log_gamma, cos_t, sin_t)
        o_gdn = o_gdn.reshape(T, GDN_GATE_COLS)
        o_ret = o_ret.reshape(T, RET_V_COLS)

        merged = _merge(p, o_gdn, o_ret, gm_norm_w[l], ws_b[l], bs_full[l], gdn_norm_w[l], ret_norm_w[l],
                        wgm_b[l], wgdn_b[l], wret_b[l])
        z, h2, route = _out_proj_route(merged, wout_b[l], z, gt1, norm2_w[l], sh2, sc2, w_router[l], b_router[l])
        expert_idx = route[:, :MOE_TOP_K].astype(jnp.int32)
        weights = route[:, MOE_TOP_K:2 * MOE_TOP_K]
        block_expert, block_cnt, slot_tok, slot_dst = _dispatch_tables(expert_idx)
        y = _moe(h2, block_expert + l * MOE_EXPERTS, block_cnt, slot_tok, slot_dst, moe_wg, moe_wu, moe_wd)
        nxt = (norm1_w[l + 1], mods[l + 1][0], mods[l + 1][1]) if l + 1 < depth else None
        z, h = _combine(z, y, weights, gt2, nxt)
    return _final_norm(z, final_norm_w)
```

```python
import functools
import math

import jax
import jax.numpy as jnp
from jax import lax
from jax.experimental import pallas as pl
from jax.experimental.pallas import tpu as pltpu

D_MODEL = 2048
CTX_LEN = 256
GRID_W = 64
NORM_EPS = 1e-6

GM_CHUNK = 128
GM_GROUPS = 4
GM_GROUP_CH = 128
GM_WIDTH = GM_GROUPS * GM_GROUP_CH
GDN_HEADS = 4
GDN_HEAD_DIM = 128
GDN_ROWS = 128
GDN_CHUNKS_PER_STEP = 6
GDN_HEADS_PER_STEP = 2
CONV_PAD = 8
RET_HEADS = 4
RET_KEY_DIM = 128
RET_VAL_DIM = 256
RET_ROWS = 256
RET_CHUNKS_PER_STEP = 3
ROPE_BASE = 10000.0
MOE_GROUPS = 4
MOE_EXPERTS_PER_GROUP = 8
MOE_EXPERTS = MOE_GROUPS * MOE_EXPERTS_PER_GROUP
MOE_TOP_K = 2
MOE_HIDDEN = 512

GM_UV_COLS = 2 * GM_WIDTH
GDN_QKV_COLS = 3 * GDN_HEADS * GDN_HEAD_DIM
GDN_AB_COLS = 2 * GDN_HEADS
GDN_GATE_COLS = GDN_HEADS * GDN_HEAD_DIM
RET_QK_COLS = RET_HEADS * RET_KEY_DIM
RET_V_COLS = RET_HEADS * RET_VAL_DIM
MERGE_COLS = 3 * D_MODEL
AB_START = GM_UV_COLS + GDN_QKV_COLS
AB_END = AB_START + 2 * GDN_AB_COLS

C_UV = 0
C_QKV = C_UV + GM_UV_COLS
C_RQ = C_QKV + GDN_QKV_COLS
C_RK = C_RQ + RET_QK_COLS
C_RV = C_RK + RET_QK_COLS
C_GG = C_RV + RET_V_COLS
C_RG = C_GG + GDN_GATE_COLS
C_ML = C_RG + RET_V_COLS
P_COLS = C_ML + MERGE_COLS

V7X_VMEM_LIMIT_BYTES = 56 * 1024 * 1024
MOE_ROWS = 256
MOE_DMA_GROUP = 8
MERGE_ROWS = 256
NORM_ROWS = 256
PROJ_COLS = 512


def _cparams(*sem):
    return pltpu.CompilerParams(dimension_semantics=sem, vmem_limit_bytes=V7X_VMEM_LIMIT_BYTES)


def _normmod_kernel(z_ref, nw_ref, sh_ref, sc_ref, o_ref):
    z = z_ref[0]
    y = z * lax.rsqrt(jnp.mean(z * z, axis=-1, keepdims=True) + NORM_EPS)
    y = y * nw_ref[...]
    o_ref[0] = (y * (1.0 + sc_ref[0, 0]) + sh_ref[0, 0]).astype(o_ref.dtype)


def _normmod(z, norm_w, shift, scale, out_dtype):
    B, LT, D = z.shape
    seg = lambda b, t: (b, jnp.minimum(t, 1), 0, 0)
    assert CTX_LEN == NORM_ROWS
    return pl.pallas_call(
        _normmod_kernel,
        out_shape=jax.ShapeDtypeStruct((B, LT, D), out_dtype),
        grid=(B, LT // NORM_ROWS),
        in_specs=[
            pl.BlockSpec((1, NORM_ROWS, D), lambda b, t: (b, t, 0)),
            pl.BlockSpec((1, D), lambda b, t: (0, 0)),
            pl.BlockSpec((1, 1, 1, D), seg),
            pl.BlockSpec((1, 1, 1, D), seg),
        ],
        out_specs=pl.BlockSpec((1, NORM_ROWS, D), lambda b, t: (b, t, 0)),
        compiler_params=_cparams("parallel", "parallel"),
        name="normmod",
    )(z, norm_w.reshape(1, D), shift, scale)


def _split_cols(ref, x):
    for c in range(x.shape[1] // 128):
        ref[c] = x[:, c * 128:(c + 1) * 128]


def _join_cols(ref):
    return jnp.concatenate([ref[c] for c in range(ref.shape[0])], axis=-1)


def _route_rows(logits):
    neg = jnp.float32(-3.0e38)
    lane = lax.broadcasted_iota(jnp.int32, logits.shape, 1)
    lane_f = lane.astype(jnp.float32)
    first_max = lambda v: jnp.min(jnp.where(v == jnp.max(v, axis=1, keepdims=True), lane_f, 128.0), axis=1,
                                  keepdims=True)
    is_g = lane < MOE_GROUPS
    gl = jnp.where(is_g, logits, neg)
    g_max = jnp.max(gl, axis=1, keepdims=True)
    grp = first_max(gl).astype(jnp.int32)
    p_top = 1.0 / jnp.sum(jnp.where(is_g, jnp.exp(logits - g_max), 0.0), axis=1, keepdims=True)
    e_lane = lane - MOE_GROUPS
    in_grp = (e_lane >= 0) & (e_lane < MOE_EXPERTS) & ((e_lane // MOE_EXPERTS_PER_GROUP) == grp)
    el = jnp.where(in_grp, logits, neg)
    m1 = jnp.max(el, axis=1, keepdims=True)
    i1 = first_max(el)
    el2 = jnp.where(lane_f == i1, neg, el)
    m2 = jnp.max(el2, axis=1, keepdims=True)
    i2 = first_max(el2)
    r = jnp.exp(m2 - m1)
    w1 = p_top / (1.0 + r)
    out = jnp.where(lane == 0, i1 - MOE_GROUPS, 0.0)
    out = jnp.where(lane == 1, i2 - MOE_GROUPS, out)
    out = jnp.where(lane == 2, w1, out)
    return jnp.where(lane == 3, w1 * r, out)


def _out_route_kernel(a_ref, w_ref, x_ref, g_ref, nw_ref, sh_ref, sc_ref, whi_ref, wlo_ref, rb_ref,
                      z_ref, h_ref, rt_ref):
    z = x_ref[0] + g_ref[0, 0] * jnp.dot(a_ref[...], w_ref[...], preferred_element_type=jnp.float32)
    z_ref[0] = z
    y = z * lax.rsqrt(jnp.mean(z * z, axis=-1, keepdims=True) + NORM_EPS)
    h = y * nw_ref[...] * (1.0 + sc_ref[0, 0]) + sh_ref[0, 0]
    _split_cols(h_ref, h)
    hi = h.astype(jnp.bfloat16)
    lo = (h - hi.astype(jnp.float32)).astype(jnp.bfloat16)
    whi = whi_ref[...]
    logits = (jnp.dot(hi, whi, preferred_element_type=jnp.float32)
              + jnp.dot(lo, whi, preferred_element_type=jnp.float32)
              + jnp.dot(hi, wlo_ref[...], preferred_element_type=jnp.float32))
    rt_ref[...] = _route_rows(logits + rb_ref[...])


def _out_proj_route(merged, w_out, z, gate, norm_w, shift, scale, w_router, b_router):
    B, LT, D = z.shape
    nt = LT // NORM_ROWS
    seg = lambda b, t: (b, jnp.minimum(t, 1), 0, 0)
    tile = pl.BlockSpec((1, NORM_ROWS, D), lambda b, t: (b, t, 0))
    const = lambda r, c: pl.BlockSpec((r, c), lambda b, t: (0, 0))
    whi = w_router.astype(jnp.bfloat16)
    wlo = (w_router - whi.astype(jnp.float32)).astype(jnp.bfloat16)
    return pl.pallas_call(
        _out_route_kernel,
        out_shape=[jax.ShapeDtypeStruct((B, LT, D), jnp.float32),
                   jax.ShapeDtypeStruct((D // 128, B * LT, 128), jnp.float32),
                   jax.ShapeDtypeStruct((B * LT, 128), jnp.float32)],
        grid=(B, nt),
        in_specs=[
            pl.BlockSpec((NORM_ROWS, D), lambda b, t: (b * nt + t, 0)),
            const(D, D),
            tile,
            pl.BlockSpec((1, 1, 1, D), seg),
            const(1, D),
            pl.BlockSpec((1, 1, 1, D), seg),
            pl.BlockSpec((1, 1, 1, D), seg),
            const(D, 128),
            const(D, 128),
            const(1, 128),
        ],
        out_specs=[tile,
                   pl.BlockSpec((D // 128, NORM_ROWS, 128), lambda b, t: (0, b * nt + t, 0)),
                   pl.BlockSpec((NORM_ROWS, 128), lambda b, t: (b * nt + t, 0))],
        compiler_params=_cparams("parallel", "parallel"),
        name="out_proj_route",
    )(merged, w_out, z, gate, norm_w.reshape(1, D), shift, scale, whi, wlo, b_router.reshape(1, 128))


def _final_norm_kernel(z_ref, nw_ref, o_ref):
    z = z_ref[0]
    o_ref[0] = z * lax.rsqrt(jnp.mean(z * z, axis=-1, keepdims=True) + NORM_EPS) * nw_ref[...]


def _final_norm(z, norm_w):
    B, LT, D = z.shape
    skip = CTX_LEN // NORM_ROWS
    return pl.pallas_call(
        _final_norm_kernel,
        out_shape=jax.ShapeDtypeStruct((B, LT - CTX_LEN, D), jnp.float32),
        grid=(B, (LT - CTX_LEN) // NORM_ROWS),
        in_specs=[
            pl.BlockSpec((1, NORM_ROWS, D), lambda b, t: (b, t + skip, 0)),
            pl.BlockSpec((1, D), lambda b, t: (0, 0)),
        ],
        out_specs=pl.BlockSpec((1, NORM_ROWS, D), lambda b, t: (b, t, 0)),
        compiler_params=_cparams("parallel", "parallel"),
        name="final_norm",
    )(z, norm_w.reshape(1, D))


def _mm_kernel(a_ref, w_ref, o_ref):
    o_ref[...] = jnp.dot(a_ref[...], w_ref[...].astype(jnp.bfloat16), preferred_element_type=jnp.float32)


def _matmul(a, w, layer, tm, tn, name):
    M, K = a.shape
    N = w.shape[2]
    return pl.pallas_call(
        _mm_kernel,
        out_shape=jax.ShapeDtypeStruct((M, N), jnp.float32),
        grid=(M // tm, N // tn),
        in_specs=[
            pl.BlockSpec((tm, K), lambda i, j: (i, 0)),
            pl.BlockSpec((None, K, tn), lambda i, j: (layer, 0, j)),
        ],
        out_specs=pl.BlockSpec((tm, tn), lambda i, j: (i, j)),
        compiler_params=_cparams("parallel", "arbitrary"),
        name=name,
    )(a, w)


def _mm_nt_kernel(a_ref, w_ref, o_ref):
    w = w_ref[0].astype(jnp.bfloat16)
    o_ref[...] = lax.dot_general(a_ref[...], w, (((1,), (1,)), ((), ())), preferred_element_type=jnp.float32)


def _in_proj(a, w_t, layer, tm, tn, n_out, first_row, name):
    M, K = a.shape
    wmap = lambda i, j: (layer, pl.multiple_of(first_row(j), 8), 0)
    return pl.pallas_call(
        _mm_nt_kernel,
        out_shape=jax.ShapeDtypeStruct((M, n_out), jnp.float32),
        grid=(M // tm, n_out // tn),
        in_specs=[
            pl.BlockSpec((tm, K), lambda i, j: (i, 0)),
            pl.BlockSpec((pl.Element(1), pl.Element(tn), pl.Element(K)), wmap),
        ],
        out_specs=pl.BlockSpec((tm, tn), lambda i, j: (i, j)),
        compiler_params=_cparams("parallel", "arbitrary"),
        name=name,
    )(a, w_t)


def _gelu_exact(x):
    return 0.5 * x * (1.0 + lax.erf(x * (1.0 / math.sqrt(2.0))))


def _sigmoid(x):
    return 1.0 / (1.0 + jnp.exp2(x * (-1.0 / math.log(2.0))))


def _merge_kernel(uv_ref, gg_ref, rg_ref, ml_ref, og_ref, or_ref, gmn_ref, ws_ref, bs_ref, gdn_nw_ref,
                  ret_nw_ref, wgm_ref, wgdn_ref, wret_ref, o_ref):
    bf16 = jnp.bfloat16
    f32 = jnp.float32
    rows = uv_ref.shape[0]
    g = _gelu_exact(uv_ref[...])
    u = g[:, :GM_WIDTH]
    v = g[:, GM_WIDTH:]
    d = v - jnp.mean(v, axis=-1, keepdims=True)
    vn = (d * lax.rsqrt(jnp.mean(d * d, axis=-1, keepdims=True) + NORM_EPS) * gmn_ref[...]).astype(bf16)
    chunks = []
    for c in range(rows // GM_CHUNK):
        groups = []
        for gi in range(GM_GROUPS):
            blk = vn[c * GM_CHUNK:(c + 1) * GM_CHUNK, gi * GM_GROUP_CH:(gi + 1) * GM_GROUP_CH]
            groups.append(jnp.dot(ws_ref[gi], blk, preferred_element_type=f32) + bs_ref[gi])
        chunks.append(jnp.concatenate(groups, axis=-1))
    s = jnp.concatenate(chunks, axis=0)
    y_gm = jnp.dot((u * s).astype(bf16), wgm_ref[...], preferred_element_type=f32)

    og = og_ref[...]
    heads = []
    for h in range(GDN_HEADS):
        oh = og[:, h * GDN_HEAD_DIM:(h + 1) * GDN_HEAD_DIM]
        heads.append(oh * lax.rsqrt(jnp.mean(oh * oh, axis=-1, keepdims=True) + NORM_EPS) * gdn_nw_ref[...])
    gg = gg_ref[...]
    a_gdn = (jnp.concatenate(heads, axis=-1) * (gg * _sigmoid(gg))).astype(bf16)
    y_gdn = jnp.dot(a_gdn, wgdn_ref[...], preferred_element_type=f32)

    orr = or_ref[...]
    heads = []
    for h in range(RET_HEADS):
        oh = orr[:, h * RET_VAL_DIM:(h + 1) * RET_VAL_DIM]
        dh = oh - jnp.mean(oh, axis=-1, keepdims=True)
        heads.append(dh * lax.rsqrt(jnp.mean(dh * dh, axis=-1, keepdims=True) + NORM_EPS))
    rg = rg_ref[...]
    a_ret = (jnp.concatenate(heads, axis=-1) * ret_nw_ref[...] * (rg * _sigmoid(rg))).astype(bf16)
    y_ret = jnp.dot(a_ret, wret_ref[...], preferred_element_type=f32)

    sg = _sigmoid(ml_ref[...])
    merged = (sg[:, :D_MODEL] * y_gm + sg[:, D_MODEL:2 * D_MODEL] * y_gdn + sg[:, 2 * D_MODEL:] * y_ret)
    o_ref[...] = merged.astype(o_ref.dtype)


def _merge(p, o_gdn, o_ret, gm_norm_w, ws, bs, gdn_norm_w, ret_norm_w, wgm, wgdn, wret):
    M = p.shape[0]
    tm = MERGE_ROWS
    const = lambda *shape: pl.BlockSpec(shape, lambda i: (0,) * len(shape))
    return pl.pallas_call(
        _merge_kernel,
        out_shape=jax.ShapeDtypeStruct((M, D_MODEL), jnp.bfloat16),
        grid=(M // tm,),
        in_specs=[
            pl.BlockSpec((tm, GM_UV_COLS), lambda i: (i, C_UV // GM_UV_COLS)),
            pl.BlockSpec((tm, GDN_GATE_COLS), lambda i: (i, C_GG // GDN_GATE_COLS)),
            pl.BlockSpec((tm, RET_V_COLS), lambda i: (i, C_RG // RET_V_COLS)),
            pl.BlockSpec((tm, MERGE_COLS), lambda i: (i, C_ML // MERGE_COLS)),
            pl.BlockSpec((tm, GDN_GATE_COLS), lambda i: (i, 0)),
            pl.BlockSpec((tm, RET_V_COLS), lambda i: (i, 0)),
            const(1, GM_WIDTH),
            const(GM_GROUPS, GM_CHUNK, GM_CHUNK),
            const(GM_GROUPS, GM_CHUNK, GM_GROUP_CH),
            const(1, GDN_HEAD_DIM),
            const(1, RET_V_COLS),
            const(GM_WIDTH, D_MODEL),
            const(GDN_GATE_COLS, D_MODEL),
            const(RET_V_COLS, D_MODEL),
        ],
        out_specs=pl.BlockSpec((tm, D_MODEL), lambda i: (i, 0)),
        compiler_params=_cparams("parallel"),
        name="merge",
    )(p, p, p, p, o_gdn, o_ret, gm_norm_w.reshape(1, -1), ws, bs, gdn_norm_w.reshape(1, -1),
      ret_norm_w.reshape(1, -1), wgm, wgdn, wret)


def _moe_kernel(bexp_ref, bcnt_ref, stok_ref, sdst_ref,
                h_hbm, wg_ref, wu_ref, wd_ref, y_hbm,
                xbuf, ybuf, wgb, wub, wdb, gsem, ssem):
    i = pl.program_id(0)
    n = pl.num_programs(0)
    R = MOE_ROWS
    slot = lax.rem(i, 2)

    def gather_copy(blk, r, s):
        return pltpu.make_async_copy(h_hbm.at[:, stok_ref[blk * R + r], :], xbuf.at[s, :, r, :], gsem.at[s])

    def scatter_copy(blk, r, s):
        return pltpu.make_async_copy(ybuf.at[s, :, r, :], y_hbm.at[:, sdst_ref[blk * R + r], :], ssem.at[s])

    def for_rows(blk, fn):
        cnt = bcnt_ref[blk]
        full = lax.shift_right_logical(cnt, MOE_DMA_GROUP.bit_length() - 1)

        def group(g, c):
            for j in range(MOE_DMA_GROUP):
                fn(g * MOE_DMA_GROUP + j)
            return c
        lax.fori_loop(0, full, group, 0)
        lax.fori_loop(full * MOE_DMA_GROUP, cnt, lambda r, c: (fn(r), c)[1], 0)

    def start_gather(blk, s):
        for_rows(blk, lambda r: gather_copy(blk, r, s).start())

    def wait_gather(blk, s):
        for_rows(blk, lambda r: gather_copy(blk, r, s).wait())

    def start_scatter(blk, s):
        for_rows(blk, lambda r: scatter_copy(blk, r, s).start())

    def wait_scatter(blk, s):
        for_rows(blk, lambda r: scatter_copy(blk, r, s).wait())

    @pl.when(i == 0)
    def _():
        xbuf[...] = jnp.zeros_like(xbuf)
        start_gather(0, 0)

    @pl.when(i + 1 < n)
    def _():
        start_gather(i + 1, 1 - slot)

    changed = jnp.logical_or(i == 0, bexp_ref[i] != bexp_ref[jnp.maximum(i - 1, 0)])

    @pl.when(changed)
    def _():
        wgb[...] = wg_ref[0].astype(jnp.bfloat16)
        wub[...] = wu_ref[0].astype(jnp.bfloat16)
        wdb[...] = wd_ref[0].astype(jnp.bfloat16)

    wait_gather(i, slot)

    @pl.when(i >= 2)
    def _():
        wait_scatter(i - 2, slot)

    ncol = xbuf.shape[1]
    xb = jnp.concatenate([xbuf[slot, c] for c in range(ncol)], axis=-1).astype(jnp.bfloat16)
    hg = jnp.dot(xb, wgb[...], preferred_element_type=jnp.float32)
    hu = jnp.dot(xb, wub[...], preferred_element_type=jnp.float32)
    act = (hg * _sigmoid(hg)) * hu
    y = jnp.dot(act.astype(jnp.bfloat16), wdb[...], preferred_element_type=jnp.float32)
    for c in range(ncol):
        ybuf[slot, c] = y[:, c * 128:(c + 1) * 128]
    start_scatter(i, slot)

    @pl.when(i == n - 1)
    def _():
        @pl.when(n >= 2)
        def _():
            wait_scatter(i - 1, 1 - slot)
        wait_scatter(i, slot)


def _moe(h, block_expert, block_cnt, slot_tok, slot_dst, w_gate, w_up, w_down):
    S, T, _ = h.shape
    D = S * 128
    n_blocks = block_expert.shape[0]
    R = MOE_ROWS
    wmap = lambda i, be, bc, st, sd: (be[i], 0, 0)
    grid_spec = pltpu.PrefetchScalarGridSpec(
        num_scalar_prefetch=4,
        grid=(n_blocks,),
        in_specs=[
            pl.BlockSpec(memory_space=pl.ANY),
            pl.BlockSpec((1, D, MOE_HIDDEN), wmap),
            pl.BlockSpec((1, D, MOE_HIDDEN), wmap),
            pl.BlockSpec((1, MOE_HIDDEN, D), wmap),
        ],
        out_specs=pl.BlockSpec(memory_space=pl.ANY),
        scratch_shapes=[
            pltpu.VMEM((2, S, R, 128), jnp.float32),
            pltpu.VMEM((2, S, R, 128), jnp.float32),
            pltpu.VMEM((D, MOE_HIDDEN), jnp.bfloat16),
            pltpu.VMEM((D, MOE_HIDDEN), jnp.bfloat16),
            pltpu.VMEM((MOE_HIDDEN, D), jnp.bfloat16),
            pltpu.SemaphoreType.DMA((2,)),
            pltpu.SemaphoreType.DMA((2,)),
        ],
    )
    return pl.pallas_call(
        _moe_kernel,
        out_shape=jax.ShapeDtypeStruct((S, MOE_TOP_K * T, 128), jnp.float32),
        grid_spec=grid_spec,
        compiler_params=_cparams("arbitrary"),
        name="moe_experts",
    )(block_expert, block_cnt, slot_tok, slot_dst, h, w_gate, w_up, w_down)


def _combine_kernel(x_ref, y0_ref, y1_ref, w_ref, g_ref, *rest, with_norm):
    w = w_ref[...]
    z = x_ref[0] + g_ref[0, 0] * (w[:, 0:1] * _join_cols(y0_ref) + w[:, 1:2] * _join_cols(y1_ref))
    if not with_norm:
        rest[0][0] = z
        return
    nw_ref, sh_ref, sc_ref, z_ref, h_ref = rest
    z_ref[0] = z
    y = z * lax.rsqrt(jnp.mean(z * z, axis=-1, keepdims=True) + NORM_EPS)
    h_ref[0] = (y * nw_ref[...] * (1.0 + sc_ref[0, 0]) + sh_ref[0, 0]).astype(h_ref.dtype)


def _combine(z, y, weights, gate, norm=None):
    B, LT, D = z.shape
    S = D // 128
    nt = LT // NORM_ROWS
    seg = lambda b, t: (b, jnp.minimum(t, 1), 0, 0)
    tile = pl.BlockSpec((1, NORM_ROWS, D), lambda b, t: (b, t, 0))
    in_specs = [
        tile,
        pl.BlockSpec((S, NORM_ROWS, 128), lambda b, t: (0, b * nt + t, 0)),
        pl.BlockSpec((S, NORM_ROWS, 128), lambda b, t: (0, B * nt + b * nt + t, 0)),
        pl.BlockSpec((NORM_ROWS, MOE_TOP_K), lambda b, t: (b * nt + t, 0)),
        pl.BlockSpec((1, 1, 1, D), seg),
    ]
    args = [z, y, y, weights, gate]
    out_shape = [jax.ShapeDtypeStruct((B, LT, D), jnp.float32)]
    out_specs = [tile]
    if norm is not None:
        norm_w, shift, scale = norm
        in_specs += [pl.BlockSpec((1, D), lambda b, t: (0, 0)), pl.BlockSpec((1, 1, 1, D), seg),
                     pl.BlockSpec((1, 1, 1, D), seg)]
        args += [norm_w.reshape(1, D), shift, scale]
        out_shape.append(jax.ShapeDtypeStruct((B, LT, D), jnp.bfloat16))
        out_specs.append(tile)
    out = pl.pallas_call(
        functools.partial(_combine_kernel, with_norm=norm is not None),
        out_shape=out_shape,
        grid=(B, nt),
        in_specs=in_specs,
        out_specs=out_specs,
        compiler_params=_cparams("parallel", "parallel"),
        name="moe_combine",
    )(*args)
    return (out[0], out[1]) if norm is not None else (out[0], None)


def _dispatch_tables(expert_idx):
    T, K = expert_idx.shape
    E, R = MOE_EXPERTS, MOE_ROWS
    A = T * K
    n_blocks = A // R + E
    flat_e = expert_idx.reshape(A).astype(jnp.int32)
    order = jnp.argsort(flat_e).astype(jnp.int32)
    cnt_end = jnp.sum(flat_e[None, :] <= jnp.arange(E, dtype=jnp.int32)[:, None], axis=1, dtype=jnp.int32)
    cnt_start = jnp.concatenate([jnp.zeros((1,), jnp.int32), cnt_end[:-1]])
    counts = cnt_end - cnt_start
    nblk = (counts + R - 1) // R
    blk_end = jnp.cumsum(nblk)
    blk_start = blk_end - nblk
    blk = jnp.arange(n_blocks, dtype=jnp.int32)
    block_expert = jnp.minimum(jnp.sum(blk_end[None, :] <= blk[:, None], axis=1, dtype=jnp.int32), E - 1)
    first = (blk - blk_start[block_expert]) * R
    block_cnt = jnp.clip(counts[block_expert] - first, 0, R).astype(jnp.int32)
    row = jnp.arange(R, dtype=jnp.int32)[None, :]
    src = (cnt_start[block_expert] + first)[:, None] + row
    a = order[jnp.clip(src, 0, A - 1)]
    valid = row < block_cnt[:, None]
    slot_tok = jnp.where(valid, a // K, 0).reshape(-1)
    slot_dst = jnp.where(valid, (a % K) * T + a // K, 0).reshape(-1)
    return block_expert, block_cnt, slot_tok, slot_dst


def _bdot(a, b):
    return jnp.dot(a.astype(jnp.bfloat16), b.astype(jnp.bfloat16), preferred_element_type=jnp.float32)


def _bdot_nt(a, b):
    return lax.dot_general(a.astype(jnp.bfloat16), b.astype(jnp.bfloat16), (((1,), (1,)), ((), ())),
                           preferred_element_type=jnp.float32)


def _bdot_tn(a, b):
    return lax.dot_general(a.astype(jnp.bfloat16), b.astype(jnp.bfloat16), (((0,), (0,)), ((), ())),
                           preferred_element_type=jnp.float32)


def _conv_silu(x_ref, col, cw, xp_s):
    LT = x_ref.shape[1]
    K = cw.shape[0]
    r = K // 2
    starts = (CONV_PAD, 2 * CONV_PAD + CTX_LEN)
    xp_s[pl.ds(starts[0], CTX_LEN), :] = x_ref[0, pl.ds(0, CTX_LEN), col:col + 128]
    xp_s[pl.ds(starts[1], LT - CTX_LEN), :] = x_ref[0, pl.ds(CTX_LEN, LT - CTX_LEN), col:col + 128]
    outs = []
    for start, n in zip(starts, (CTX_LEN, LT - CTX_LEN)):
        acc = xp_s[pl.ds(start - r, n), :] * cw[0:1, :]
        for j in range(1, K):
            acc = acc + xp_s[pl.ds(start - r + j, n), :] * cw[j:j + 1, :]
        outs.append(acc * _sigmoid(acc))
    return jnp.concatenate(outs, axis=0)


def _l2n(t):
    return t * lax.rsqrt(jnp.sum(t * t, axis=-1, keepdims=True) + NORM_EPS)


def _lane_col(x, idx):
    lane = lax.broadcasted_iota(jnp.int32, x.shape, 1)
    return jnp.sum(jnp.where(lane == idx, x, 0.0), axis=1, keepdims=True)


def _inv_unit_tri(Ns, eye, m16, moffs):
    Dg = [N * m16 for N in Ns]
    D2 = [_bdot(d, d) for d in Dg]
    D4 = [_bdot(d, d) for d in D2]
    D8 = [_bdot(d, d) for d in D4]
    T = [eye - d for d in Dg]
    for P in (D2, D4, D8):
        T = [t + _bdot(t, p) for t, p in zip(T, P)]
    for m in moffs:
        X = [_bdot(t, N * m) for t, N in zip(T, Ns)]
        T = [t - _bdot(x, t) for t, x in zip(T, X)]
    return T


def _gdn_kernel(q_ref, k_ref, v_ref, ab_ref, cwq_ref, cwk_ref, cwv_ref, alog_ref, dtb_ref, o_ref,
                qn_s, kn_s, vn_s, gb_s, gall_s, xp_s, u_s, w_s, qs_s, kst_s, qkd_s, el_s, o_s):
    f32 = jnp.float32
    bf16 = jnp.bfloat16
    HP = GDN_HEADS_PER_STEP
    Dh = GDN_HEAD_DIM
    hp = pl.program_id(1)
    LT = q_ref.shape[1]
    C = GDN_ROWS
    nch = LT // C
    nctx = CTX_LEN // C

    lane = lax.broadcasted_iota(jnp.int32, (LT, 128), 1)

    @pl.when(hp == 0)
    def _():
        ab = ab_ref[0]
        xa = ab + dtb_ref[...]
        g_all = -jnp.exp(alog_ref[...]) * (jnp.maximum(xa, 0.0) + jnp.log1p(jnp.exp(-jnp.abs(xa))))
        gall_s[...] = jnp.where(lane < 2 * GDN_HEADS, g_all, _sigmoid(ab))

    for start in (0, CONV_PAD + CTX_LEN, 2 * CONV_PAD + LT):
        xp_s[pl.ds(start, CONV_PAD), :] = jnp.zeros((CONV_PAD, 128), f32)

    gb_all = gall_s[...]
    for hh in range(HP):
        col = hh * Dh
        qn_s[hh] = _l2n(_conv_silu(q_ref, col, cwq_ref[:, col:col + Dh], xp_s)) * (Dh ** -0.5)
        kn_s[hh] = _l2n(_conv_silu(k_ref, col, cwk_ref[:, col:col + Dh], xp_s))
        vn_s[hh] = _conv_silu(v_ref, col, cwv_ref[:, col:col + Dh], xp_s)
        head = hp * HP + hh
        gb = jnp.where(lane == 0, _lane_col(gb_all, head), 0.0)
        for j in range(1, 4):
            gb = jnp.where(lane == j, _lane_col(gb_all, j * GDN_HEADS + head), gb)
        gb_s[hh] = gb

    ri = lax.broadcasted_iota(jnp.int32, (C, C), 0)
    ci = lax.broadcasted_iota(jnp.int32, (C, C), 1)
    eye = (ri == ci).astype(f32)
    low = (ri >= ci).astype(f32)
    upp = (ri <= ci).astype(f32)
    m16 = ((ri // 16) == (ci // 16)).astype(f32)
    moffs = [(((ri // (2 * s)) == (ci // (2 * s))) & ((ri // s) != (ci // s))).astype(f32) for s in (16, 32, 64)]
    hi = lax.Precision.HIGHEST

    def make_chunk_group(hh):
        def chunk_group(i, carry):
            probs = []
            for j in range(GDN_CHUNKS_PER_STEP):
                c = i * GDN_CHUNKS_PER_STEP + j
                rows = pl.ds(pl.multiple_of(c * C, C), C)
                q = qn_s[hh, rows, :]
                k = kn_s[hh, rows, :]
                v = vn_s[hh, rows, :]
                gbc = gb_s[hh, rows, :]
                kk = _bdot_nt(k, k)
                qk = _bdot_nt(q, k)
                cs_f = jnp.dot(low, gbc, precision=hi, preferred_element_type=f32)
                cs_b = jnp.dot(upp, gbc, precision=hi, preferred_element_type=f32)
                for d in range(2):
                    cs = cs_f if d == 0 else cs_b
                    gc = cs[:, d:d + 1]
                    gc_row = cs.T[d:d + 1, :]
                    beta = gbc[:, 2 + d:3 + d]
                    incl = low if d == 0 else upp
                    decay = jnp.exp(jnp.where(incl > 0, gc - gc_row, -jnp.inf))
                    g_last = gc[C - 1:C, :] if d == 0 else gc[0:1, :]
                    probs.append(dict(d=d, c=c, q=q, k=k, v=v, qk=qk, gc=gc, beta=beta, decay=decay,
                                      g_last=g_last, N=(beta * kk) * decay * (1.0 - eye)))
            Ts = _inv_unit_tri([p["N"] for p in probs], eye, m16, moffs)
            for p, T in zip(probs, Ts):
                d, c, gc, beta = p["d"], p["c"], p["gc"], p["beta"]
                eg = jnp.exp(gc)
                u_s[hh, d, c] = _bdot(T, p["v"] * beta)
                w_s[hh, d, c] = _bdot(T, p["k"] * beta * eg).astype(bf16)
                qs_s[hh, d, c] = (p["q"] * eg).astype(bf16)
                kst_s[hh, d, c] = (p["k"] * jnp.exp(p["g_last"] - gc)).T.astype(bf16)
                qkd_s[hh, d, c] = (p["qk"] * p["decay"]).astype(bf16)
                el_s[hh, d, c] = jnp.broadcast_to(jnp.exp(p["g_last"]), (8, 128))
            return carry
        return chunk_group

    for hh in range(HP):
        lax.fori_loop(0, nch // GDN_CHUNKS_PER_STEP, make_chunk_group(hh), 0)

    def step(t, S):
        cb = jnp.where(t < nctx, nctx - 1 - t, nch - 1 + nctx - t)
        idx = [(hh, d, c) for hh in range(HP) for d, c in ((0, t), (1, cb))]
        mm = lambda a, b: jnp.dot(a, b, preferred_element_type=f32)
        Sbf = [Sd.astype(bf16) for Sd in S]
        wS = [mm(w_s[i], sb) for i, sb in zip(idx, Sbf)]
        vb = [(u_s[i] - x).astype(bf16) for i, x in zip(idx, wS)]
        upd = [mm(kst_s[i], v) for i, v in zip(idx, vb)]
        o_in = [mm(qkd_s[i], v) for i, v in zip(idx, vb)]
        o_st = [mm(qs_s[i], sb) for i, sb in zip(idx, Sbf)]
        for i, a, b in zip(idx, o_st, o_in):
            o_s[i] = a + b
        return tuple(Sd * el_s[i][0:1, :] + x for i, Sd, x in zip(idx, S, upd))

    z = jnp.zeros((Dh, Dh), f32)
    lax.fori_loop(0, nch, step, (z,) * (2 * HP))
    for hh in range(HP):
        o_ref[0, :, hh * Dh:(hh + 1) * Dh] = (o_s[hh, 0] + o_s[hh, 1]).reshape(LT, Dh)


def _gdn_mixer(p3, ab, conv_w, a_log, dt_bias):
    B, LT, _ = p3.shape
    H, Dh, C, HP = GDN_HEADS, GDN_HEAD_DIM, GDN_ROWS, GDN_HEADS_PER_STEP
    W = HP * Dh
    nch = LT // C
    qb = C_QKV // W
    alog_row = jnp.zeros((1, 128), jnp.float32).at[0, :2 * H].set(a_log.reshape(-1))
    dtb_row = jnp.zeros((1, 128), jnp.float32).at[0, :2 * H].set(dt_bias.reshape(-1))
    taps = conv_w.shape[0]
    nstep = H // HP
    per_head = lambda dt: pltpu.VMEM((HP, LT, Dh), dt)
    per_chunk = lambda dt: pltpu.VMEM((HP, 2, nch, C, Dh), dt)
    once = pl.Buffered(1)
    return pl.pallas_call(
        _gdn_kernel,
        out_shape=jax.ShapeDtypeStruct((B, LT, H * Dh), jnp.float32),
        grid=(B, nstep),
        in_specs=[
            pl.BlockSpec((1, LT, W), lambda b, h: (b, 0, qb + h), pipeline_mode=once),
            pl.BlockSpec((1, LT, W), lambda b, h: (b, 0, qb + nstep + h), pipeline_mode=once),
            pl.BlockSpec((1, LT, W), lambda b, h: (b, 0, qb + 2 * nstep + h), pipeline_mode=once),
            pl.BlockSpec((1, LT, 128), lambda b, h: (b, 0, 0)),
            pl.BlockSpec((taps, W), lambda b, h: (0, h)),
            pl.BlockSpec((taps, W), lambda b, h: (0, nstep + h)),
            pl.BlockSpec((taps, W), lambda b, h: (0, 2 * nstep + h)),
            pl.BlockSpec((1, 128), lambda b, h: (0, 0)),
            pl.BlockSpec((1, 128), lambda b, h: (0, 0)),
        ],
        out_specs=pl.BlockSpec((1, LT, W), lambda b, h: (b, 0, h)),
        scratch_shapes=[
            per_head(jnp.float32), per_head(jnp.float32), per_head(jnp.float32), per_head(jnp.float32),
            pltpu.VMEM((LT, 128), jnp.float32),
            pltpu.VMEM((LT + 3 * CONV_PAD, 128), jnp.float32),
            per_chunk(jnp.float32), per_chunk(jnp.bfloat16), per_chunk(jnp.bfloat16), per_chunk(jnp.bfloat16),
            per_chunk(jnp.bfloat16),
            pltpu.VMEM((HP, 2, nch, 8, 128), jnp.float32),
            per_chunk(jnp.float32),
        ],
        compiler_params=_cparams("parallel", "arbitrary"),
        name="gdn_mixer",
    )(p3, p3, p3, ab, conv_w, conv_w, conv_w, alog_row, dtb_row)


def _ret_kernel(lg_ref, q_ref, k_ref, v_ref, cos_ref, sin_ref, o_ref, qr_s, kr_s, st_s):
    f32 = jnp.float32
    h = pl.program_id(1)
    LT = q_ref.shape[1]
    C = RET_ROWS
    nch = LT // C
    nctx = CTX_LEN // C
    cos = cos_ref[...]
    sin = sin_ref[...]
    half = RET_KEY_DIM // 2
    q = q_ref[0]
    k = k_ref[0]
    qr_s[...] = (q * cos + pltpu.roll(q, half, axis=1) * sin) * (RET_KEY_DIM ** -0.5)
    kr_s[...] = k * cos + pltpu.roll(k, half, axis=1) * sin

    lg_f = lg_ref[0, h]
    lg_b = lg_ref[1, h]
    ri = lax.broadcasted_iota(jnp.int32, (C, C), 0)
    ci = lax.broadcasted_iota(jnp.int32, (C, C), 1)
    diff = (ri - ci).astype(f32)
    dm = (jnp.exp(jnp.where(diff >= 0, lg_f * diff, -jnp.inf))
          + jnp.exp(jnp.where(diff <= 0, -lg_b * diff, -jnp.inf)))
    pcol = lax.broadcasted_iota(jnp.int32, (C, 1), 0).astype(f32)
    cross_f = jnp.exp(lg_f * (pcol + 1.0))
    in_f = jnp.exp(lg_f * (C - 1.0 - pcol))
    cross_b = jnp.exp(lg_b * (C - pcol))
    in_b = jnp.exp(lg_b * pcol)
    cd_f = jnp.exp(lg_f * C)
    cd_b = jnp.exp(lg_b * C)

    G = RET_CHUNKS_PER_STEP
    chunk_rows = lambda i, j: pl.ds(pl.multiple_of((i * G + j) * C, C), C)

    def local(i, carry):
        rs = [chunk_rows(i, j) for j in range(G)]
        qs = [qr_s[r, :] for r in rs]
        ks = [kr_s[r, :] for r in rs]
        vs = [v_ref[0, r, :] for r in rs]
        ss = [_bdot_nt(q, k) for q, k in zip(qs, ks)]
        os_ = [_bdot(s * dm, v) for s, v in zip(ss, vs)]
        mf = [_bdot_tn(k * in_f, v) for k, v in zip(ks, vs)]
        mb = [_bdot_tn(k * in_b, v) for k, v in zip(ks, vs)]
        for j in range(G):
            o_ref[0, rs[j], :] = os_[j]
            st_s[0, i * G + j] = mf[j]
            st_s[1, i * G + j] = mb[j]
        return carry

    lax.fori_loop(0, nch // G, local, 0)

    fwd_order = list(range(nch))
    bwd_order = list(range(nctx - 1, -1, -1)) + list(range(nch - 1, nctx - 1, -1))
    for d, order, cd in ((0, fwd_order, cd_f), (1, bwd_order, cd_b)):
        S = jnp.zeros((RET_KEY_DIM, RET_VAL_DIM), f32)
        for c in order:
            contrib = st_s[d, c]
            st_s[d, c] = S
            S = S * cd + contrib

    def cross(i, carry):
        rs = [chunk_rows(i, j) for j in range(G)]
        qs = [qr_s[r, :] for r in rs]
        cf = [_bdot(q, st_s[0, i * G + j]) for j, q in enumerate(qs)]
        cb = [_bdot(q, st_s[1, i * G + j]) for j, q in enumerate(qs)]
        for j in range(G):
            o_ref[0, rs[j], :] += cf[j] * cross_f + cb[j] * cross_b
        return carry

    lax.fori_loop(0, nch // G, cross, 0)


def _ret_mixer(p3, log_gamma, cos_t, sin_t):
    B, LT, _ = p3.shape
    H, Dk, Dv = RET_HEADS, RET_KEY_DIM, RET_VAL_DIM
    grid_spec = pltpu.PrefetchScalarGridSpec(
        num_scalar_prefetch=1,
        grid=(B, H),
        in_specs=[
            pl.BlockSpec((1, LT, Dk), lambda b, h, lg: (b, 0, C_RQ // Dk + h)),
            pl.BlockSpec((1, LT, Dk), lambda b, h, lg: (b, 0, C_RK // Dk + h)),
            pl.BlockSpec((1, LT, Dv), lambda b, h, lg: (b, 0, C_RV // Dv + h)),
            pl.BlockSpec((LT, Dk), lambda b, h, lg: (0, 0)),
            pl.BlockSpec((LT, Dk), lambda b, h, lg: (0, 0)),
        ],
        out_specs=pl.BlockSpec((1, LT, Dv), lambda b, h, lg: (b, 0, h)),
        scratch_shapes=[pltpu.VMEM((LT, Dk), jnp.float32), pltpu.VMEM((LT, Dk), jnp.float32),
                        pltpu.VMEM((2, LT // RET_ROWS, Dk, Dv), jnp.float32)],
    )
    return pl.pallas_call(
        _ret_kernel,
        out_shape=jax.ShapeDtypeStruct((B, LT, H * Dv), jnp.float32),
        grid_spec=grid_spec,
        compiler_params=_cparams("parallel", "parallel"),
        name="ret_mixer",
    )(log_gamma, p3, p3, p3, cos_t, sin_t)


def _axial_rope(rows, dim):
    n = dim // 4
    freq = ROPE_BASE ** (-jnp.arange(n, dtype=jnp.float32) / n)
    row = jnp.repeat(jnp.arange(rows, dtype=jnp.float32), GRID_W)
    col = (jnp.arange(rows * GRID_W) % GRID_W).astype(jnp.float32)
    ang = jnp.concatenate([row[:, None] * freq, col[:, None] * freq], axis=-1)
    ang = jnp.concatenate([ang, ang], axis=-1)[:, None, :]
    return jnp.cos(ang), jnp.sin(ang)


def _rope_tables(L):
    cos, sin = _axial_rope(L // GRID_W, RET_KEY_DIM)
    half = RET_KEY_DIM // 2
    sign = jnp.concatenate([-jnp.ones((half,), jnp.float32), jnp.ones((half,), jnp.float32)])
    cos_t = jnp.concatenate([jnp.ones((CTX_LEN, RET_KEY_DIM), jnp.float32), cos[:, 0, :]], axis=0)
    sin_t = jnp.concatenate([jnp.zeros((CTX_LEN, RET_KEY_DIM), jnp.float32), sin[:, 0, :] * sign], axis=0)
    return cos_t, sin_t


def _seg_table(ctx_vec, lat_vec):
    B, D = lat_vec.shape
    return jnp.stack([jnp.broadcast_to(ctx_vec, (B, D)), lat_vec], axis=1).reshape(B, 2, 1, D)


def kernel(x, c, ctx, c_ctx, mod_w, mod_b, norm1_w, w_in, gm_norm_w, gm_spatial_w, gm_spatial_b, gdn_conv_w, gdn_a_log, gdn_dt_bias, gdn_norm_w, ret_decay_logit, ret_norm_w, w_br_gm, w_br_gdn, w_br_ret, w_out, norm2_w, router_group_w, router_group_b, router_expert_w, router_expert_b, moe_w_gate, moe_w_up, moe_w_down, final_norm_w):
    B, L, D = x.shape
    depth = mod_w.shape[0]
    LT = CTX_LEN + L
    T = B * LT
    bf16 = jnp.bfloat16
    cos_t, sin_t = _rope_tables(L)

    w_in_t = jnp.swapaxes(w_in, 1, 2)
    wgm_b, wgdn_b, wret_b, wout_b = (t.astype(bf16) for t in (w_br_gm, w_br_gdn, w_br_ret, w_out))
    ws_b = gm_spatial_w.astype(bf16)
    bs_full = jnp.broadcast_to(gm_spatial_b[..., None], gm_spatial_b.shape + (GM_GROUP_CH,))
    pad = 128 - MOE_GROUPS - MOE_EXPERTS
    w_router = jnp.concatenate([router_group_w, router_expert_w, jnp.zeros((depth, D, pad), jnp.float32)], axis=-1)
    b_router = jnp.concatenate([router_group_b, router_expert_b, jnp.zeros((depth, pad), jnp.float32)], axis=-1)
    moe_wg, moe_wu = (t.reshape(depth * MOE_EXPERTS, D, MOE_HIDDEN) for t in (moe_w_gate, moe_w_up))
    moe_wd = moe_w_down.reshape(depth * MOE_EXPERTS, MOE_HIDDEN, D)
    main_rows = lambda j: j * PROJ_COLS + jnp.where(j * PROJ_COLS < AB_START, 0, AB_END - AB_START)

    cc = jnp.concatenate([jax.nn.silu(c), jax.nn.silu(c_ctx)[None], jnp.zeros((16 - B - 1, D), c.dtype)], axis=0)
    cc = cc.astype(bf16)
    mods = []
    for l in range(depth):
        mod = _matmul(cc, mod_w, l, 16, 1024, "adaln_mod") + mod_b[l]
        mods.append([_seg_table(mod[B, i * D:(i + 1) * D], mod[:B, i * D:(i + 1) * D]) for i in range(6)])

    z = jnp.concatenate([ctx, x], axis=1)
    h = _normmod(z, norm1_w[0], mods[0][0], mods[0][1], bf16)
    for l in range(depth):
        sh1, sc1, gt1, sh2, sc2, gt2 = mods[l]
        log_gamma = jax.nn.log_sigmoid(ret_decay_logit[l])

        h = h.reshape(T, D)
        p = _in_proj(h, w_in_t, l, LT, PROJ_COLS, P_COLS, main_rows, "in_proj")
        pab = _in_proj(h, w_in_t, l, LT, 128, 128, lambda j: j * 0 + AB_START, "in_proj_ab")
        p3 = p.reshape(B, LT, P_COLS)
        o_gdn = _gdn_mixer(p3, pab.reshape(B, LT, 128), gdn_conv_w[l], gdn_a_log[l], gdn_dt_bias[l])
        o_ret = _ret_mixer(p3, log_gamma, cos_t, sin_t)
        o_gdn = o_gdn.reshape(T, GDN_GATE_COLS)
        o_ret = o_ret.reshape(T, RET_V_COLS)

        merged = _merge(p, o_gdn, o_ret, gm_norm_w[l], ws_b[l], bs_full[l], gdn_norm_w[l], ret_norm_w[l],
                        wgm_b[l], wgdn_b[l], wret_b[l])
        z, h2, route = _out_proj_route(merged, wout_b[l], z, gt1, norm2_w[l], sh2, sc2, w_router[l], b_router[l])
        expert_idx = route[:, :MOE_TOP_K].astype(jnp.int32)
        weights = route[:, MOE_TOP_K:2 * MOE_TOP_K]
        block_expert, block_cnt, slot_tok, slot_dst = _dispatch_tables(expert_idx)
        y = _moe(h2, block_expert + l * MOE_EXPERTS, block_cnt, slot_tok, slot_dst, moe_wg, moe_wu, moe_wd)
        nxt = (norm1_w[l + 1], mods[l + 1][0], mods[l + 1][1]) if l + 1 < depth else None
        z, h = _combine(z, y, weights, gt2, nxt)
    return _final_norm(z, final_norm_w)
```

```python
import functools
import math

import jax
import jax.numpy as jnp
from jax import lax
from jax.experimental import pallas as pl
from jax.experimental.pallas import tpu as pltpu

D_MODEL = 2048
CTX_LEN = 256
GRID_W = 64
NORM_EPS = 1e-6

GM_CHUNK = 128
GM_GROUPS = 4
GM_GROUP_CH = 128
GM_WIDTH = GM_GROUPS * GM_GROUP_CH
GDN_HEADS = 4
GDN_HEAD_DIM = 128
GDN_ROWS = 128
GDN_CHUNKS_PER_STEP = 6
GDN_HEADS_PER_STEP = 2
CONV_PAD = 8
RET_HEADS = 4
RET_KEY_DIM = 128
RET_VAL_DIM = 256
RET_ROWS = 256
RET_CHUNKS_PER_STEP = 3
ROPE_BASE = 10000.0
MOE_GROUPS = 4
MOE_EXPERTS_PER_GROUP = 8
MOE_EXPERTS = MOE_GROUPS * MOE_EXPERTS_PER_GROUP
MOE_TOP_K = 2
MOE_HIDDEN = 512

GM_UV_COLS = 2 * GM_WIDTH
GDN_QKV_COLS = 3 * GDN_HEADS * GDN_HEAD_DIM
GDN_AB_COLS = 2 * GDN_HEADS
GDN_GATE_COLS = GDN_HEADS * GDN_HEAD_DIM
RET_QK_COLS = RET_HEADS * RET_KEY_DIM
RET_V_COLS = RET_HEADS * RET_VAL_DIM
MERGE_COLS = 3 * D_MODEL
AB_START = GM_UV_COLS + GDN_QKV_COLS
AB_END = AB_START + 2 * GDN_AB_COLS

C_UV = 0
C_QKV = C_UV + GM_UV_COLS
C_RQ = C_QKV + GDN_QKV_COLS
C_RK = C_RQ + RET_QK_COLS
C_RV = C_RK + RET_QK_COLS
C_GG = C_RV + RET_V_COLS
C_RG = C_GG + GDN_GATE_COLS
C_ML = C_RG + RET_V_COLS
P_COLS = C_ML + MERGE_COLS

V7X_VMEM_LIMIT_BYTES = 56 * 1024 * 1024
MOE_ROWS = 256
MOE_DMA_GROUP = 8
MERGE_ROWS = 256
NORM_ROWS = 256
PROJ_COLS = 512


def _cparams(*sem):
    return pltpu.CompilerParams(dimension_semantics=sem, vmem_limit_bytes=V7X_VMEM_LIMIT_BYTES)


def _normmod_kernel(z_ref, nw_ref, sh_ref, sc_ref, o_ref):
    z = z_ref[0]
    y = z * lax.rsqrt(jnp.mean(z * z, axis=-1, keepdims=True) + NORM_EPS)
    y = y * nw_ref[...]
    o_ref[0] = (y * (1.0 + sc_ref[0, 0]) + sh_ref[0, 0]).astype(o_ref.dtype)


def _normmod(z, norm_w, shift, scale, out_dtype):
    B, LT, D = z.shape
    seg = lambda b, t: (b, jnp.minimum(t, 1), 0, 0)
    assert CTX_LEN == NORM_ROWS
    return pl.pallas_call(
        _normmod_kernel,
        out_shape=jax.ShapeDtypeStruct((B, LT, D), out_dtype),
        grid=(B, LT // NORM_ROWS),
        in_specs=[
            pl.BlockSpec((1, NORM_ROWS, D), lambda b, t: (b, t, 0)),
            pl.BlockSpec((1, D), lambda b, t: (0, 0)),
            pl.BlockSpec((1, 1, 1, D), seg),
            pl.BlockSpec((1, 1, 1, D), seg),
        ],
        out_specs=pl.BlockSpec((1, NORM_ROWS, D), lambda b, t: (b, t, 0)),
        compiler_params=_cparams("parallel", "parallel"),
        name="normmod",
    )(z, norm_w.reshape(1, D), shift, scale)


def _split_cols(ref, x):
    for c in range(x.shape[1] // 128):
        ref[c] = x[:, c * 128:(c + 1) * 128]


def _join_cols(ref):
    return jnp.concatenate([ref[c] for c in range(ref.shape[0])], axis=-1)


def _route_rows(logits):
    neg = jnp.float32(-3.0e38)
    lane = lax.broadcasted_iota(jnp.int32, logits.shape, 1)
    lane_f = lane.astype(jnp.float32)
    first_max = lambda v: jnp.min(jnp.where(v == jnp.max(v, axis=1, keepdims=True), lane_f, 128.0), axis=1,
                                  keepdims=True)
    is_g = lane < MOE_GROUPS
    gl = jnp.where(is_g, logits, neg)
    g_max = jnp.max(gl, axis=1, keepdims=True)
    grp = first_max(gl).astype(jnp.int32)
    p_top = 1.0 / jnp.sum(jnp.where(is_g, jnp.exp(logits - g_max), 0.0), axis=1, keepdims=True)
    e_lane = lane - MOE_GROUPS
    in_grp = (e_lane >= 0) & (e_lane < MOE_EXPERTS) & ((e_lane // MOE_EXPERTS_PER_GROUP) == grp)
    el = jnp.where(in_grp, logits, neg)
    m1 = jnp.max(el, axis=1, keepdims=True)
    i1 = first_max(el)
    el2 = jnp.where(lane_f == i1, neg, el)
    m2 = jnp.max(el2, axis=1, keepdims=True)
    i2 = first_max(el2)
    r = jnp.exp(m2 - m1)
    w1 = p_top / (1.0 + r)
    out = jnp.where(lane == 0, i1 - MOE_GROUPS, 0.0)
    out = jnp.where(lane == 1, i2 - MOE_GROUPS, out)
    out = jnp.where(lane == 2, w1, out)
    return jnp.where(lane == 3, w1 * r, out)


def _out_route_kernel(a_ref, w_ref, x_ref, g_ref, nw_ref, sh_ref, sc_ref, whi_ref, wlo_ref, rb_ref,
                      z_ref, h_ref, rt_ref):
    z = x_ref[0] + g_ref[0, 0] * jnp.dot(a_ref[...], w_ref[...], preferred_element_type=jnp.float32)
    z_ref[0] = z
    y = z * lax.rsqrt(jnp.mean(z * z, axis=-1, keepdims=True) + NORM_EPS)
    h = y * nw_ref[...] * (1.0 + sc_ref[0, 0]) + sh_ref[0, 0]
    _split_cols(h_ref, h)
    hi = h.astype(jnp.bfloat16)
    lo = (h - hi.astype(jnp.float32)).astype(jnp.bfloat16)
    whi = whi_ref[...]
    logits = (jnp.dot(hi, whi, preferred_element_type=jnp.float32)
              + jnp.dot(lo, whi, preferred_element_type=jnp.float32)
              + jnp.dot(hi, wlo_ref[...], preferred_element_type=jnp.float32))
    rt_ref[...] = _route_rows(logits + rb_ref[...])


def _out_proj_route(merged, w_out, z, gate, norm_w, shift, scale, w_router, b_router):
    B, LT, D = z.shape
    nt = LT // NORM_ROWS
    seg = lambda b, t: (b, jnp.minimum(t, 1), 0, 0)
    tile = pl.BlockSpec((1, NORM_ROWS, D), lambda b, t: (b, t, 0))
    const = lambda r, c: pl.BlockSpec((r, c), lambda b, t: (0, 0))
    whi = w_router.astype(jnp.bfloat16)
    wlo = (w_router - whi.astype(jnp.float32)).astype(jnp.bfloat16)
    return pl.pallas_call(
        _out_route_kernel,
        out_shape=[jax.ShapeDtypeStruct((B, LT, D), jnp.float32),
                   jax.ShapeDtypeStruct((D // 128, B * LT, 128), jnp.float32),
                   jax.ShapeDtypeStruct((B * LT, 128), jnp.float32)],
        grid=(B, nt),
        in_specs=[
            pl.BlockSpec((NORM_ROWS, D), lambda b, t: (b * nt + t, 0)),
            const(D, D),
            tile,
            pl.BlockSpec((1, 1, 1, D), seg),
            const(1, D),
            pl.BlockSpec((1, 1, 1, D), seg),
            pl.BlockSpec((1, 1, 1, D), seg),
            const(D, 128),
            const(D, 128),
            const(1, 128),
        ],
        out_specs=[tile,
                   pl.BlockSpec((D // 128, NORM_ROWS, 128), lambda b, t: (0, b * nt + t, 0)),
                   pl.BlockSpec((NORM_ROWS, 128), lambda b, t: (b * nt + t, 0))],
        compiler_params=_cparams("parallel", "parallel"),
        name="out_proj_route",
    )(merged, w_out, z, gate, norm_w.reshape(1, D), shift, scale, whi, wlo, b_router.reshape(1, 128))


def _final_norm_kernel(z_ref, nw_ref, o_ref):
    z = z_ref[0]
    o_ref[0] = z * lax.rsqrt(jnp.mean(z * z, axis=-1, keepdims=True) + NORM_EPS) * nw_ref[...]


def _final_norm(z, norm_w):
    B, LT, D = z.shape
    skip = CTX_LEN // NORM_ROWS
    return pl.pallas_call(
        _final_norm_kernel,
        out_shape=jax.ShapeDtypeStruct((B, LT - CTX_LEN, D), jnp.float32),
        grid=(B, (LT - CTX_LEN) // NORM_ROWS),
        in_specs=[
            pl.BlockSpec((1, NORM_ROWS, D), lambda b, t: (b, t + skip, 0)),
            pl.BlockSpec((1, D), lambda b, t: (0, 0)),
        ],
        out_specs=pl.BlockSpec((1, NORM_ROWS, D), lambda b, t: (b, t, 0)),
        compiler_params=_cparams("parallel", "parallel"),
        name="final_norm",
    )(z, norm_w.reshape(1, D))


def _mm_kernel(a_ref, w_ref, o_ref):
    o_ref[...] = jnp.dot(a_ref[...], w_ref[...].astype(jnp.bfloat16), preferred_element_type=jnp.float32)


def _matmul(a, w, layer, tm, tn, name):
    M, K = a.shape
    N = w.shape[2]
    return pl.pallas_call(
        _mm_kernel,
        out_shape=jax.ShapeDtypeStruct((M, N), jnp.float32),
        grid=(M // tm, N // tn),
        in_specs=[
            pl.BlockSpec((tm, K), lambda i, j: (i, 0)),
            pl.BlockSpec((None, K, tn), lambda i, j: (layer, 0, j)),
        ],
        out_specs=pl.BlockSpec((tm, tn), lambda i, j: (i, j)),
        compiler_params=_cparams("parallel", "arbitrary"),
        name=name,
    )(a, w)


def _mm_nt_kernel(a_ref, w_ref, o_ref):
    w = w_ref[0].astype(jnp.bfloat16)
    o_ref[...] = lax.dot_general(a_ref[...], w, (((1,), (1,)), ((), ())), preferred_element_type=jnp.float32)


def _in_proj(a, w_t, layer, tm, tn, n_out, first_row, name):
    M, K = a.shape
    wmap = lambda i, j: (layer, pl.multiple_of(first_row(j), 8), 0)
    return pl.pallas_call(
        _mm_nt_kernel,
        out_shape=jax.ShapeDtypeStruct((M, n_out), jnp.float32),
        grid=(M // tm, n_out // tn),
        in_specs=[
            pl.BlockSpec((tm, K), lambda i, j: (i, 0)),
            pl.BlockSpec((pl.Element(1), pl.Element(tn), pl.Element(K)), wmap),
        ],
        out_specs=pl.BlockSpec((tm, tn), lambda i, j: (i, j)),
        compiler_params=_cparams("parallel", "arbitrary"),
        name=name,
    )(a, w_t)


def _gelu_exact(x):
    return 0.5 * x * (1.0 + lax.erf(x * (1.0 / math.sqrt(2.0))))


def _sigmoid(x):
    return 1.0 / (1.0 + jnp.exp2(x * (-1.0 / math.log(2.0))))


def _merge_kernel(uv_ref, gg_ref, rg_ref, ml_ref, og_ref, or_ref, gmn_ref, ws_ref, bs_ref, gdn_nw_ref,
                  ret_nw_ref, wgm_ref, wgdn_ref, wret_ref, o_ref):
    bf16 = jnp.bfloat16
    f32 = jnp.float32
    rows = uv_ref.shape[0]
    g = _gelu_exact(uv_ref[...])
    u = g[:, :GM_WIDTH]
    v = g[:, GM_WIDTH:]
    d = v - jnp.mean(v, axis=-1, keepdims=True)
    vn = (d * lax.rsqrt(jnp.mean(d * d, axis=-1, keepdims=True) + NORM_EPS) * gmn_ref[...]).astype(bf16)
    chunks = []
    for c in range(rows // GM_CHUNK):
        groups = []
        for gi in range(GM_GROUPS):
            blk = vn[c * GM_CHUNK:(c + 1) * GM_CHUNK, gi * GM_GROUP_CH:(gi + 1) * GM_GROUP_CH]
            groups.append(jnp.dot(ws_ref[gi], blk, preferred_element_type=f32) + bs_ref[gi])
        chunks.append(jnp.concatenate(groups, axis=-1))
    s = jnp.concatenate(chunks, axis=0)
    y_gm = jnp.dot((u * s).astype(bf16), wgm_ref[...], preferred_element_type=f32)

    og = og_ref[...]
    heads = []
    for h in range(GDN_HEADS):
        oh = og[:, h * GDN_HEAD_DIM:(h + 1) * GDN_HEAD_DIM]
        heads.append(oh * lax.rsqrt(jnp.mean(oh * oh, axis=-1, keepdims=True) + NORM_EPS) * gdn_nw_ref[...])
    gg = gg_ref[...]
    a_gdn = (jnp.concatenate(heads, axis=-1) * (gg * _sigmoid(gg))).astype(bf16)
    y_gdn = jnp.dot(a_gdn, wgdn_ref[...], preferred_element_type=f32)

    orr = or_ref[...]
    heads = []
    for h in range(RET_HEADS):
        oh = orr[:, h * RET_VAL_DIM:(h + 1) * RET_VAL_DIM]
        dh = oh - jnp.mean(oh, axis=-1, keepdims=True)
        heads.append(dh * lax.rsqrt(jnp.mean(dh * dh, axis=-1, keepdims=True) + NORM_EPS))
    rg = rg_ref[...]
    a_ret = (jnp.concatenate(heads, axis=-1) * ret_nw_ref[...] * (rg * _sigmoid(rg))).astype(bf16)
    y_ret = jnp.dot(a_ret, wret_ref[...], preferred_element_type=f32)

    sg = _sigmoid(ml_ref[...])
    merged = (sg[:, :D_MODEL] * y_gm + sg[:, D_MODEL:2 * D_MODEL] * y_gdn + sg[:, 2 * D_MODEL:] * y_ret)
    o_ref[...] = merged.astype(o_ref.dtype)


def _merge(p, o_gdn, o_ret, gm_norm_w, ws, bs, gdn_norm_w, ret_norm_w, wgm, wgdn, wret):
    M = p.shape[0]
    tm = MERGE_ROWS
    const = lambda *shape: pl.BlockSpec(shape, lambda i: (0,) * len(shape))
    return pl.pallas_call(
        _merge_kernel,
        out_shape=jax.ShapeDtypeStruct((M, D_MODEL), jnp.bfloat16),
        grid=(M // tm,),
        in_specs=[
            pl.BlockSpec((tm, GM_UV_COLS), lambda i: (i, C_UV // GM_UV_COLS)),
            pl.BlockSpec((tm, GDN_GATE_COLS), lambda i: (i, C_GG // GDN_GATE_COLS)),
            pl.BlockSpec((tm, RET_V_COLS), lambda i: (i, C_RG // RET_V_COLS)),
            pl.BlockSpec((tm, MERGE_COLS), lambda i: (i, C_ML // MERGE_COLS)),
            pl.BlockSpec((tm, GDN_GATE_COLS), lambda i: (i, 0)),
            pl.BlockSpec((tm, RET_V_COLS), lambda i: (i, 0)),
            const(1, GM_WIDTH),
            const(GM_GROUPS, GM_CHUNK, GM_CHUNK),
            const(GM_GROUPS, GM_CHUNK, GM_GROUP_CH),
            const(1, GDN_HEAD_DIM),
            const(1, RET_V_COLS),
            const(GM_WIDTH, D_MODEL),
            const(GDN_GATE_COLS, D_MODEL),
            const(RET_V_COLS, D_MODEL),
        ],
        out_specs=pl.BlockSpec((tm, D_MODEL), lambda i: (i, 0)),
        compiler_params=_cparams("parallel"),
        name="merge",
    )(p, p, p, p, o_gdn, o_ret, gm_norm_w.reshape(1, -1), ws, bs, gdn_norm_w.reshape(1, -1),
      ret_norm_w.reshape(1, -1), wgm, wgdn, wret)


def _moe_kernel(bexp_ref, bcnt_ref, stok_ref, sdst_ref,
                h_hbm, wg_ref, wu_ref, wd_ref, y_hbm,
                xbuf, ybuf, wgb, wub, wdb, gsem, ssem):
    i = pl.program_id(0)
    n = pl.num_programs(0)
    R = MOE_ROWS
    slot = lax.rem(i, 2)

    def gather_copy(blk, r, s):
        return pltpu.make_async_copy(h_hbm.at[:, stok_ref[blk * R + r], :], xbuf.at[s, :, r, :], gsem.at[s])

    def scatter_copy(blk, r, s):
        return pltpu.make_async_copy(ybuf.at[s, :, r, :], y_hbm.at[:, sdst_ref[blk * R + r], :], ssem.at[s])

    def for_rows(blk, fn):
        cnt = bcnt_ref[blk]
        full = lax.shift_right_logical(cnt, MOE_DMA_GROUP.bit_length() - 1)

        def group(g, c):
            for j in range(MOE_DMA_GROUP):
                fn(g * MOE_DMA_GROUP + j)
            return c
        lax.fori_loop(0, full, group, 0)
        lax.fori_loop(full * MOE_DMA_GROUP, cnt, lambda r, c: (fn(r), c)[1], 0)

    def start_gather(blk, s):
        for_rows(blk, lambda r: gather_copy(blk, r, s).start())

    def wait_gather(blk, s):
        for_rows(blk, lambda r: gather_copy(blk, r, s).wait())

    def start_scatter(blk, s):
        for_rows(blk, lambda r: scatter_copy(blk, r, s).start())

    def wait_scatter(blk, s):
        for_rows(blk, lambda r: scatter_copy(blk, r, s).wait())

    @pl.when(i == 0)
    def _():
        xbuf[...] = jnp.zeros_like(xbuf)
        start_gather(0, 0)

    @pl.when(i + 1 < n)
    def _():
        start_gather(i + 1, 1 - slot)

    changed = jnp.logical_or(i == 0, bexp_ref[i] != bexp_ref[jnp.maximum(i - 1, 0)])

    @pl.when(changed)
    def _():
        wgb[...] = wg_ref[0].astype(jnp.bfloat16)
        wub[...] = wu_ref[0].astype(jnp.bfloat16)
        wdb[...] = wd_ref[0].astype(jnp.bfloat16)

    wait_gather(i, slot)

    @pl.when(i >= 2)
    def _():
        wait_scatter(i - 2, slot)

    @pl.when(bcnt_ref[i] > 0)
    def _():
        ncol = xbuf.shape[1]
        xb = jnp.concatenate([xbuf[slot, c] for c in range(ncol)], axis=-1).astype(jnp.bfloat16)
        hg = jnp.dot(xb, wgb[...], preferred_element_type=jnp.float32)
        hu = jnp.dot(xb, wub[...], preferred_element_type=jnp.float32)
        act = (hg * _sigmoid(hg)) * hu
        y = jnp.dot(act.astype(jnp.bfloat16), wdb[...], preferred_element_type=jnp.float32)
        for c in range(ncol):
            ybuf[slot, c] = y[:, c * 128:(c + 1) * 128]
        start_scatter(i, slot)

    @pl.when(i == n - 1)
    def _():
        @pl.when(n >= 2)
        def _():
            wait_scatter(i - 1, 1 - slot)
        wait_scatter(i, slot)


def _moe(h, block_expert, block_cnt, slot_tok, slot_dst, w_gate, w_up, w_down):
    S, T, _ = h.shape
    D = S * 128
    n_blocks = block_expert.shape[0]
    R = MOE_ROWS
    wmap = lambda i, be, bc, st, sd: (be[i], 0, 0)
    grid_spec = pltpu.PrefetchScalarGridSpec(
        num_scalar_prefetch=4,
        grid=(n_blocks,),
        in_specs=[
            pl.BlockSpec(memory_space=pl.ANY),
            pl.BlockSpec((1, D, MOE_HIDDEN), wmap),
            pl.BlockSpec((1, D, MOE_HIDDEN), wmap),
            pl.BlockSpec((1, MOE_HIDDEN, D), wmap),
        ],
        out_specs=pl.BlockSpec(memory_space=pl.ANY),
        scratch_shapes=[
            pltpu.VMEM((2, S, R, 128), jnp.float32),
            pltpu.VMEM((2, S, R, 128), jnp.float32),
            pltpu.VMEM((D, MOE_HIDDEN), jnp.bfloat16),
            pltpu.VMEM((D, MOE_HIDDEN), jnp.bfloat16),
            pltpu.VMEM((MOE_HIDDEN, D), jnp.bfloat16),
            pltpu.SemaphoreType.DMA((2,)),
            pltpu.SemaphoreType.DMA((2,)),
        ],
    )
    return pl.pallas_call(
        _moe_kernel,
        out_shape=jax.ShapeDtypeStruct((S, MOE_TOP_K * T, 128), jnp.float32),
        grid_spec=grid_spec,
        compiler_params=_cparams("arbitrary"),
        name="moe_experts",
    )(block_expert, block_cnt, slot_tok, slot_dst, h, w_gate, w_up, w_down)


def _combine_kernel(x_ref, y0_ref, y1_ref, w_ref, g_ref, *rest, with_norm):
    w = w_ref[...]
    z = x_ref[0] + g_ref[0, 0] * (w[:, 0:1] * _join_cols(y0_ref) + w[:, 1:2] * _join_cols(y1_ref))
    if not with_norm:
        rest[0][0] = z
        return
    nw_ref, sh_ref, sc_ref, z_ref, h_ref = rest
    z_ref[0] = z
    y = z * lax.rsqrt(jnp.mean(z * z, axis=-1, keepdims=True) + NORM_EPS)
    h_ref[0] = (y * nw_ref[...] * (1.0 + sc_ref[0, 0]) + sh_ref[0, 0]).astype(h_ref.dtype)


def _combine(z, y, weights, gate, norm=None):
    B, LT, D = z.shape
    S = D // 128
    nt = LT // NORM_ROWS
    seg = lambda b, t: (b, jnp.minimum(t, 1), 0, 0)
    tile = pl.BlockSpec((1, NORM_ROWS, D), lambda b, t: (b, t, 0))
    in_specs = [
        tile,
        pl.BlockSpec((S, NORM_ROWS, 128), lambda b, t: (0, b * nt + t, 0)),
        pl.BlockSpec((S, NORM_ROWS, 128), lambda b, t: (0, B * nt + b * nt + t, 0)),
        pl.BlockSpec((NORM_ROWS, MOE_TOP_K), lambda b, t: (b * nt + t, 0)),
        pl.BlockSpec((1, 1, 1, D), seg),
    ]
    args = [z, y, y, weights, gate]
    out_shape = [jax.ShapeDtypeStruct((B, LT, D), jnp.float32)]
    out_specs = [tile]
    if norm is not None:
        norm_w, shift, scale = norm
        in_specs += [pl.BlockSpec((1, D), lambda b, t: (0, 0)), pl.BlockSpec((1, 1, 1, D), seg),
                     pl.BlockSpec((1, 1, 1, D), seg)]
        args += [norm_w.reshape(1, D), shift, scale]
        out_shape.append(jax.ShapeDtypeStruct((B, LT, D), jnp.bfloat16))
        out_specs.append(tile)
    out = pl.pallas_call(
        functools.partial(_combine_kernel, with_norm=norm is not None),
        out_shape=out_shape,
        grid=(B, nt),
        in_specs=in_specs,
        out_specs=out_specs,
        compiler_params=_cparams("parallel", "parallel"),
        name="moe_combine",
    )(*args)
    return (out[0], out[1]) if norm is not None else (out[0], None)


def _dispatch_tables(expert_idx):
    T, K = expert_idx.shape
    E, R = MOE_EXPERTS, MOE_ROWS
    A = T * K
    n_blocks = A // R + E
    flat_e = expert_idx.reshape(A).astype(jnp.int32)
    order = jnp.argsort(flat_e).astype(jnp.int32)
    cnt_end = jnp.sum(flat_e[None, :] <= jnp.arange(E, dtype=jnp.int32)[:, None], axis=1, dtype=jnp.int32)
    cnt_start = jnp.concatenate([jnp.zeros((1,), jnp.int32), cnt_end[:-1]])
    counts = cnt_end - cnt_start
    nblk = (counts + R - 1) // R
    blk_end = jnp.cumsum(nblk)
    blk_start = blk_end - nblk
    blk = jnp.arange(n_blocks, dtype=jnp.int32)
    block_expert = jnp.minimum(jnp.sum(blk_end[None, :] <= blk[:, None], axis=1, dtype=jnp.int32), E - 1)
    first = (blk - blk_start[block_expert]) * R
    block_cnt = jnp.clip(counts[block_expert] - first, 0, R).astype(jnp.int32)
    last_used = jnp.max(jnp.where(counts > 0, jnp.arange(E, dtype=jnp.int32), 0))
    block_expert = jnp.where(block_cnt > 0, block_expert, last_used)
    row = jnp.arange(R, dtype=jnp.int32)[None, :]
    src = (cnt_start[block_expert] + first)[:, None] + row
    a = order[jnp.clip(src, 0, A - 1)]
    valid = row < block_cnt[:, None]
    slot_tok = jnp.where(valid, a // K, 0).reshape(-1)
    slot_dst = jnp.where(valid, (a % K) * T + a // K, 0).reshape(-1)
    return block_expert, block_cnt, slot_tok, slot_dst


def _bdot(a, b):
    return jnp.dot(a.astype(jnp.bfloat16), b.astype(jnp.bfloat16), preferred_element_type=jnp.float32)


def _bdot_nt(a, b):
    return lax.dot_general(a.astype(jnp.bfloat16), b.astype(jnp.bfloat16), (((1,), (1,)), ((), ())),
                           preferred_element_type=jnp.float32)


def _bdot_tn(a, b):
    return lax.dot_general(a.astype(jnp.bfloat16), b.astype(jnp.bfloat16), (((0,), (0,)), ((), ())),
                           preferred_element_type=jnp.float32)


def _conv_silu(x_ref, col, cw, xp_s):
    LT = x_ref.shape[1]
    K = cw.shape[0]
    r = K // 2
    starts = (CONV_PAD, 2 * CONV_PAD + CTX_LEN)
    xp_s[pl.ds(starts[0], CTX_LEN), :] = x_ref[0, pl.ds(0, CTX_LEN), col:col + 128]
    xp_s[pl.ds(starts[1], LT - CTX_LEN), :] = x_ref[0, pl.ds(CTX_LEN, LT - CTX_LEN), col:col + 128]
    outs = []
    for start, n in zip(starts, (CTX_LEN, LT - CTX_LEN)):
        acc = xp_s[pl.ds(start - r, n), :] * cw[0:1, :]
        for j in range(1, K):
            acc = acc + xp_s[pl.ds(start - r + j, n), :] * cw[j:j + 1, :]
        outs.append(acc * _sigmoid(acc))
    return jnp.concatenate(outs, axis=0)


def _l2n(t):
    return t * lax.rsqrt(jnp.sum(t * t, axis=-1, keepdims=True) + NORM_EPS)


def _lane_col(x, idx):
    lane = lax.broadcasted_iota(jnp.int32, x.shape, 1)
    return jnp.sum(jnp.where(lane == idx, x, 0.0), axis=1, keepdims=True)


def _inv_unit_tri(Ns, eye, m16, moffs):
    Dg = [N * m16 for N in Ns]
    D2 = [_bdot(d, d) for d in Dg]
    D4 = [_bdot(d, d) for d in D2]
    D8 = [_bdot(d, d) for d in D4]
    T = [eye - d for d in Dg]
    for P in (D2, D4, D8):
        T = [t + _bdot(t, p) for t, p in zip(T, P)]
    for m in moffs:
        X = [_bdot(t, N * m) for t, N in zip(T, Ns)]
        T = [t - _bdot(x, t) for t, x in zip(T, X)]
    return T


def _gdn_kernel(q_ref, k_ref, v_ref, ab_ref, cwq_ref, cwk_ref, cwv_ref, alog_ref, dtb_ref, o_ref,
                qn_s, kn_s, vn_s, gb_s, gall_s, xp_s, u_s, w_s, qs_s, kst_s, qkd_s, el_s, o_s):
    f32 = jnp.float32
    bf16 = jnp.bfloat16
    HP = GDN_HEADS_PER_STEP
    Dh = GDN_HEAD_DIM
    hp = pl.program_id(1)
    LT = q_ref.shape[1]
    C = GDN_ROWS
    nch = LT // C
    nctx = CTX_LEN // C

    lane = lax.broadcasted_iota(jnp.int32, (LT, 128), 1)

    @pl.when(hp == 0)
    def _():
        ab = ab_ref[0]
        xa = ab + dtb_ref[...]
        g_all = -jnp.exp(alog_ref[...]) * (jnp.maximum(xa, 0.0) + jnp.log1p(jnp.exp(-jnp.abs(xa))))
        gall_s[...] = jnp.where(lane < 2 * GDN_HEADS, g_all, _sigmoid(ab))

    for start in (0, CONV_PAD + CTX_LEN, 2 * CONV_PAD + LT):
        xp_s[pl.ds(start, CONV_PAD), :] = jnp.zeros((CONV_PAD, 128), f32)

    gb_all = gall_s[...]
    for hh in range(HP):
        col = hh * Dh
        qn_s[hh] = _l2n(_conv_silu(q_ref, col, cwq_ref[:, col:col + Dh], xp_s)) * (Dh ** -0.5)
        kn_s[hh] = _l2n(_conv_silu(k_ref, col, cwk_ref[:, col:col + Dh], xp_s))
        vn_s[hh] = _conv_silu(v_ref, col, cwv_ref[:, col:col + Dh], xp_s)
        head = hp * HP + hh
        gb = jnp.where(lane == 0, _lane_col(gb_all, head), 0.0)
        for j in range(1, 4):
            gb = jnp.where(lane == j, _lane_col(gb_all, j * GDN_HEADS + head), gb)
        gb_s[hh] = gb

    ri = lax.broadcasted_iota(jnp.int32, (C, C), 0)
    ci = lax.broadcasted_iota(jnp.int32, (C, C), 1)
    eye = (ri == ci).astype(f32)
    low = (ri >= ci).astype(f32)
    upp = (ri <= ci).astype(f32)
    m16 = ((ri // 16) == (ci // 16)).astype(f32)
    moffs = [(((ri // (2 * s)) == (ci // (2 * s))) & ((ri // s) != (ci // s))).astype(f32) for s in (16, 32, 64)]
    hi = lax.Precision.HIGHEST

    def make_chunk_group(hh):
        def chunk_group(i, carry):
            probs = []
            for j in range(GDN_CHUNKS_PER_STEP):
                c = i * GDN_CHUNKS_PER_STEP + j
                rows = pl.ds(pl.multiple_of(c * C, C), C)
                q = qn_s[hh, rows, :]
                k = kn_s[hh, rows, :]
                v = vn_s[hh, rows, :]
                gbc = gb_s[hh, rows, :]
                kq = _bdot_nt(jnp.concatenate([k, q], axis=0), k)
                kk, qk = kq[:C], kq[C:]
                cs_f = jnp.dot(low, gbc, precision=hi, preferred_element_type=f32)
                cs_b = jnp.dot(upp, gbc, precision=hi, preferred_element_type=f32)
                for d in range(2):
                    cs = cs_f if d == 0 else cs_b
                    gc = cs[:, d:d + 1]
                    gc_row = cs.T[d:d + 1, :]
                    beta = gbc[:, 2 + d:3 + d]
                    incl = low if d == 0 else upp
                    decay = jnp.exp(jnp.where(incl > 0, gc - gc_row, -jnp.inf))
                    g_last = gc[C - 1:C, :] if d == 0 else gc[0:1, :]
                    probs.append(dict(d=d, c=c, q=q, k=k, v=v, qk=qk, gc=gc, beta=beta, decay=decay,
                                      g_last=g_last, N=(beta * kk) * decay * (1.0 - eye)))
            Ts = _inv_unit_tri([p["N"] for p in probs], eye, m16, moffs)
            for p, T in zip(probs, Ts):
                d, c, gc, beta = p["d"], p["c"], p["gc"], p["beta"]
                eg = jnp.exp(gc)
                uw = _bdot(T, jnp.concatenate([p["v"] * beta, p["k"] * beta * eg], axis=1))
                u_s[hh, d, c] = uw[:, :Dh]
                w_s[hh, d, c] = uw[:, Dh:].astype(bf16)
                qs_s[hh, d, c] = (p["q"] * eg).astype(bf16)
                kst_s[hh, d, c] = (p["k"] * jnp.exp(p["g_last"] - gc)).T.astype(bf16)
                qkd_s[hh, d, c] = (p["qk"] * p["decay"]).astype(bf16)
                el_s[hh, d, c] = jnp.broadcast_to(jnp.exp(p["g_last"]), (8, 128))
            return carry
        return chunk_group

    for hh in range(HP):
        lax.fori_loop(0, nch // GDN_CHUNKS_PER_STEP, make_chunk_group(hh), 0)

    def step(t, S):
        cb = jnp.where(t < nctx, nctx - 1 - t, nch - 1 + nctx - t)
        idx = [(hh, d, c) for hh in range(HP) for d, c in ((0, t), (1, cb))]
        mm = lambda a, b: jnp.dot(a, b, preferred_element_type=f32)
        Sbf = [Sd.astype(bf16) for Sd in S]
        wS = [mm(w_s[i], sb) for i, sb in zip(idx, Sbf)]
        vb = [(u_s[i] - x).astype(bf16) for i, x in zip(idx, wS)]
        upd = [mm(kst_s[i], v) for i, v in zip(idx, vb)]
        o_in = [mm(qkd_s[i], v) for i, v in zip(idx, vb)]
        o_st = [mm(qs_s[i], sb) for i, sb in zip(idx, Sbf)]
        for i, a, b in zip(idx, o_st, o_in):
            o_s[i] = a + b
        return tuple(Sd * el_s[i][0:1, :] + x for i, Sd, x in zip(idx, S, upd))

    z = jnp.zeros((Dh, Dh), f32)
    lax.fori_loop(0, nch, step, (z,) * (2 * HP))
    for hh in range(HP):
        o_ref[0, :, hh * Dh:(hh + 1) * Dh] = (o_s[hh, 0] + o_s[hh, 1]).reshape(LT, Dh)


def _gdn_mixer(p3, ab, conv_w, a_log, dt_bias):
    B, LT, _ = p3.shape
    H, Dh, C, HP = GDN_HEADS, GDN_HEAD_DIM, GDN_ROWS, GDN_HEADS_PER_STEP
    W = HP * Dh
    nch = LT // C
    qb = C_QKV // W
    alog_row = jnp.zeros((1, 128), jnp.float32).at[0, :2 * H].set(a_log.reshape(-1))
    dtb_row = jnp.zeros((1, 128), jnp.float32).at[0, :2 * H].set(dt_bias.reshape(-1))
    taps = conv_w.shape[0]
    nstep = H // HP
    per_head = lambda dt: pltpu.VMEM((HP, LT, Dh), dt)
    per_chunk = lambda dt: pltpu.VMEM((HP, 2, nch, C, Dh), dt)
    once = pl.Buffered(1)
    return pl.pallas_call(
        _gdn_kernel,
        out_shape=jax.ShapeDtypeStruct((B, LT, H * Dh), jnp.float32),
        grid=(B, nstep),
        in_specs=[
            pl.BlockSpec((1, LT, W), lambda b, h: (b, 0, qb + h), pipeline_mode=once),
            pl.BlockSpec((1, LT, W), lambda b, h: (b, 0, qb + nstep + h), pipeline_mode=once),
            pl.BlockSpec((1, LT, W), lambda b, h: (b, 0, qb + 2 * nstep + h), pipeline_mode=once),
            pl.BlockSpec((1, LT, 128), lambda b, h: (b, 0, 0)),
            pl.BlockSpec((taps, W), lambda b, h: (0, h)),
            pl.BlockSpec((taps, W), lambda b, h: (0, nstep + h)),
            pl.BlockSpec((taps, W), lambda b, h: (0, 2 * nstep + h)),
            pl.BlockSpec((1, 128), lambda b, h: (0, 0)),
            pl.BlockSpec((1, 128), lambda b, h: (0, 0)),
        ],
        out_specs=pl.BlockSpec((1, LT, W), lambda b, h: (b, 0, h)),
        scratch_shapes=[
            per_head(jnp.float32), per_head(jnp.float32), per_head(jnp.float32), per_head(jnp.float32),
            pltpu.VMEM((LT, 128), jnp.float32),
            pltpu.VMEM((LT + 3 * CONV_PAD, 128), jnp.float32),
            per_chunk(jnp.float32), per_chunk(jnp.bfloat16), per_chunk(jnp.bfloat16), per_chunk(jnp.bfloat16),
            per_chunk(jnp.bfloat16),
            pltpu.VMEM((HP, 2, nch, 8, 128), jnp.float32),
            per_chunk(jnp.float32),
        ],
        compiler_params=_cparams("parallel", "arbitrary"),
        name="gdn_mixer",
    )(p3, p3, p3, ab, conv_w, conv_w, conv_w, alog_row, dtb_row)


def _ret_kernel(lg_ref, q_ref, k_ref, v_ref, cos_ref, sin_ref, o_ref, qr_s, kr_s, st_s):
    f32 = jnp.float32
    h = pl.program_id(1)
    LT = q_ref.shape[1]
    C = RET_ROWS
    nch = LT // C
    nctx = CTX_LEN // C
    cos = cos_ref[...]
    sin = sin_ref[...]
    half = RET_KEY_DIM // 2
    q = q_ref[0]
    k = k_ref[0]
    qr_s[...] = (q * cos + pltpu.roll(q, half, axis=1) * sin) * (RET_KEY_DIM ** -0.5)
    kr_s[...] = k * cos + pltpu.roll(k, half, axis=1) * sin

    lg_f = lg_ref[0, h]
    lg_b = lg_ref[1, h]
    ri = lax.broadcasted_iota(jnp.int32, (C, C), 0)
    ci = lax.broadcasted_iota(jnp.int32, (C, C), 1)
    diff = (ri - ci).astype(f32)
    dm = (jnp.exp(jnp.where(diff >= 0, lg_f * diff, -jnp.inf))
          + jnp.exp(jnp.where(diff <= 0, -lg_b * diff, -jnp.inf)))
    pcol = lax.broadcasted_iota(jnp.int32, (C, 1), 0).astype(f32)
    cross_f = jnp.exp(lg_f * (pcol + 1.0))
    in_f = jnp.exp(lg_f * (C - 1.0 - pcol))
    cross_b = jnp.exp(lg_b * (C - pcol))
    in_b = jnp.exp(lg_b * pcol)
    cd_f = jnp.exp(lg_f * C)
    cd_b = jnp.exp(lg_b * C)

    G = RET_CHUNKS_PER_STEP
    chunk_rows = lambda i, j: pl.ds(pl.multiple_of((i * G + j) * C, C), C)

    def local(i, carry):
        rs = [chunk_rows(i, j) for j in range(G)]
        qs = [qr_s[r, :] for r in rs]
        ks = [kr_s[r, :] for r in rs]
        vs = [v_ref[0, r, :] for r in rs]
        ss = [_bdot_nt(q, k) for q, k in zip(qs, ks)]
        os_ = [_bdot(s * dm, v) for s, v in zip(ss, vs)]
        mf = [_bdot_tn(k * in_f, v) for k, v in zip(ks, vs)]
        mb = [_bdot_tn(k * in_b, v) for k, v in zip(ks, vs)]
        for j in range(G):
            o_ref[0, rs[j], :] = os_[j]
            st_s[0, i * G + j] = mf[j]
            st_s[1, i * G + j] = mb[j]
        return carry

    lax.fori_loop(0, nch // G, local, 0)

    fwd_order = list(range(nch))
    bwd_order = list(range(nctx - 1, -1, -1)) + list(range(nch - 1, nctx - 1, -1))
    for d, order, cd in ((0, fwd_order, cd_f), (1, bwd_order, cd_b)):
        S = jnp.zeros((RET_KEY_DIM, RET_VAL_DIM), f32)
        for c in order:
            contrib = st_s[d, c]
            st_s[d, c] = S
            S = S * cd + contrib

    def cross(i, carry):
        rs = [chunk_rows(i, j) for j in range(G)]
        qs = [qr_s[r, :] for r in rs]
        cf = [_bdot(q, st_s[0, i * G + j]) for j, q in enumerate(qs)]
        cb = [_bdot(q, st_s[1, i * G + j]) for j, q in enumerate(qs)]
        for j in range(G):
            o_ref[0, rs[j], :] += cf[j] * cross_f + cb[j] * cross_b
        return carry

    lax.fori_loop(0, nch // G, cross, 0)


def _ret_mixer(p3, log_gamma, cos_t, sin_t):
    B, LT, _ = p3.shape
    H, Dk, Dv = RET_HEADS, RET_KEY_DIM, RET_VAL_DIM
    grid_spec = pltpu.PrefetchScalarGridSpec(
        num_scalar_prefetch=1,
        grid=(B, H),
        in_specs=[
            pl.BlockSpec((1, LT, Dk), lambda b, h, lg: (b, 0, C_RQ // Dk + h)),
            pl.BlockSpec((1, LT, Dk), lambda b, h, lg: (b, 0, C_RK // Dk + h)),
            pl.BlockSpec((1, LT, Dv), lambda b, h, lg: (b, 0, C_RV // Dv + h)),
            pl.BlockSpec((LT, Dk), lambda b, h, lg: (0, 0)),
            pl.BlockSpec((LT, Dk), lambda b, h, lg: (0, 0)),
        ],
        out_specs=pl.BlockSpec((1, LT, Dv), lambda b, h, lg: (b, 0, h)),
        scratch_shapes=[pltpu.VMEM((LT, Dk), jnp.float32), pltpu.VMEM((LT, Dk), jnp.float32),
                        pltpu.VMEM((2, LT // RET_ROWS, Dk, Dv), jnp.float32)],
    )
    return pl.pallas_call(
        _ret_kernel,
        out_shape=jax.ShapeDtypeStruct((B, LT, H * Dv), jnp.float32),
        grid_spec=grid_spec,
        compiler_params=_cparams("parallel", "parallel"),
        name="ret_mixer",
    )(log_gamma, p3, p3, p3, cos_t, sin_t)


def _axial_rope(rows, dim):
    n = dim // 4
    freq = ROPE_BASE ** (-jnp.arange(n, dtype=jnp.float32) / n)
    row = jnp.repeat(jnp.arange(rows, dtype=jnp.float32), GRID_W)
    col = (jnp.arange(rows * GRID_W) % GRID_W).astype(jnp.float32)
    ang = jnp.concatenate([row[:, None] * freq, col[:, None] * freq], axis=-1)
    ang = jnp.concatenate([ang, ang], axis=-1)[:, None, :]
    return jnp.cos(ang), jnp.sin(ang)


def _rope_tables(L):
    cos, sin = _axial_rope(L // GRID_W, RET_KEY_DIM)
    half = RET_KEY_DIM // 2
    sign = jnp.concatenate([-jnp.ones((half,), jnp.float32), jnp.ones((half,), jnp.float32)])
    cos_t = jnp.concatenate([jnp.ones((CTX_LEN, RET_KEY_DIM), jnp.float32), cos[:, 0, :]], axis=0)
    sin_t = jnp.concatenate([jnp.zeros((CTX_LEN, RET_KEY_DIM), jnp.float32), sin[:, 0, :] * sign], axis=0)
    return cos_t, sin_t


def _seg_table(ctx_vec, lat_vec):
    B, D = lat_vec.shape
    return jnp.stack([jnp.broadcast_to(ctx_vec, (B, D)), lat_vec], axis=1).reshape(B, 2, 1, D)


def kernel(x, c, ctx, c_ctx, mod_w, mod_b, norm1_w, w_in, gm_norm_w, gm_spatial_w, gm_spatial_b, gdn_conv_w, gdn_a_log, gdn_dt_bias, gdn_norm_w, ret_decay_logit, ret_norm_w, w_br_gm, w_br_gdn, w_br_ret, w_out, norm2_w, router_group_w, router_group_b, router_expert_w, router_expert_b, moe_w_gate, moe_w_up, moe_w_down, final_norm_w):
    B, L, D = x.shape
    depth = mod_w.shape[0]
    LT = CTX_LEN + L
    T = B * LT
    bf16 = jnp.bfloat16
    cos_t, sin_t = _rope_tables(L)

    w_in_t = jnp.swapaxes(w_in, 1, 2)
    wgm_b, wgdn_b, wret_b, wout_b = (t.astype(bf16) for t in (w_br_gm, w_br_gdn, w_br_ret, w_out))
    ws_b = gm_spatial_w.astype(bf16)
    bs_full = jnp.broadcast_to(gm_spatial_b[..., None], gm_spatial_b.shape + (GM_GROUP_CH,))
    pad = 128 - MOE_GROUPS - MOE_EXPERTS
    w_router = jnp.concatenate([router_group_w, router_expert_w, jnp.zeros((depth, D, pad), jnp.float32)], axis=-1)
    b_router = jnp.concatenate([router_group_b, router_expert_b, jnp.zeros((depth, pad), jnp.float32)], axis=-1)
    moe_wg, moe_wu = (t.reshape(depth * MOE_EXPERTS, D, MOE_HIDDEN) for t in (moe_w_gate, moe_w_up))
    moe_wd = moe_w_down.reshape(depth * MOE_EXPERTS, MOE_HIDDEN, D)
    main_rows = lambda j: j * PROJ_COLS + jnp.where(j * PROJ_COLS < AB_START, 0, AB_END - AB_START)

    cc = jnp.concatenate([jax.nn.silu(c), jax.nn.silu(c_ctx)[None], jnp.zeros((16 - B - 1, D), c.dtype)], axis=0)
    cc = cc.astype(bf16)
    mods = []
    for l in range(depth):
        mod = _matmul(cc, mod_w, l, 16, 1024, "adaln_mod") + mod_b[l]
        mods.append([_seg_table(mod[B, i * D:(i + 1) * D], mod[:B, i * D:(i + 1) * D]) for i in range(6)])

    z = jnp.concatenate([ctx, x], axis=1)
    h = _normmod(z, norm1_w[0], mods[0][0], mods[0][1], bf16)
    for l in range(depth):
        sh1, sc1, gt1, sh2, sc2, gt2 = mods[l]
        log_gamma = jax.nn.log_sigmoid(ret_decay_logit[l])

        h = h.reshape(T, D)
        p = _in_proj(h, w_in_t, l, LT, PROJ_COLS, P_COLS, main_rows, "in_proj")
        pab = _in_proj(h, w_in_t, l, LT, 128, 128, lambda j: j * 0 + AB_START, "in_proj_ab")
        p3 = p.reshape(B, LT, P_COLS)
        o_gdn = _gdn_mixer(p3, pab.reshape(B, LT, 128), gdn_conv_w[l], gdn_a_log[l], gdn_dt_bias[l])
        o_ret = _ret_mixer(p3, log_gamma, cos_t, sin_t)
        o_gdn = o_gdn.reshape(T, GDN_GATE_COLS)
        o_ret = o_ret.reshape(T, RET_V_COLS)

        merged = _merge(p, o_gdn, o_ret, gm_norm_w[l], ws_b[l], bs_full[l], gdn_norm_w[l], ret_norm_w[l],
                        wgm_b[l], wgdn_b[l], wret_b[l])
        z, h2, route = _out_proj_route(merged, wout_b[l], z, gt1, norm2_w[l], sh2, sc2, w_router[l], b_router[l])
        expert_idx = route[:, :MOE_TOP_K].astype(jnp.int32)
        weights = route[:, MOE_TOP_K:2 * MOE_TOP_K]
        block_expert, block_cnt, slot_tok, slot_dst = _dispatch_tables(expert_idx)
        y = _moe(h2, block_expert + l * MOE_EXPERTS, block_cnt, slot_tok, slot_dst, moe_wg, moe_wu, moe_wd)
        nxt = (norm1_w[l + 1], mods[l + 1][0], mods[l + 1][1]) if l + 1 < depth else None
        z, h = _combine(z, y, weights, gt2, nxt)
    return _final_norm(z, final_norm_w)
```

```python
import functools
import math

import jax
import jax.numpy as jnp
from jax import lax
from jax.experimental import pallas as pl
from jax.experimental.pallas import tpu as pltpu

D_MODEL = 2048
CTX_LEN = 256
GRID_W = 64
NORM_EPS = 1e-6

GM_CHUNK = 128
GM_GROUPS = 4
GM_GROUP_CH = 128
GM_WIDTH = GM_GROUPS * GM_GROUP_CH
GDN_HEADS = 4
GDN_HEAD_DIM = 128
GDN_ROWS = 128
GDN_CHUNKS_PER_STEP = 6
GDN_HEADS_PER_STEP = 2
CONV_PAD = 8
RET_HEADS = 4
RET_KEY_DIM = 128
RET_VAL_DIM = 256
RET_ROWS = 256
RET_CHUNKS_PER_STEP = 3
ROPE_BASE = 10000.0
MOE_GROUPS = 4
MOE_EXPERTS_PER_GROUP = 8
MOE_EXPERTS = MOE_GROUPS * MOE_EXPERTS_PER_GROUP
MOE_TOP_K = 2
MOE_HIDDEN = 512

GM_UV_COLS = 2 * GM_WIDTH
GDN_QKV_COLS = 3 * GDN_HEADS * GDN_HEAD_DIM
GDN_AB_COLS = 2 * GDN_HEADS
GDN_GATE_COLS = GDN_HEADS * GDN_HEAD_DIM
RET_QK_COLS = RET_HEADS * RET_KEY_DIM
RET_V_COLS = RET_HEADS * RET_VAL_DIM
MERGE_COLS = 3 * D_MODEL
AB_START = GM_UV_COLS + GDN_QKV_COLS
AB_END = AB_START + 2 * GDN_AB_COLS

C_UV = 0
C_QKV = C_UV + GM_UV_COLS
C_RQ = C_QKV + GDN_QKV_COLS
C_RK = C_RQ + RET_QK_COLS
C_RV = C_RK + RET_QK_COLS
C_GG = C_RV + RET_V_COLS
C_RG = C_GG + GDN_GATE_COLS
C_ML = C_RG + RET_V_COLS
P_COLS = C_ML + MERGE_COLS

V7X_VMEM_LIMIT_BYTES = 56 * 1024 * 1024
MOE_ROWS = 256
MOE_DMA_GROUP = 8
MERGE_ROWS = 256
NORM_ROWS = 256
PROJ_COLS = 512


def _cparams(*sem):
    return pltpu.CompilerParams(dimension_semantics=sem, vmem_limit_bytes=V7X_VMEM_LIMIT_BYTES)


def _normmod_kernel(z_ref, nw_ref, sh_ref, sc_ref, o_ref):
    z = z_ref[0]
    y = z * lax.rsqrt(jnp.mean(z * z, axis=-1, keepdims=True) + NORM_EPS)
    y = y * nw_ref[...]
    o_ref[0] = (y * (1.0 + sc_ref[0, 0]) + sh_ref[0, 0]).astype(o_ref.dtype)


def _normmod(z, norm_w, shift, scale, out_dtype):
    B, LT, D = z.shape
    seg = lambda b, t: (b, jnp.minimum(t, 1), 0, 0)
    assert CTX_LEN == NORM_ROWS
    return pl.pallas_call(
        _normmod_kernel,
        out_shape=jax.ShapeDtypeStruct((B, LT, D), out_dtype),
        grid=(B, LT // NORM_ROWS),
        in_specs=[
            pl.BlockSpec((1, NORM_ROWS, D), lambda b, t: (b, t, 0)),
            pl.BlockSpec((1, D), lambda b, t: (0, 0)),
            pl.BlockSpec((1, 1, 1, D), seg),
            pl.BlockSpec((1, 1, 1, D), seg),
        ],
        out_specs=pl.BlockSpec((1, NORM_ROWS, D), lambda b, t: (b, t, 0)),
        compiler_params=_cparams("parallel", "parallel"),
        name="normmod",
    )(z, norm_w.reshape(1, D), shift, scale)


def _split_cols(ref, x):
    for c in range(x.shape[1] // 128):
        ref[c] = x[:, c * 128:(c + 1) * 128]


def _join_cols(ref):
    return jnp.concatenate([ref[c] for c in range(ref.shape[0])], axis=-1)


def _route_rows(logits):
    neg = jnp.float32(-3.0e38)
    lane = lax.broadcasted_iota(jnp.int32, logits.shape, 1)
    lane_f = lane.astype(jnp.float32)
    first_max = lambda v: jnp.min(jnp.where(v == jnp.max(v, axis=1, keepdims=True), lane_f, 128.0), axis=1,
                                  keepdims=True)
    is_g = lane < MOE_GROUPS
    gl = jnp.where(is_g, logits, neg)
    g_max = jnp.max(gl, axis=1, keepdims=True)
    grp = first_max(gl).astype(jnp.int32)
    p_top = 1.0 / jnp.sum(jnp.where(is_g, jnp.exp(logits - g_max), 0.0), axis=1, keepdims=True)
    e_lane = lane - MOE_GROUPS
    in_grp = (e_lane >= 0) & (e_lane < MOE_EXPERTS) & ((e_lane // MOE_EXPERTS_PER_GROUP) == grp)
    el = jnp.where(in_grp, logits, neg)
    m1 = jnp.max(el, axis=1, keepdims=True)
    i1 = first_max(el)
    el2 = jnp.where(lane_f == i1, neg, el)
    m2 = jnp.max(el2, axis=1, keepdims=True)
    i2 = first_max(el2)
    r = jnp.exp(m2 - m1)
    w1 = p_top / (1.0 + r)
    out = jnp.where(lane == 0, i1 - MOE_GROUPS, 0.0)
    out = jnp.where(lane == 1, i2 - MOE_GROUPS, out)
    out = jnp.where(lane == 2, w1, out)
    return jnp.where(lane == 3, w1 * r, out)


def _out_route_kernel(a_ref, w_ref, x_ref, g_ref, nw_ref, sh_ref, sc_ref, whi_ref, wlo_ref, rb_ref,
                      z_ref, h_ref, rt_ref):
    z = x_ref[0] + g_ref[0, 0] * jnp.dot(a_ref[...], w_ref[...], preferred_element_type=jnp.float32)
    z_ref[0] = z
    y = z * lax.rsqrt(jnp.mean(z * z, axis=-1, keepdims=True) + NORM_EPS)
    h = y * nw_ref[...] * (1.0 + sc_ref[0, 0]) + sh_ref[0, 0]
    _split_cols(h_ref, h)
    hi = h.astype(jnp.bfloat16)
    lo = (h - hi.astype(jnp.float32)).astype(jnp.bfloat16)
    whi = whi_ref[...]
    logits = (jnp.dot(hi, whi, preferred_element_type=jnp.float32)
              + jnp.dot(lo, whi, preferred_element_type=jnp.float32)
              + jnp.dot(hi, wlo_ref[...], preferred_element_type=jnp.float32))
    rt_ref[...] = _route_rows(logits + rb_ref[...])


def _out_proj_route(merged, w_out, z, gate, norm_w, shift, scale, w_router, b_router):
    B, LT, D = z.shape
    nt = LT // NORM_ROWS
    seg = lambda b, t: (b, jnp.minimum(t, 1), 0, 0)
    tile = pl.BlockSpec((1, NORM_ROWS, D), lambda b, t: (b, t, 0))
    const = lambda r, c: pl.BlockSpec((r, c), lambda b, t: (0, 0))
    whi = w_router.astype(jnp.bfloat16)
    wlo = (w_router - whi.astype(jnp.float32)).astype(jnp.bfloat16)
    return pl.pallas_call(
        _out_route_kernel,
        out_shape=[jax.ShapeDtypeStruct((B, LT, D), jnp.float32),
                   jax.ShapeDtypeStruct((D // 128, B * LT, 128), jnp.float32),
                   jax.ShapeDtypeStruct((B * LT, 128), jnp.float32)],
        grid=(B, nt),
        in_specs=[
            pl.BlockSpec((NORM_ROWS, D), lambda b, t: (b * nt + t, 0)),
            const(D, D),
            tile,
            pl.BlockSpec((1, 1, 1, D), seg),
            const(1, D),
            pl.BlockSpec((1, 1, 1, D), seg),
            pl.BlockSpec((1, 1, 1, D), seg),
            const(D, 128),
            const(D, 128),
            const(1, 128),
        ],
        out_specs=[tile,
                   pl.BlockSpec((D // 128, NORM_ROWS, 128), lambda b, t: (0, b * nt + t, 0)),
                   pl.BlockSpec((NORM_ROWS, 128), lambda b, t: (b * nt + t, 0))],
        compiler_params=_cparams("parallel", "parallel"),
        name="out_proj_route",
    )(merged, w_out, z, gate, norm_w.reshape(1, D), shift, scale, whi, wlo, b_router.reshape(1, 128))


def _combine_final_kernel(x_ref, y0_ref, y1_ref, w_ref, g_ref, nw_ref, o_ref):
    w = w_ref[...]
    z = x_ref[0] + g_ref[0, 0] * (w[:, 0:1] * _join_cols(y0_ref) + w[:, 1:2] * _join_cols(y1_ref))
    o_ref[0] = z * lax.rsqrt(jnp.mean(z * z, axis=-1, keepdims=True) + NORM_EPS) * nw_ref[...]


def _combine_final(z, y, weights, gate, norm_w):
    B, LT, D = z.shape
    S = D // 128
    nt = LT // NORM_ROWS
    skip = CTX_LEN // NORM_ROWS
    return pl.pallas_call(
        _combine_final_kernel,
        out_shape=jax.ShapeDtypeStruct((B, LT - CTX_LEN, D), jnp.float32),
        grid=(B, nt - skip),
        in_specs=[
            pl.BlockSpec((1, NORM_ROWS, D), lambda b, t: (b, t + skip, 0)),
            pl.BlockSpec((S, NORM_ROWS, 128), lambda b, t: (0, b * nt + t + skip, 0)),
            pl.BlockSpec((S, NORM_ROWS, 128), lambda b, t: (0, B * nt + b * nt + t + skip, 0)),
            pl.BlockSpec((NORM_ROWS, MOE_TOP_K), lambda b, t: (b * nt + t + skip, 0)),
            pl.BlockSpec((1, 1, 1, D), lambda b, t: (b, 1, 0, 0)),
            pl.BlockSpec((1, D), lambda b, t: (0, 0)),
        ],
        out_specs=pl.BlockSpec((1, NORM_ROWS, D), lambda b, t: (b, t, 0)),
        compiler_params=_cparams("parallel", "parallel"),
        name="combine_final_norm",
    )(z, y, y, weights, gate, norm_w.reshape(1, D))


def _mm_kernel(a_ref, w_ref, o_ref):
    o_ref[...] = jnp.dot(a_ref[...], w_ref[...].astype(jnp.bfloat16), preferred_element_type=jnp.float32)


def _matmul(a, w, layer, tm, tn, name):
    M, K = a.shape
    N = w.shape[2]
    return pl.pallas_call(
        _mm_kernel,
        out_shape=jax.ShapeDtypeStruct((M, N), jnp.float32),
        grid=(M // tm, N // tn),
        in_specs=[
            pl.BlockSpec((tm, K), lambda i, j: (i, 0)),
            pl.BlockSpec((None, K, tn), lambda i, j: (layer, 0, j)),
        ],
        out_specs=pl.BlockSpec((tm, tn), lambda i, j: (i, j)),
        compiler_params=_cparams("parallel", "arbitrary"),
        name=name,
    )(a, w)


def _mm_nt_kernel(a_ref, w_ref, o_ref):
    w = w_ref[0].astype(jnp.bfloat16)
    o_ref[...] = lax.dot_general(a_ref[...], w, (((1,), (1,)), ((), ())), preferred_element_type=jnp.float32)


def _in_proj(a, w_t, layer, tm, tn, n_out, first_row, name):
    M, K = a.shape
    wmap = lambda i, j: (layer, pl.multiple_of(first_row(j), 8), 0)
    return pl.pallas_call(
        _mm_nt_kernel,
        out_shape=jax.ShapeDtypeStruct((M, n_out), jnp.float32),
        grid=(M // tm, n_out // tn),
        in_specs=[
            pl.BlockSpec((tm, K), lambda i, j: (i, 0)),
            pl.BlockSpec((pl.Element(1), pl.Element(tn), pl.Element(K)), wmap),
        ],
        out_specs=pl.BlockSpec((tm, tn), lambda i, j: (i, j)),
        compiler_params=_cparams("parallel", "arbitrary"),
        name=name,
    )(a, w_t)


def _gelu_exact(x):
    return 0.5 * x * (1.0 + lax.erf(x * (1.0 / math.sqrt(2.0))))


def _sigmoid(x):
    return 1.0 / (1.0 + jnp.exp2(x * (-1.0 / math.log(2.0))))


def _merge_kernel(uv_ref, gg_ref, rg_ref, ml_ref, og_ref, or_ref, gmn_ref, ws_ref, bs_ref, gdn_nw_ref,
                  ret_nw_ref, wgm_ref, wgdn_ref, wret_ref, o_ref):
    bf16 = jnp.bfloat16
    f32 = jnp.float32
    rows = uv_ref.shape[0]
    g = _gelu_exact(uv_ref[...])
    u = g[:, :GM_WIDTH]
    v = g[:, GM_WIDTH:]
    d = v - jnp.mean(v, axis=-1, keepdims=True)
    vn = (d * lax.rsqrt(jnp.mean(d * d, axis=-1, keepdims=True) + NORM_EPS) * gmn_ref[...]).astype(bf16)
    chunks = []
    for c in range(rows // GM_CHUNK):
        groups = []
        for gi in range(GM_GROUPS):
            blk = vn[c * GM_CHUNK:(c + 1) * GM_CHUNK, gi * GM_GROUP_CH:(gi + 1) * GM_GROUP_CH]
            groups.append(jnp.dot(ws_ref[gi], blk, preferred_element_type=f32) + bs_ref[gi])
        chunks.append(jnp.concatenate(groups, axis=-1))
    s = jnp.concatenate(chunks, axis=0)
    y_gm = jnp.dot((u * s).astype(bf16), wgm_ref[...], preferred_element_type=f32)

    og = og_ref[...]
    heads = []
    for h in range(GDN_HEADS):
        oh = og[:, h * GDN_HEAD_DIM:(h + 1) * GDN_HEAD_DIM]
        heads.append(oh * lax.rsqrt(jnp.mean(oh * oh, axis=-1, keepdims=True) + NORM_EPS) * gdn_nw_ref[...])
    gg = gg_ref[...]
    a_gdn = (jnp.concatenate(heads, axis=-1) * (gg * _sigmoid(gg))).astype(bf16)
    y_gdn = jnp.dot(a_gdn, wgdn_ref[...], preferred_element_type=f32)

    orr = or_ref[...]
    heads = []
    for h in range(RET_HEADS):
        oh = orr[:, h * RET_VAL_DIM:(h + 1) * RET_VAL_DIM]
        dh = oh - jnp.mean(oh, axis=-1, keepdims=True)
        heads.append(dh * lax.rsqrt(jnp.mean(dh * dh, axis=-1, keepdims=True) + NORM_EPS))
    rg = rg_ref[...]
    a_ret = (jnp.concatenate(heads, axis=-1) * ret_nw_ref[...] * (rg * _sigmoid(rg))).astype(bf16)
    y_ret = jnp.dot(a_ret, wret_ref[...], preferred_element_type=f32)

    sg = _sigmoid(ml_ref[...])
    merged = (sg[:, :D_MODEL] * y_gm + sg[:, D_MODEL:2 * D_MODEL] * y_gdn + sg[:, 2 * D_MODEL:] * y_ret)
    o_ref[...] = merged.astype(o_ref.dtype)


def _merge(p, o_gdn, o_ret, gm_norm_w, ws, bs, gdn_norm_w, ret_norm_w, wgm, wgdn, wret):
    M = p.shape[0]
    tm = MERGE_ROWS
    const = lambda *shape: pl.BlockSpec(shape, lambda i: (0,) * len(shape))
    return pl.pallas_call(
        _merge_kernel,
        out_shape=jax.ShapeDtypeStruct((M, D_MODEL), jnp.bfloat16),
        grid=(M // tm,),
        in_specs=[
            pl.BlockSpec((tm, GM_UV_COLS), lambda i: (i, C_UV // GM_UV_COLS)),
            pl.BlockSpec((tm, GDN_GATE_COLS), lambda i: (i, C_GG // GDN_GATE_COLS)),
            pl.BlockSpec((tm, RET_V_COLS), lambda i: (i, C_RG // RET_V_COLS)),
            pl.BlockSpec((tm, MERGE_COLS), lambda i: (i, C_ML // MERGE_COLS)),
            pl.BlockSpec((tm, GDN_GATE_COLS), lambda i: (i, 0)),
            pl.BlockSpec((tm, RET_V_COLS), lambda i: (i, 0)),
            const(1, GM_WIDTH),
            const(GM_GROUPS, GM_CHUNK, GM_CHUNK),
            const(GM_GROUPS, GM_CHUNK, GM_GROUP_CH),
            const(1, GDN_HEAD_DIM),
            const(1, RET_V_COLS),
            const(GM_WIDTH, D_MODEL),
            const(GDN_GATE_COLS, D_MODEL),
            const(RET_V_COLS, D_MODEL),
        ],
        out_specs=pl.BlockSpec((tm, D_MODEL), lambda i: (i, 0)),
        compiler_params=_cparams("parallel"),
        name="merge",
    )(p, p, p, p, o_gdn, o_ret, gm_norm_w.reshape(1, -1), ws, bs, gdn_norm_w.reshape(1, -1),
      ret_norm_w.reshape(1, -1), wgm, wgdn, wret)


def _moe_kernel(bexp_ref, bcnt_ref, stok_ref, sdst_ref,
                h_hbm, wg_ref, wu_ref, wd_ref, y_hbm,
                xbuf, ybuf, wgb, wub, wdb, gsem, ssem):
    i = pl.program_id(0)
    n = pl.num_programs(0)
    R = MOE_ROWS
    slot = lax.rem(i, 2)

    def gather_copy(blk, r, s):
        return pltpu.make_async_copy(h_hbm.at[:, stok_ref[blk * R + r], :], xbuf.at[s, :, r, :], gsem.at[s])

    def scatter_copy(blk, r, s):
        return pltpu.make_async_copy(ybuf.at[s, :, r, :], y_hbm.at[:, sdst_ref[blk * R + r], :], ssem.at[s])

    def for_rows(blk, fn):
        cnt = bcnt_ref[blk]
        full = lax.shift_right_logical(cnt, MOE_DMA_GROUP.bit_length() - 1)

        def group(g, c):
            for j in range(MOE_DMA_GROUP):
                fn(g * MOE_DMA_GROUP + j)
            return c
        lax.fori_loop(0, full, group, 0)
        lax.fori_loop(full * MOE_DMA_GROUP, cnt, lambda r, c: (fn(r), c)[1], 0)

    def start_gather(blk, s):
        for_rows(blk, lambda r: gather_copy(blk, r, s).start())

    def wait_gather(blk, s):
        for_rows(blk, lambda r: gather_copy(blk, r, s).wait())

    def start_scatter(blk, s):
        for_rows(blk, lambda r: scatter_copy(blk, r, s).start())

    def wait_scatter(blk, s):
        for_rows(blk, lambda r: scatter_copy(blk, r, s).wait())

    @pl.when(i == 0)
    def _():
        xbuf[...] = jnp.zeros_like(xbuf)
        start_gather(0, 0)

    @pl.when(i + 1 < n)
    def _():
        start_gather(i + 1, 1 - slot)

    changed = jnp.logical_or(i == 0, bexp_ref[i] != bexp_ref[jnp.maximum(i - 1, 0)])

    @pl.when(changed)
    def _():
        wgb[...] = wg_ref[0].astype(jnp.bfloat16)
        wub[...] = wu_ref[0].astype(jnp.bfloat16)
        wdb[...] = wd_ref[0].astype(jnp.bfloat16)

    wait_gather(i, slot)

    @pl.when(i >= 2)
    def _():
        wait_scatter(i - 2, slot)

    @pl.when(bcnt_ref[i] > 0)
    def _():
        ncol = xbuf.shape[1]
        xb = jnp.concatenate([xbuf[slot, c] for c in range(ncol)], axis=-1).astype(jnp.bfloat16)
        hg = jnp.dot(xb, wgb[...], preferred_element_type=jnp.float32)
        hu = jnp.dot(xb, wub[...], preferred_element_type=jnp.float32)
        act = (hg * _sigmoid(hg)) * hu
        y = jnp.dot(act.astype(jnp.bfloat16), wdb[...], preferred_element_type=jnp.float32)
        for c in range(ncol):
            ybuf[slot, c] = y[:, c * 128:(c + 1) * 128]
        start_scatter(i, slot)

    @pl.when(i == n - 1)
    def _():
        @pl.when(n >= 2)
        def _():
            wait_scatter(i - 1, 1 - slot)
        wait_scatter(i, slot)


def _moe(h, block_expert, block_cnt, slot_tok, slot_dst, w_gate, w_up, w_down):
    S, T, _ = h.shape
    D = S * 128
    n_blocks = block_expert.shape[0]
    R = MOE_ROWS
    wmap = lambda i, be, bc, st, sd: (be[i], 0, 0)
    grid_spec = pltpu.PrefetchScalarGridSpec(
        num_scalar_prefetch=4,
        grid=(n_blocks,),
        in_specs=[
            pl.BlockSpec(memory_space=pl.ANY),
            pl.BlockSpec((1, D, MOE_HIDDEN), wmap),
            pl.BlockSpec((1, D, MOE_HIDDEN), wmap),
            pl.BlockSpec((1, MOE_HIDDEN, D), wmap),
        ],
        out_specs=pl.BlockSpec(memory_space=pl.ANY),
        scratch_shapes=[
            pltpu.VMEM((2, S, R, 128), jnp.float32),
            pltpu.VMEM((2, S, R, 128), jnp.float32),
            pltpu.VMEM((D, MOE_HIDDEN), jnp.bfloat16),
            pltpu.VMEM((D, MOE_HIDDEN), jnp.bfloat16),
            pltpu.VMEM((MOE_HIDDEN, D), jnp.bfloat16),
            pltpu.SemaphoreType.DMA((2,)),
            pltpu.SemaphoreType.DMA((2,)),
        ],
    )
    return pl.pallas_call(
        _moe_kernel,
        out_shape=jax.ShapeDtypeStruct((S, MOE_TOP_K * T, 128), jnp.float32),
        grid_spec=grid_spec,
        compiler_params=_cparams("arbitrary"),
        name="moe_experts",
    )(block_expert, block_cnt, slot_tok, slot_dst, h, w_gate, w_up, w_down)


def _combine_kernel(x_ref, y0_ref, y1_ref, w_ref, g_ref, nw_ref, sh_ref, sc_ref, z_ref, h_ref):
    w = w_ref[...]
    z = x_ref[0] + g_ref[0, 0] * (w[:, 0:1] * _join_cols(y0_ref) + w[:, 1:2] * _join_cols(y1_ref))
    z_ref[0] = z
    y = z * lax.rsqrt(jnp.mean(z * z, axis=-1, keepdims=True) + NORM_EPS)
    h_ref[0] = (y * nw_ref[...] * (1.0 + sc_ref[0, 0]) + sh_ref[0, 0]).astype(h_ref.dtype)


def _combine(z, y, weights, gate, norm_w, shift, scale):
    B, LT, D = z.shape
    S = D // 128
    nt = LT // NORM_ROWS
    seg = pl.BlockSpec((1, 1, 1, D), lambda b, t: (b, jnp.minimum(t, 1), 0, 0))
    tile = pl.BlockSpec((1, NORM_ROWS, D), lambda b, t: (b, t, 0))
    return pl.pallas_call(
        _combine_kernel,
        out_shape=[jax.ShapeDtypeStruct((B, LT, D), jnp.float32), jax.ShapeDtypeStruct((B, LT, D), jnp.bfloat16)],
        grid=(B, nt),
        in_specs=[
            tile,
            pl.BlockSpec((S, NORM_ROWS, 128), lambda b, t: (0, b * nt + t, 0)),
            pl.BlockSpec((S, NORM_ROWS, 128), lambda b, t: (0, B * nt + b * nt + t, 0)),
            pl.BlockSpec((NORM_ROWS, MOE_TOP_K), lambda b, t: (b * nt + t, 0)),
            seg,
            pl.BlockSpec((1, D), lambda b, t: (0, 0)),
            seg,
            seg,
        ],
        out_specs=[tile, tile],
        compiler_params=_cparams("parallel", "parallel"),
        name="moe_combine",
    )(z, y, y, weights, gate, norm_w.reshape(1, D), shift, scale)


def _dispatch_tables(expert_idx):
    T, K = expert_idx.shape
    E, R = MOE_EXPERTS, MOE_ROWS
    A = T * K
    n_blocks = A // R + E
    flat_e = expert_idx.reshape(A).astype(jnp.int32)
    order = jnp.argsort(flat_e).astype(jnp.int32)
    cnt_end = jnp.sum(flat_e[None, :] <= jnp.arange(E, dtype=jnp.int32)[:, None], axis=1, dtype=jnp.int32)
    cnt_start = jnp.concatenate([jnp.zeros((1,), jnp.int32), cnt_end[:-1]])
    counts = cnt_end - cnt_start
    nblk = (counts + R - 1) // R
    blk_end = jnp.cumsum(nblk)
    blk_start = blk_end - nblk
    blk = jnp.arange(n_blocks, dtype=jnp.int32)
    block_expert = jnp.minimum(jnp.sum(blk_end[None, :] <= blk[:, None], axis=1, dtype=jnp.int32), E - 1)
    first = (blk - blk_start[block_expert]) * R
    block_cnt = jnp.clip(counts[block_expert] - first, 0, R).astype(jnp.int32)
    last_used = jnp.max(jnp.where(counts > 0, jnp.arange(E, dtype=jnp.int32), 0))
    block_expert = jnp.where(block_cnt > 0, block_expert, last_used)
    row = jnp.arange(R, dtype=jnp.int32)[None, :]
    src = (cnt_start[block_expert] + first)[:, None] + row
    a = order[jnp.clip(src, 0, A - 1)]
    valid = row < block_cnt[:, None]
    slot_tok = jnp.where(valid, a // K, 0).reshape(-1)
    slot_dst = jnp.where(valid, (a % K) * T + a // K, 0).reshape(-1)
    return block_expert, block_cnt, slot_tok, slot_dst


def _bdot(a, b):
    return jnp.dot(a.astype(jnp.bfloat16), b.astype(jnp.bfloat16), preferred_element_type=jnp.float32)


def _bdot_nt(a, b):
    return lax.dot_general(a.astype(jnp.bfloat16), b.astype(jnp.bfloat16), (((1,), (1,)), ((), ())),
                           preferred_element_type=jnp.float32)


def _bdot_tn(a, b):
    return lax.dot_general(a.astype(jnp.bfloat16), b.astype(jnp.bfloat16), (((0,), (0,)), ((), ())),
                           preferred_element_type=jnp.float32)


def _conv_silu(x_ref, col, cw, xp_s):
    LT = x_ref.shape[1]
    K = cw.shape[0]
    r = K // 2
    starts = (CONV_PAD, 2 * CONV_PAD + CTX_LEN)
    xp_s[pl.ds(starts[0], CTX_LEN), :] = x_ref[0, pl.ds(0, CTX_LEN), col:col + 128]
    xp_s[pl.ds(starts[1], LT - CTX_LEN), :] = x_ref[0, pl.ds(CTX_LEN, LT - CTX_LEN), col:col + 128]
    outs = []
    for start, n in zip(starts, (CTX_LEN, LT - CTX_LEN)):
        acc = xp_s[pl.ds(start - r, n), :] * cw[0:1, :]
        for j in range(1, K):
            acc = acc + xp_s[pl.ds(start - r + j, n), :] * cw[j:j + 1, :]
        outs.append(acc * _sigmoid(acc))
    return jnp.concatenate(outs, axis=0)


def _l2n(t):
    return t * lax.rsqrt(jnp.sum(t * t, axis=-1, keepdims=True) + NORM_EPS)


def _lane_col(x, idx):
    lane = lax.broadcasted_iota(jnp.int32, x.shape, 1)
    return jnp.sum(jnp.where(lane == idx, x, 0.0), axis=1, keepdims=True)


def _inv_unit_tri(Ns, eye, m16, moffs):
    Dg = [N * m16 for N in Ns]
    D2 = [_bdot(d, d) for d in Dg]
    D4 = [_bdot(d, d) for d in D2]
    D8 = [_bdot(d, d) for d in D4]
    T = [eye - d for d in Dg]
    for P in (D2, D4, D8):
        T = [t + _bdot(t, p) for t, p in zip(T, P)]
    for m in moffs:
        X = [_bdot(t, N * m) for t, N in zip(T, Ns)]
        T = [t - _bdot(x, t) for t, x in zip(T, X)]
    return T


def _gdn_kernel(q_ref, k_ref, v_ref, ab_ref, cwq_ref, cwk_ref, cwv_ref, alog_ref, dtb_ref, o_ref,
                qn_s, kn_s, vn_s, gb_s, gall_s, xp_s, u_s, w_s, qs_s, kst_s, qkd_s, el_s, o_s):
    f32 = jnp.float32
    bf16 = jnp.bfloat16
    HP = GDN_HEADS_PER_STEP
    Dh = GDN_HEAD_DIM
    hp = pl.program_id(1)
    LT = q_ref.shape[1]
    C = GDN_ROWS
    nch = LT // C
    nctx = CTX_LEN // C

    lane = lax.broadcasted_iota(jnp.int32, (LT, 128), 1)

    @pl.when(hp == 0)
    def _():
        ab = ab_ref[0]
        xa = ab + dtb_ref[...]
        g_all = -jnp.exp(alog_ref[...]) * (jnp.maximum(xa, 0.0) + jnp.log1p(jnp.exp(-jnp.abs(xa))))
        gall_s[...] = jnp.where(lane < 2 * GDN_HEADS, g_all, _sigmoid(ab))

    for start in (0, CONV_PAD + CTX_LEN, 2 * CONV_PAD + LT):
        xp_s[pl.ds(start, CONV_PAD), :] = jnp.zeros((CONV_PAD, 128), f32)

    gb_all = gall_s[...]
    for hh in range(HP):
        col = hh * Dh
        qn_s[hh] = _l2n(_conv_silu(q_ref, col, cwq_ref[:, col:col + Dh], xp_s)) * (Dh ** -0.5)
        kn_s[hh] = _l2n(_conv_silu(k_ref, col, cwk_ref[:, col:col + Dh], xp_s))
        vn_s[hh] = _conv_silu(v_ref, col, cwv_ref[:, col:col + Dh], xp_s)
        head = hp * HP + hh
        gb = jnp.where(lane == 0, _lane_col(gb_all, head), 0.0)
        for j in range(1, 4):
            gb = jnp.where(lane == j, _lane_col(gb_all, j * GDN_HEADS + head), gb)
        gb_s[hh] = gb

    ri = lax.broadcasted_iota(jnp.int32, (C, C), 0)
    ci = lax.broadcasted_iota(jnp.int32, (C, C), 1)
    eye = (ri == ci).astype(f32)
    low = (ri >= ci).astype(f32)
    upp = (ri <= ci).astype(f32)
    m16 = ((ri // 16) == (ci // 16)).astype(f32)
    moffs = [(((ri // (2 * s)) == (ci // (2 * s))) & ((ri // s) != (ci // s))).astype(f32) for s in (16, 32, 64)]
    hi = lax.Precision.HIGHEST

    def make_chunk_group(hh):
        def chunk_group(i, carry):
            probs = []
            for j in range(GDN_CHUNKS_PER_STEP):
                c = i * GDN_CHUNKS_PER_STEP + j
                rows = pl.ds(pl.multiple_of(c * C, C), C)
                q = qn_s[hh, rows, :]
                k = kn_s[hh, rows, :]
                v = vn_s[hh, rows, :]
                gbc = gb_s[hh, rows, :]
                kq = _bdot_nt(jnp.concatenate([k, q], axis=0), k)
                kk, qk = kq[:C], kq[C:]
                cs_f = jnp.dot(low, gbc, precision=hi, preferred_element_type=f32)
                cs_b = jnp.dot(upp, gbc, precision=hi, preferred_element_type=f32)
                for d in range(2):
                    cs = cs_f if d == 0 else cs_b
                    gc = cs[:, d:d + 1]
                    gc_row = cs.T[d:d + 1, :]
                    beta = gbc[:, 2 + d:3 + d]
                    incl = low if d == 0 else upp
                    decay = jnp.exp(jnp.where(incl > 0, gc - gc_row, -jnp.inf))
                    g_last = gc[C - 1:C, :] if d == 0 else gc[0:1, :]
                    probs.append(dict(d=d, c=c, q=q, k=k, v=v, qk=qk, gc=gc, beta=beta, decay=decay,
                                      g_last=g_last, N=(beta * kk) * decay * (1.0 - eye)))
            Ts = _inv_unit_tri([p["N"] for p in probs], eye, m16, moffs)
            for p, T in zip(probs, Ts):
                d, c, gc, beta = p["d"], p["c"], p["gc"], p["beta"]
                eg = jnp.exp(gc)
                uw = _bdot(T, jnp.concatenate([p["v"] * beta, p["k"] * beta * eg], axis=1))
                u_s[hh, d, c] = uw[:, :Dh]
                w_s[hh, d, c] = uw[:, Dh:].astype(bf16)
                qs_s[hh, d, c] = (p["q"] * eg).astype(bf16)
                kst_s[hh, d, c] = (p["k"] * jnp.exp(p["g_last"] - gc)).T.astype(bf16)
                qkd_s[hh, d, c] = (p["qk"] * p["decay"]).astype(bf16)
                el_s[hh, d, c] = jnp.broadcast_to(jnp.exp(p["g_last"]), (8, 128))
            return carry
        return chunk_group

    for hh in range(HP):
        lax.fori_loop(0, nch // GDN_CHUNKS_PER_STEP, make_chunk_group(hh), 0)

    def step(t, S):
        cb = jnp.where(t < nctx, nctx - 1 - t, nch - 1 + nctx - t)
        idx = [(hh, d, c) for hh in range(HP) for d, c in ((0, t), (1, cb))]
        mm = lambda a, b: jnp.dot(a, b, preferred_element_type=f32)
        Sbf = [Sd.astype(bf16) for Sd in S]
        wS = [mm(w_s[i], sb) for i, sb in zip(idx, Sbf)]
        vb = [(u_s[i] - x).astype(bf16) for i, x in zip(idx, wS)]
        upd = [mm(kst_s[i], v) for i, v in zip(idx, vb)]
        o_in = [mm(qkd_s[i], v) for i, v in zip(idx, vb)]
        o_st = [mm(qs_s[i], sb) for i, sb in zip(idx, Sbf)]
        for i, a, b in zip(idx, o_st, o_in):
            o_s[i] = a + b
        return tuple(Sd * el_s[i][0:1, :] + x for i, Sd, x in zip(idx, S, upd))

    z = jnp.zeros((Dh, Dh), f32)
    lax.fori_loop(0, nch, step, (z,) * (2 * HP))
    for hh in range(HP):
        o_ref[0, :, hh * Dh:(hh + 1) * Dh] = (o_s[hh, 0] + o_s[hh, 1]).reshape(LT, Dh)


def _gdn_mixer(p3, ab, conv_w, a_log, dt_bias):
    B, LT, _ = p3.shape
    H, Dh, C, HP = GDN_HEADS, GDN_HEAD_DIM, GDN_ROWS, GDN_HEADS_PER_STEP
    W = HP * Dh
    nch = LT // C
    qb = C_QKV // W
    alog_row = jnp.zeros((1, 128), jnp.float32).at[0, :2 * H].set(a_log.reshape(-1))
    dtb_row = jnp.zeros((1, 128), jnp.float32).at[0, :2 * H].set(dt_bias.reshape(-1))
    taps = conv_w.shape[0]
    nstep = H // HP
    per_head = lambda dt: pltpu.VMEM((HP, LT, Dh), dt)
    per_chunk = lambda dt: pltpu.VMEM((HP, 2, nch, C, Dh), dt)
    once = pl.Buffered(1)
    return pl.pallas_call(
        _gdn_kernel,
        out_shape=jax.ShapeDtypeStruct((B, LT, H * Dh), jnp.float32),
        grid=(B, nstep),
        in_specs=[
            pl.BlockSpec((1, LT, W), lambda b, h: (b, 0, qb + h), pipeline_mode=once),
            pl.BlockSpec((1, LT, W), lambda b, h: (b, 0, qb + nstep + h), pipeline_mode=once),
            pl.BlockSpec((1, LT, W), lambda b, h: (b, 0, qb + 2 * nstep + h), pipeline_mode=once),
            pl.BlockSpec((1, LT, 128), lambda b, h: (b, 0, 0)),
            pl.BlockSpec((taps, W), lambda b, h: (0, h)),
            pl.BlockSpec((taps, W), lambda b, h: (0, nstep + h)),
            pl.BlockSpec((taps, W), lambda b, h: (0, 2 * nstep + h)),
            pl.BlockSpec((1, 128), lambda b, h: (0, 0)),
            pl.BlockSpec((1, 128), lambda b, h: (0, 0)),
        ],
        out_specs=pl.BlockSpec((1, LT, W), lambda b, h: (b, 0, h)),
        scratch_shapes=[
            per_head(jnp.float32), per_head(jnp.float32), per_head(jnp.float32), per_head(jnp.float32),
            pltpu.VMEM((LT, 128), jnp.float32),
            pltpu.VMEM((LT + 3 * CONV_PAD, 128), jnp.float32),
            per_chunk(jnp.float32), per_chunk(jnp.bfloat16), per_chunk(jnp.bfloat16), per_chunk(jnp.bfloat16),
            per_chunk(jnp.bfloat16),
            pltpu.VMEM((HP, 2, nch, 8, 128), jnp.float32),
            per_chunk(jnp.float32),
        ],
        compiler_params=_cparams("parallel", "arbitrary"),
        name="gdn_mixer",
    )(p3, p3, p3, ab, conv_w, conv_w, conv_w, alog_row, dtb_row)


def _ret_kernel(lg_ref, q_ref, k_ref, v_ref, cos_ref, sin_ref, o_ref, qr_s, kr_s, st_s):
    f32 = jnp.float32
    h = pl.program_id(1)
    LT = q_ref.shape[1]
    C = RET_ROWS
    nch = LT // C
    nctx = CTX_LEN // C
    cos = cos_ref[...]
    sin = sin_ref[...]
    half = RET_KEY_DIM // 2
    q = q_ref[0]
    k = k_ref[0]
    qr_s[...] = (q * cos + pltpu.roll(q, half, axis=1) * sin) * (RET_KEY_DIM ** -0.5)
    kr_s[...] = k * cos + pltpu.roll(k, half, axis=1) * sin

    lg_f = lg_ref[0, h]
    lg_b = lg_ref[1, h]
    ri = lax.broadcasted_iota(jnp.int32, (C, C), 0)
    ci = lax.broadcasted_iota(jnp.int32, (C, C), 1)
    diff = (ri - ci).astype(f32)
    dm = (jnp.exp(jnp.where(diff >= 0, lg_f * diff, -jnp.inf))
          + jnp.exp(jnp.where(diff <= 0, -lg_b * diff, -jnp.inf)))
    pcol = lax.broadcasted_iota(jnp.int32, (C, 1), 0).astype(f32)
    cross_f = jnp.exp(lg_f * (pcol + 1.0))
    in_f = jnp.exp(lg_f * (C - 1.0 - pcol))
    cross_b = jnp.exp(lg_b * (C - pcol))
    in_b = jnp.exp(lg_b * pcol)
    cd_f = jnp.exp(lg_f * C)
    cd_b = jnp.exp(lg_b * C)

    G = RET_CHUNKS_PER_STEP
    chunk_rows = lambda i, j: pl.ds(pl.multiple_of((i * G + j) * C, C), C)

    def local(i, carry):
        rs = [chunk_rows(i, j) for j in range(G)]
        qs = [qr_s[r, :] for r in rs]
        ks = [kr_s[r, :] for r in rs]
        vs = [v_ref[0, r, :] for r in rs]
        ss = [_bdot_nt(q, k) for q, k in zip(qs, ks)]
        os_ = [_bdot(s * dm, v) for s, v in zip(ss, vs)]
        mf = [_bdot_tn(k * in_f, v) for k, v in zip(ks, vs)]
        mb = [_bdot_tn(k * in_b, v) for k, v in zip(ks, vs)]
        for j in range(G):
            o_ref[0, rs[j], :] = os_[j]
            st_s[0, i * G + j] = mf[j]
            st_s[1, i * G + j] = mb[j]
        return carry

    lax.fori_loop(0, nch // G, local, 0)

    fwd_order = list(range(nch))
    bwd_order = list(range(nctx - 1, -1, -1)) + list(range(nch - 1, nctx - 1, -1))
    for d, order, cd in ((0, fwd_order, cd_f), (1, bwd_order, cd_b)):
        S = jnp.zeros((RET_KEY_DIM, RET_VAL_DIM), f32)
        for c in order:
            contrib = st_s[d, c]
            st_s[d, c] = S
            S = S * cd + contrib

    def cross(i, carry):
        rs = [chunk_rows(i, j) for j in range(G)]
        qs = [qr_s[r, :] for r in rs]
        cf = [_bdot(q, st_s[0, i * G + j]) for j, q in enumerate(qs)]
        cb = [_bdot(q, st_s[1, i * G + j]) for j, q in enumerate(qs)]
        for j in range(G):
            o_ref[0, rs[j], :] += cf[j] * cross_f + cb[j] * cross_b
        return carry

    lax.fori_loop(0, nch // G, cross, 0)


def _ret_mixer(p3, log_gamma, cos_t, sin_t):
    B, LT, _ = p3.shape
    H, Dk, Dv = RET_HEADS, RET_KEY_DIM, RET_VAL_DIM
    grid_spec = pltpu.PrefetchScalarGridSpec(
        num_scalar_prefetch=1,
        grid=(B, H),
        in_specs=[
            pl.BlockSpec((1, LT, Dk), lambda b, h, lg: (b, 0, C_RQ // Dk + h)),
            pl.BlockSpec((1, LT, Dk), lambda b, h, lg: (b, 0, C_RK // Dk + h)),
            pl.BlockSpec((1, LT, Dv), lambda b, h, lg: (b, 0, C_RV // Dv + h)),
            pl.BlockSpec((LT, Dk), lambda b, h, lg: (0, 0)),
            pl.BlockSpec((LT, Dk), lambda b, h, lg: (0, 0)),
        ],
        out_specs=pl.BlockSpec((1, LT, Dv), lambda b, h, lg: (b, 0, h)),
        scratch_shapes=[pltpu.VMEM((LT, Dk), jnp.float32), pltpu.VMEM((LT, Dk), jnp.float32),
                        pltpu.VMEM((2, LT // RET_ROWS, Dk, Dv), jnp.float32)],
    )
    return pl.pallas_call(
        _ret_kernel,
        out_shape=jax.ShapeDtypeStruct((B, LT, H * Dv), jnp.float32),
        grid_spec=grid_spec,
        compiler_params=_cparams("parallel", "parallel"),
        name="ret_mixer",
    )(log_gamma, p3, p3, p3, cos_t, sin_t)


def _axial_rope(rows, dim):
    n = dim // 4
    freq = ROPE_BASE ** (-jnp.arange(n, dtype=jnp.float32) / n)
    row = jnp.repeat(jnp.arange(rows, dtype=jnp.float32), GRID_W)
    col = (jnp.arange(rows * GRID_W) % GRID_W).astype(jnp.float32)
    ang = jnp.concatenate([row[:, None] * freq, col[:, None] * freq], axis=-1)
    ang = jnp.concatenate([ang, ang], axis=-1)[:, None, :]
    return jnp.cos(ang), jnp.sin(ang)


def _rope_tables(L):
    cos, sin = _axial_rope(L // GRID_W, RET_KEY_DIM)
    half = RET_KEY_DIM // 2
    sign = jnp.concatenate([-jnp.ones((half,), jnp.float32), jnp.ones((half,), jnp.float32)])
    cos_t = jnp.concatenate([jnp.ones((CTX_LEN, RET_KEY_DIM), jnp.float32), cos[:, 0, :]], axis=0)
    sin_t = jnp.concatenate([jnp.zeros((CTX_LEN, RET_KEY_DIM), jnp.float32), sin[:, 0, :] * sign], axis=0)
    return cos_t, sin_t


def _seg_table(ctx_vec, lat_vec):
    B, D = lat_vec.shape
    return jnp.stack([jnp.broadcast_to(ctx_vec, (B, D)), lat_vec], axis=1).reshape(B, 2, 1, D)


def kernel(x, c, ctx, c_ctx, mod_w, mod_b, norm1_w, w_in, gm_norm_w, gm_spatial_w, gm_spatial_b, gdn_conv_w, gdn_a_log, gdn_dt_bias, gdn_norm_w, ret_decay_logit, ret_norm_w, w_br_gm, w_br_gdn, w_br_ret, w_out, norm2_w, router_group_w, router_group_b, router_expert_w, router_expert_b, moe_w_gate, moe_w_up, moe_w_down, final_norm_w):
    B, L, D = x.shape
    depth = mod_w.shape[0]
    LT = CTX_LEN + L
    T = B * LT
    bf16 = jnp.bfloat16
    cos_t, sin_t = _rope_tables(L)

    w_in_t = jnp.swapaxes(w_in, 1, 2)
    wgm_b, wgdn_b, wret_b, wout_b = (t.astype(bf16) for t in (w_br_gm, w_br_gdn, w_br_ret, w_out))
    ws_b = gm_spatial_w.astype(bf16)
    bs_full = jnp.broadcast_to(gm_spatial_b[..., None], gm_spatial_b.shape + (GM_GROUP_CH,))
    pad = 128 - MOE_GROUPS - MOE_EXPERTS
    w_router = jnp.concatenate([router_group_w, router_expert_w, jnp.zeros((depth, D, pad), jnp.float32)], axis=-1)
    b_router = jnp.concatenate([router_group_b, router_expert_b, jnp.zeros((depth, pad), jnp.float32)], axis=-1)
    moe_wg, moe_wu = (t.reshape(depth * MOE_EXPERTS, D, MOE_HIDDEN) for t in (moe_w_gate, moe_w_up))
    moe_wd = moe_w_down.reshape(depth * MOE_EXPERTS, MOE_HIDDEN, D)
    main_rows = lambda j: j * PROJ_COLS + jnp.where(j * PROJ_COLS < AB_START, 0, AB_END - AB_START)

    cc = jnp.concatenate([jax.nn.silu(c), jax.nn.silu(c_ctx)[None], jnp.zeros((16 - B - 1, D), c.dtype)], axis=0)
    cc = cc.astype(bf16)
    mods = []
    for l in range(depth):
        mod = _matmul(cc, mod_w, l, 16, 1024, "adaln_mod") + mod_b[l]
        mods.append([_seg_table(mod[B, i * D:(i + 1) * D], mod[:B, i * D:(i + 1) * D]) for i in range(6)])

    z = jnp.concatenate([ctx, x], axis=1)
    h = _normmod(z, norm1_w[0], mods[0][0], mods[0][1], bf16)
    for l in range(depth):
        sh1, sc1, gt1, sh2, sc2, gt2 = mods[l]
        log_gamma = jax.nn.log_sigmoid(ret_decay_logit[l])

        h = h.reshape(T, D)
        p = _in_proj(h, w_in_t, l, LT, PROJ_COLS, P_COLS, main_rows, "in_proj")
        pab = _in_proj(h, w_in_t, l, LT, 128, 128, lambda j: j * 0 + AB_START, "in_proj_ab")
        p3 = p.reshape(B, LT, P_COLS)
        o_gdn = _gdn_mixer(p3, pab.reshape(B, LT, 128), gdn_conv_w[l], gdn_a_log[l], gdn_dt_bias[l])
        o_ret = _ret_mixer(p3, log_gamma, cos_t, sin_t)
        o_gdn = o_gdn.reshape(T, GDN_GATE_COLS)
        o_ret = o_ret.reshape(T, RET_V_COLS)

        merged = _merge(p, o_gdn, o_ret, gm_norm_w[l], ws_b[l], bs_full[l], gdn_norm_w[l], ret_norm_w[l],
                        wgm_b[l], wgdn_b[l], wret_b[l])
        z, h2, route = _out_proj_route(merged, wout_b[l], z, gt1, norm2_w[l], sh2, sc2, w_router[l], b_router[l])
        expert_idx = route[:, :MOE_TOP_K].astype(jnp.int32)
        weights = route[:, MOE_TOP_K:2 * MOE_TOP_K]
        block_expert, block_cnt, slot_tok, slot_dst = _dispatch_tables(expert_idx)
        y = _moe(h2, block_expert + l * MOE_EXPERTS, block_cnt, slot_tok, slot_dst, moe_wg, moe_wu, moe_wd)
        if l + 1 == depth:
            return _combine_final(z, y, weights, gt2, final_norm_w)
        z, h = _combine(z, y, weights, gt2, norm1_w[l + 1], mods[l + 1][0], mods[l + 1][1])
```

```python
import functools
import math

import jax
import jax.numpy as jnp
from jax import lax
from jax.experimental import pallas as pl
from jax.experimental.pallas import tpu as pltpu

D_MODEL = 2048
CTX_LEN = 256
GRID_W = 64
NORM_EPS = 1e-6

GM_CHUNK = 128
GM_GROUPS = 4
GM_GROUP_CH = 128
GM_WIDTH = GM_GROUPS * GM_GROUP_CH
GDN_HEADS = 4
GDN_HEAD_DIM = 128
GDN_ROWS = 128
GDN_CHUNKS_PER_STEP = 6
GDN_HEADS_PER_STEP = 2
CONV_PAD = 8
RET_HEADS = 4
RET_KEY_DIM = 128
RET_VAL_DIM = 256
RET_ROWS = 256
RET_CHUNKS_PER_STEP = 3
ROPE_BASE = 10000.0
MOE_GROUPS = 4
MOE_EXPERTS_PER_GROUP = 8
MOE_EXPERTS = MOE_GROUPS * MOE_EXPERTS_PER_GROUP
MOE_TOP_K = 2
MOE_HIDDEN = 512

GM_UV_COLS = 2 * GM_WIDTH
GDN_QKV_COLS = 3 * GDN_HEADS * GDN_HEAD_DIM
GDN_AB_COLS = 2 * GDN_HEADS
GDN_GATE_COLS = GDN_HEADS * GDN_HEAD_DIM
RET_QK_COLS = RET_HEADS * RET_KEY_DIM
RET_V_COLS = RET_HEADS * RET_VAL_DIM
MERGE_COLS = 3 * D_MODEL
AB_START = GM_UV_COLS + GDN_QKV_COLS
AB_END = AB_START + 2 * GDN_AB_COLS

C_UV = 0
C_QKV = C_UV + GM_UV_COLS
C_RQ = C_QKV + GDN_QKV_COLS
C_RK = C_RQ + RET_QK_COLS
C_RV = C_RK + RET_QK_COLS
C_GG = C_RV + RET_V_COLS
C_RG = C_GG + GDN_GATE_COLS
C_ML = C_RG + RET_V_COLS
P_COLS = C_ML + MERGE_COLS

V7X_VMEM_LIMIT_BYTES = 56 * 1024 * 1024
MOE_ROWS = 256
MOE_DMA_GROUP = 8
MERGE_ROWS = 256
NORM_ROWS = 256
PROJ_COLS = 512


def _cparams(*sem):
    return pltpu.CompilerParams(dimension_semantics=sem, vmem_limit_bytes=V7X_VMEM_LIMIT_BYTES)


def _normmod_kernel(z_ref, nw_ref, sh_ref, sc_ref, o_ref):
    z = z_ref[0]
    y = z * lax.rsqrt(jnp.mean(z * z, axis=-1, keepdims=True) + NORM_EPS)
    y = y * nw_ref[...]
    o_ref[0] = (y * (1.0 + sc_ref[0, 0]) + sh_ref[0, 0]).astype(o_ref.dtype)


def _normmod(z, norm_w, shift, scale, out_dtype):
    B, LT, D = z.shape
    seg = lambda b, t: (b, jnp.minimum(t, 1), 0, 0)
    assert CTX_LEN == NORM_ROWS
    return pl.pallas_call(
        _normmod_kernel,
        out_shape=jax.ShapeDtypeStruct((B, LT, D), out_dtype),
        grid=(B, LT // NORM_ROWS),
        in_specs=[
            pl.BlockSpec((1, NORM_ROWS, D), lambda b, t: (b, t, 0)),
            pl.BlockSpec((1, D), lambda b, t: (0, 0)),
            pl.BlockSpec((1, 1, 1, D), seg),
            pl.BlockSpec((1, 1, 1, D), seg),
        ],
        out_specs=pl.BlockSpec((1, NORM_ROWS, D), lambda b, t: (b, t, 0)),
        compiler_params=_cparams("parallel", "parallel"),
        name="normmod",
    )(z, norm_w.reshape(1, D), shift, scale)


def _split_cols(ref, x):
    for c in range(x.shape[1] // 128):
        ref[c] = x[:, c * 128:(c + 1) * 128]


def _join_cols(ref):
    return jnp.concatenate([ref[c] for c in range(ref.shape[0])], axis=-1)


def _route_rows(logits):
    neg = jnp.float32(-3.0e38)
    lane = lax.broadcasted_iota(jnp.int32, logits.shape, 1)
    lane_f = lane.astype(jnp.float32)
    first_max = lambda v: jnp.min(jnp.where(v == jnp.max(v, axis=1, keepdims=True), lane_f, 128.0), axis=1,
                                  keepdims=True)
    is_g = lane < MOE_GROUPS
    gl = jnp.where(is_g, logits, neg)
    g_max = jnp.max(gl, axis=1, keepdims=True)
    grp = first_max(gl).astype(jnp.int32)
    p_top = 1.0 / jnp.sum(jnp.where(is_g, jnp.exp(logits - g_max), 0.0), axis=1, keepdims=True)
    e_lane = lane - MOE_GROUPS
    in_grp = (e_lane >= 0) & (e_lane < MOE_EXPERTS) & ((e_lane // MOE_EXPERTS_PER_GROUP) == grp)
    el = jnp.where(in_grp, logits, neg)
    m1 = jnp.max(el, axis=1, keepdims=True)
    i1 = first_max(el)
    el2 = jnp.where(lane_f == i1, neg, el)
    m2 = jnp.max(el2, axis=1, keepdims=True)
    i2 = first_max(el2)
    r = jnp.exp(m2 - m1)
    w1 = p_top / (1.0 + r)
    out = jnp.where(lane == 0, i1 - MOE_GROUPS, 0.0)
    out = jnp.where(lane == 1, i2 - MOE_GROUPS, out)
    out = jnp.where(lane == 2, w1, out)
    return jnp.where(lane == 3, w1 * r, out)


def _out_route_kernel(a_ref, w_ref, x_ref, g_ref, nw_ref, sh_ref, sc_ref, whi_ref, wlo_ref, rb_ref,
                      z_ref, h_ref, rt_ref):
    z = x_ref[0] + g_ref[0, 0] * jnp.dot(a_ref[...], w_ref[...], preferred_element_type=jnp.float32)
    z_ref[0] = z
    y = z * lax.rsqrt(jnp.mean(z * z, axis=-1, keepdims=True) + NORM_EPS)
    h = y * nw_ref[...] * (1.0 + sc_ref[0, 0]) + sh_ref[0, 0]
    _split_cols(h_ref, h)
    hi = h.astype(jnp.bfloat16)
    lo = (h - hi.astype(jnp.float32)).astype(jnp.bfloat16)
    whi = whi_ref[...]
    logits = (jnp.dot(hi, whi, preferred_element_type=jnp.float32)
              + jnp.dot(lo, whi, preferred_element_type=jnp.float32)
              + jnp.dot(hi, wlo_ref[...], preferred_element_type=jnp.float32))
    rt_ref[...] = _route_rows(logits + rb_ref[...])


def _out_proj_route(merged, w_out, z, gate, norm_w, shift, scale, w_router, b_router):
    B, LT, D = z.shape
    nt = LT // NORM_ROWS
    seg = lambda b, t: (b, jnp.minimum(t, 1), 0, 0)
    tile = pl.BlockSpec((1, NORM_ROWS, D), lambda b, t: (b, t, 0))
    const = lambda r, c: pl.BlockSpec((r, c), lambda b, t: (0, 0))
    whi = w_router.astype(jnp.bfloat16)
    wlo = (w_router - whi.astype(jnp.float32)).astype(jnp.bfloat16)
    return pl.pallas_call(
        _out_route_kernel,
        out_shape=[jax.ShapeDtypeStruct((B, LT, D), jnp.float32),
                   jax.ShapeDtypeStruct((D // 128, B * LT, 128), jnp.float32),
                   jax.ShapeDtypeStruct((B * LT, 128), jnp.float32)],
        grid=(B, nt),
        in_specs=[
            pl.BlockSpec((NORM_ROWS, D), lambda b, t: (b * nt + t, 0)),
            const(D, D),
            tile,
            pl.BlockSpec((1, 1, 1, D), seg),
            const(1, D),
            pl.BlockSpec((1, 1, 1, D), seg),
            pl.BlockSpec((1, 1, 1, D), seg),
            const(D, 128),
            const(D, 128),
            const(1, 128),
        ],
        out_specs=[tile,
                   pl.BlockSpec((D // 128, NORM_ROWS, 128), lambda b, t: (0, b * nt + t, 0)),
                   pl.BlockSpec((NORM_ROWS, 128), lambda b, t: (b * nt + t, 0))],
        compiler_params=_cparams("parallel", "parallel"),
        name="out_proj_route",
    )(merged, w_out, z, gate, norm_w.reshape(1, D), shift, scale, whi, wlo, b_router.reshape(1, 128))


def _combine_final_kernel(x_ref, y0_ref, y1_ref, w_ref, g_ref, nw_ref, o_ref):
    w = w_ref[...]
    z = x_ref[0] + g_ref[0, 0] * (w[:, 0:1] * _join_cols(y0_ref) + w[:, 1:2] * _join_cols(y1_ref))
    o_ref[0] = z * lax.rsqrt(jnp.mean(z * z, axis=-1, keepdims=True) + NORM_EPS) * nw_ref[...]


def _combine_final(z, y, weights, gate, norm_w):
    B, LT, D = z.shape
    S = D // 128
    nt = LT // NORM_ROWS
    skip = CTX_LEN // NORM_ROWS
    return pl.pallas_call(
        _combine_final_kernel,
        out_shape=jax.ShapeDtypeStruct((B, LT - CTX_LEN, D), jnp.float32),
        grid=(B, nt - skip),
        in_specs=[
            pl.BlockSpec((1, NORM_ROWS, D), lambda b, t: (b, t + skip, 0)),
            pl.BlockSpec((S, NORM_ROWS, 128), lambda b, t: (0, b * nt + t + skip, 0)),
            pl.BlockSpec((S, NORM_ROWS, 128), lambda b, t: (0, B * nt + b * nt + t + skip, 0)),
            pl.BlockSpec((NORM_ROWS, MOE_TOP_K), lambda b, t: (b * nt + t + skip, 0)),
            pl.BlockSpec((1, 1, 1, D), lambda b, t: (b, 1, 0, 0)),
            pl.BlockSpec((1, D), lambda b, t: (0, 0)),
        ],
        out_specs=pl.BlockSpec((1, NORM_ROWS, D), lambda b, t: (b, t, 0)),
        compiler_params=_cparams("parallel", "parallel"),
        name="combine_final_norm",
    )(z, y, y, weights, gate, norm_w.reshape(1, D))


def _mm_kernel(a_ref, w_ref, o_ref):
    o_ref[...] = jnp.dot(a_ref[...], w_ref[...].astype(jnp.bfloat16), preferred_element_type=jnp.float32)


def _matmul(a, w, layer, tm, tn, name):
    M, K = a.shape
    N = w.shape[2]
    return pl.pallas_call(
        _mm_kernel,
        out_shape=jax.ShapeDtypeStruct((M, N), jnp.float32),
        grid=(M // tm, N // tn),
        in_specs=[
            pl.BlockSpec((tm, K), lambda i, j: (i, 0)),
            pl.BlockSpec((None, K, tn), lambda i, j: (layer, 0, j)),
        ],
        out_specs=pl.BlockSpec((tm, tn), lambda i, j: (i, j)),
        compiler_params=_cparams("parallel", "arbitrary"),
        name=name,
    )(a, w)


def _mm_nt_kernel(a_ref, w_ref, o_ref):
    w = w_ref[0].astype(jnp.bfloat16)
    o_ref[...] = lax.dot_general(a_ref[...], w, (((1,), (1,)), ((), ())), preferred_element_type=jnp.float32)


def _in_proj(a, w_t, layer, tm, tn, n_out, first_row, name):
    M, K = a.shape
    wmap = lambda i, j: (layer, pl.multiple_of(first_row(j), 8), 0)
    return pl.pallas_call(
        _mm_nt_kernel,
        out_shape=jax.ShapeDtypeStruct((M, n_out), jnp.float32),
        grid=(M // tm, n_out // tn),
        in_specs=[
            pl.BlockSpec((tm, K), lambda i, j: (i, 0)),
            pl.BlockSpec((pl.Element(1), pl.Element(tn), pl.Element(K)), wmap),
        ],
        out_specs=pl.BlockSpec((tm, tn), lambda i, j: (i, j)),
        compiler_params=_cparams("parallel", "arbitrary"),
        name=name,
    )(a, w_t)


def _gelu_exact(x):
    return 0.5 * x * (1.0 + lax.erf(x * (1.0 / math.sqrt(2.0))))


def _sigmoid(x):
    return 1.0 / (1.0 + jnp.exp2(x * (-1.0 / math.log(2.0))))


def _merge_kernel(uv_ref, gg_ref, rg_ref, ml_ref, og_ref, or_ref, gmn_ref, ws_ref, bs_ref, gdn_nw_ref,
                  ret_nw_ref, wgm_ref, wgdn_ref, wret_ref, o_ref):
    bf16 = jnp.bfloat16
    f32 = jnp.float32
    rows = uv_ref.shape[0]
    g = _gelu_exact(uv_ref[...])
    u = g[:, :GM_WIDTH]
    v = g[:, GM_WIDTH:]
    d = v - jnp.mean(v, axis=-1, keepdims=True)
    vn = (d * lax.rsqrt(jnp.mean(d * d, axis=-1, keepdims=True) + NORM_EPS) * gmn_ref[...]).astype(bf16)
    chunks = []
    for c in range(rows // GM_CHUNK):
        groups = []
        for gi in range(GM_GROUPS):
            blk = vn[c * GM_CHUNK:(c + 1) * GM_CHUNK, gi * GM_GROUP_CH:(gi + 1) * GM_GROUP_CH]
            groups.append(jnp.dot(ws_ref[gi], blk, preferred_element_type=f32) + bs_ref[gi])
        chunks.append(jnp.concatenate(groups, axis=-1))
    s = jnp.concatenate(chunks, axis=0)
    y_gm = jnp.dot((u * s).astype(bf16), wgm_ref[...], preferred_element_type=f32)

    og = og_ref[...]
    heads = []
    for h in range(GDN_HEADS):
        oh = og[:, h * GDN_HEAD_DIM:(h + 1) * GDN_HEAD_DIM]
        heads.append(oh * lax.rsqrt(jnp.mean(oh * oh, axis=-1, keepdims=True) + NORM_EPS) * gdn_nw_ref[...])
    gg = gg_ref[...]
    a_gdn = (jnp.concatenate(heads, axis=-1) * (gg * _sigmoid(gg))).astype(bf16)
    y_gdn = jnp.dot(a_gdn, wgdn_ref[...], preferred_element_type=f32)

    orr = or_ref[...]
    heads = []
    for h in range(RET_HEADS):
        oh = orr[:, h * RET_VAL_DIM:(h + 1) * RET_VAL_DIM]
        dh = oh - jnp.mean(oh, axis=-1, keepdims=True)
        heads.append(dh * lax.rsqrt(jnp.mean(dh * dh, axis=-1, keepdims=True) + NORM_EPS))
    rg = rg_ref[...]
    a_ret = (jnp.concatenate(heads, axis=-1) * ret_nw_ref[...] * (rg * _sigmoid(rg))).astype(bf16)
    y_ret = jnp.dot(a_ret, wret_ref[...], preferred_element_type=f32)

    sg = _sigmoid(ml_ref[...])
    merged = (sg[:, :D_MODEL] * y_gm + sg[:, D_MODEL:2 * D_MODEL] * y_gdn + sg[:, 2 * D_MODEL:] * y_ret)
    o_ref[...] = merged.astype(o_ref.dtype)


def _merge(p, o_gdn, o_ret, gm_norm_w, ws, bs, gdn_norm_w, ret_norm_w, wgm, wgdn, wret):
    M = p.shape[0]
    tm = MERGE_ROWS
    const = lambda *shape: pl.BlockSpec(shape, lambda i: (0,) * len(shape))
    return pl.pallas_call(
        _merge_kernel,
        out_shape=jax.ShapeDtypeStruct((M, D_MODEL), jnp.bfloat16),
        grid=(M // tm,),
        in_specs=[
            pl.BlockSpec((tm, GM_UV_COLS), lambda i: (i, C_UV // GM_UV_COLS)),
            pl.BlockSpec((tm, GDN_GATE_COLS), lambda i: (i, C_GG // GDN_GATE_COLS)),
            pl.BlockSpec((tm, RET_V_COLS), lambda i: (i, C_RG // RET_V_COLS)),
            pl.BlockSpec((tm, MERGE_COLS), lambda i: (i, C_ML // MERGE_COLS)),
            pl.BlockSpec((tm, GDN_GATE_COLS), lambda i: (i, 0)),
            pl.BlockSpec((tm, RET_V_COLS), lambda i: (i, 0)),
            const(1, GM_WIDTH),
            const(GM_GROUPS, GM_CHUNK, GM_CHUNK),
            const(GM_GROUPS, GM_CHUNK, GM_GROUP_CH),
            const(1, GDN_HEAD_DIM),
            const(1, RET_V_COLS),
            const(GM_WIDTH, D_MODEL),
            const(GDN_GATE_COLS, D_MODEL),
            const(RET_V_COLS, D_MODEL),
        ],
        out_specs=pl.BlockSpec((tm, D_MODEL), lambda i: (i, 0)),
        compiler_params=_cparams("parallel"),
        name="merge",
    )(p, p, p, p, o_gdn, o_ret, gm_norm_w.reshape(1, -1), ws, bs, gdn_norm_w.reshape(1, -1),
      ret_norm_w.reshape(1, -1), wgm, wgdn, wret)


def _moe_kernel(bexp_ref, bcnt_ref, stok_ref, sdst_ref,
                h_hbm, wg_ref, wu_ref, wd_ref, y_hbm,
                xbuf, ybuf, wgb, wub, wdb, gsem, ssem):
    i = pl.program_id(0)
    n = pl.num_programs(0)
    R = MOE_ROWS
    slot = lax.rem(i, 2)

    def gather_copy(blk, r, s):
        return pltpu.make_async_copy(h_hbm.at[:, stok_ref[blk * R + r], :], xbuf.at[s, :, r, :], gsem.at[s])

    def scatter_copy(blk, r, s):
        return pltpu.make_async_copy(ybuf.at[s, :, r, :], y_hbm.at[:, sdst_ref[blk * R + r], :], ssem.at[s])

    def for_rows(blk, fn):
        cnt = bcnt_ref[blk]
        full = lax.shift_right_logical(cnt, MOE_DMA_GROUP.bit_length() - 1)

        def group(g, c):
            for j in range(MOE_DMA_GROUP):
                fn(g * MOE_DMA_GROUP + j, j % 2)
            return c
        lax.fori_loop(0, full, group, 0)
        lax.fori_loop(full * MOE_DMA_GROUP, cnt, lambda r, c: (fn(r, 0), c)[1], 0)

    def start_gather(blk, s):
        for_rows(blk, lambda r, p: gather_copy(blk, r, s).start(priority=p))

    def wait_gather(blk, s):
        for_rows(blk, lambda r, p: gather_copy(blk, r, s).wait())

    def start_scatter(blk, s):
        for_rows(blk, lambda r, p: scatter_copy(blk, r, s).start(priority=p))

    def wait_scatter(blk, s):
        for_rows(blk, lambda r, p: scatter_copy(blk, r, s).wait())

    @pl.when(i == 0)
    def _():
        xbuf[...] = jnp.zeros_like(xbuf)
        start_gather(0, 0)

    @pl.when(i + 1 < n)
    def _():
        start_gather(i + 1, 1 - slot)

    changed = jnp.logical_or(i == 0, bexp_ref[i] != bexp_ref[jnp.maximum(i - 1, 0)])

    @pl.when(changed)
    def _():
        wgb[...] = wg_ref[0].astype(jnp.bfloat16)
        wub[...] = wu_ref[0].astype(jnp.bfloat16)
        wdb[...] = wd_ref[0].astype(jnp.bfloat16)

    wait_gather(i, slot)

    @pl.when(i >= 2)
    def _():
        wait_scatter(i - 2, slot)

    @pl.when(bcnt_ref[i] > 0)
    def _():
        ncol = xbuf.shape[1]
        xb = jnp.concatenate([xbuf[slot, c] for c in range(ncol)], axis=-1).astype(jnp.bfloat16)
        hg = jnp.dot(xb, wgb[...], preferred_element_type=jnp.float32)
        hu = jnp.dot(xb, wub[...], preferred_element_type=jnp.float32)
        act = (hg * _sigmoid(hg)) * hu
        y = jnp.dot(act.astype(jnp.bfloat16), wdb[...], preferred_element_type=jnp.float32)
        for c in range(ncol):
            ybuf[slot, c] = y[:, c * 128:(c + 1) * 128]
        start_scatter(i, slot)

    @pl.when(i == n - 1)
    def _():
        @pl.when(n >= 2)
        def _():
            wait_scatter(i - 1, 1 - slot)
        wait_scatter(i, slot)


def _moe(h, block_expert, block_cnt, slot_tok, slot_dst, w_gate, w_up, w_down):
    S, T, _ = h.shape
    D = S * 128
    n_blocks = block_expert.shape[0]
    R = MOE_ROWS
    wmap = lambda i, be, bc, st, sd: (be[i], 0, 0)
    grid_spec = pltpu.PrefetchScalarGridSpec(
        num_scalar_prefetch=4,
        grid=(n_blocks,),
        in_specs=[
            pl.BlockSpec(memory_space=pl.ANY),
            pl.BlockSpec((1, D, MOE_HIDDEN), wmap),
            pl.BlockSpec((1, D, MOE_HIDDEN), wmap),
            pl.BlockSpec((1, MOE_HIDDEN, D), wmap),
        ],
        out_specs=pl.BlockSpec(memory_space=pl.ANY),
        scratch_shapes=[
            pltpu.VMEM((2, S, R, 128), jnp.float32),
            pltpu.VMEM((2, S, R, 128), jnp.float32),
            pltpu.VMEM((D, MOE_HIDDEN), jnp.bfloat16),
            pltpu.VMEM((D, MOE_HIDDEN), jnp.bfloat16),
            pltpu.VMEM((MOE_HIDDEN, D), jnp.bfloat16),
            pltpu.SemaphoreType.DMA((2,)),
            pltpu.SemaphoreType.DMA((2,)),
        ],
    )
    return pl.pallas_call(
        _moe_kernel,
        out_shape=jax.ShapeDtypeStruct((S, MOE_TOP_K * T, 128), jnp.float32),
        grid_spec=grid_spec,
        compiler_params=_cparams("arbitrary"),
        name="moe_experts",
    )(block_expert, block_cnt, slot_tok, slot_dst, h, w_gate, w_up, w_down)


def _combine_kernel(x_ref, y0_ref, y1_ref, w_ref, g_ref, nw_ref, sh_ref, sc_ref, z_ref, h_ref):
    w = w_ref[...]
    z = x_ref[0] + g_ref[0, 0] * (w[:, 0:1] * _join_cols(y0_ref) + w[:, 1:2] * _join_cols(y1_ref))
    z_ref[0] = z
    y = z * lax.rsqrt(jnp.mean(z * z, axis=-1, keepdims=True) + NORM_EPS)
    h_ref[0] = (y * nw_ref[...] * (1.0 + sc_ref[0, 0]) + sh_ref[0, 0]).astype(h_ref.dtype)


def _combine(z, y, weights, gate, norm_w, shift, scale):
    B, LT, D = z.shape
    S = D // 128
    nt = LT // NORM_ROWS
    seg = pl.BlockSpec((1, 1, 1, D), lambda b, t: (b, jnp.minimum(t, 1), 0, 0))
    tile = pl.BlockSpec((1, NORM_ROWS, D), lambda b, t: (b, t, 0))
    return pl.pallas_call(
        _combine_kernel,
        out_shape=[jax.ShapeDtypeStruct((B, LT, D), jnp.float32), jax.ShapeDtypeStruct((B, LT, D), jnp.bfloat16)],
        grid=(B, nt),
        in_specs=[
            tile,
            pl.BlockSpec((S, NORM_ROWS, 128), lambda b, t: (0, b * nt + t, 0)),
            pl.BlockSpec((S, NORM_ROWS, 128), lambda b, t: (0, B * nt + b * nt + t, 0)),
            pl.BlockSpec((NORM_ROWS, MOE_TOP_K), lambda b, t: (b * nt + t, 0)),
            seg,
            pl.BlockSpec((1, D), lambda b, t: (0, 0)),
            seg,
            seg,
        ],
        out_specs=[tile, tile],
        compiler_params=_cparams("parallel", "parallel"),
        name="moe_combine",
    )(z, y, y, weights, gate, norm_w.reshape(1, D), shift, scale)


def _dispatch_tables(expert_idx):
    T, K = expert_idx.shape
    E, R = MOE_EXPERTS, MOE_ROWS
    A = T * K
    n_blocks = A // R + E
    flat_e = expert_idx.reshape(A).astype(jnp.int32)
    order = jnp.argsort(flat_e).astype(jnp.int32)
    cnt_end = jnp.sum(flat_e[None, :] <= jnp.arange(E, dtype=jnp.int32)[:, None], axis=1, dtype=jnp.int32)
    cnt_start = jnp.concatenate([jnp.zeros((1,), jnp.int32), cnt_end[:-1]])
    counts = cnt_end - cnt_start
    nblk = (counts + R - 1) // R
    blk_end = jnp.cumsum(nblk)
    blk_start = blk_end - nblk
    blk = jnp.arange(n_blocks, dtype=jnp.int32)
    block_expert = jnp.minimum(jnp.sum(blk_end[None, :] <= blk[:, None], axis=1, dtype=jnp.int32), E - 1)
    first = (blk - blk_start[block_expert]) * R
    block_cnt = jnp.clip(counts[block_expert] - first, 0, R).astype(jnp.int32)
    last_used = jnp.max(jnp.where(counts > 0, jnp.arange(E, dtype=jnp.int32), 0))
    block_expert = jnp.where(block_cnt > 0, block_expert, last_used)
    row = jnp.arange(R, dtype=jnp.int32)[None, :]
    src = (cnt_start[block_expert] + first)[:, None] + row
    a = order[jnp.clip(src, 0, A - 1)]
    valid = row < block_cnt[:, None]
    slot_tok = jnp.where(valid, a // K, 0).reshape(-1)
    slot_dst = jnp.where(valid, (a % K) * T + a // K, 0).reshape(-1)
    return block_expert, block_cnt, slot_tok, slot_dst


def _bdot(a, b):
    return jnp.dot(a.astype(jnp.bfloat16), b.astype(jnp.bfloat16), preferred_element_type=jnp.float32)


def _bdot_nt(a, b):
    return lax.dot_general(a.astype(jnp.bfloat16), b.astype(jnp.bfloat16), (((1,), (1,)), ((), ())),
                           preferred_element_type=jnp.float32)


def _bdot_tn(a, b):
    return lax.dot_general(a.astype(jnp.bfloat16), b.astype(jnp.bfloat16), (((0,), (0,)), ((), ())),
                           preferred_element_type=jnp.float32)


def _conv_silu(x_ref, col, cw, xp_s):
    LT = x_ref.shape[1]
    K = cw.shape[0]
    r = K // 2
    starts = (CONV_PAD, 2 * CONV_PAD + CTX_LEN)
    xp_s[pl.ds(starts[0], CTX_LEN), :] = x_ref[0, pl.ds(0, CTX_LEN), col:col + 128]
    xp_s[pl.ds(starts[1], LT - CTX_LEN), :] = x_ref[0, pl.ds(CTX_LEN, LT - CTX_LEN), col:col + 128]
    outs = []
    for start, n in zip(starts, (CTX_LEN, LT - CTX_LEN)):
        acc = xp_s[pl.ds(start - r, n), :] * cw[0:1, :]
        for j in range(1, K):
            acc = acc + xp_s[pl.ds(start - r + j, n), :] * cw[j:j + 1, :]
        outs.append(acc * _sigmoid(acc))
    return jnp.concatenate(outs, axis=0)


def _l2n(t):
    return t * lax.rsqrt(jnp.sum(t * t, axis=-1, keepdims=True) + NORM_EPS)


def _lane_col(x, idx):
    lane = lax.broadcasted_iota(jnp.int32, x.shape, 1)
    return jnp.sum(jnp.where(lane == idx, x, 0.0), axis=1, keepdims=True)


def _inv_unit_tri(Ns, eye, m16, moffs):
    Dg = [N * m16 for N in Ns]
    D2 = [_bdot(d, d) for d in Dg]
    D4 = [_bdot(d, d) for d in D2]
    D8 = [_bdot(d, d) for d in D4]
    T = [eye - d for d in Dg]
    for P in (D2, D4, D8):
        T = [t + _bdot(t, p) for t, p in zip(T, P)]
    for m in moffs:
        X = [_bdot(t, N * m) for t, N in zip(T, Ns)]
        T = [t - _bdot(x, t) for t, x in zip(T, X)]
    return T


def _gdn_kernel(q_ref, k_ref, v_ref, ab_ref, cwq_ref, cwk_ref, cwv_ref, alog_ref, dtb_ref, o_ref,
                qn_s, kn_s, vn_s, gb_s, gall_s, xp_s, u_s, w_s, qs_s, kst_s, qkd_s, el_s, o_s):
    f32 = jnp.float32
    bf16 = jnp.bfloat16
    HP = GDN_HEADS_PER_STEP
    Dh = GDN_HEAD_DIM
    hp = pl.program_id(1)
    LT = q_ref.shape[1]
    C = GDN_ROWS
    nch = LT // C
    nctx = CTX_LEN // C

    lane = lax.broadcasted_iota(jnp.int32, (LT, 128), 1)

    @pl.when(hp == 0)
    def _():
        ab = ab_ref[0]
        xa = ab + dtb_ref[...]
        g_all = -jnp.exp(alog_ref[...]) * (jnp.maximum(xa, 0.0) + jnp.log1p(jnp.exp(-jnp.abs(xa))))
        gall_s[...] = jnp.where(lane < 2 * GDN_HEADS, g_all, _sigmoid(ab))

    for start in (0, CONV_PAD + CTX_LEN, 2 * CONV_PAD + LT):
        xp_s[pl.ds(start, CONV_PAD), :] = jnp.zeros((CONV_PAD, 128), f32)

    gb_all = gall_s[...]
    for hh in range(HP):
        col = hh * Dh
        qn_s[hh] = _l2n(_conv_silu(q_ref, col, cwq_ref[:, col:col + Dh], xp_s)) * (Dh ** -0.5)
        kn_s[hh] = _l2n(_conv_silu(k_ref, col, cwk_ref[:, col:col + Dh], xp_s))
        vn_s[hh] = _conv_silu(v_ref, col, cwv_ref[:, col:col + Dh], xp_s)
        head = hp * HP + hh
        gb = jnp.where(lane == 0, _lane_col(gb_all, head), 0.0)
        for j in range(1, 4):
            gb = jnp.where(lane == j, _lane_col(gb_all, j * GDN_HEADS + head), gb)
        gb_s[hh] = gb

    ri = lax.broadcasted_iota(jnp.int32, (C, C), 0)
    ci = lax.broadcasted_iota(jnp.int32, (C, C), 1)
    eye = (ri == ci).astype(f32)
    low = (ri >= ci).astype(f32)
    upp = (ri <= ci).astype(f32)
    m16 = ((ri // 16) == (ci // 16)).astype(f32)
    moffs = [(((ri // (2 * s)) == (ci // (2 * s))) & ((ri // s) != (ci // s))).astype(f32) for s in (16, 32, 64)]
    hi = lax.Precision.HIGHEST

    def make_chunk_group(hh):
        def chunk_group(i, carry):
            probs = []
            for j in range(GDN_CHUNKS_PER_STEP):
                c = i * GDN_CHUNKS_PER_STEP + j
                rows = pl.ds(pl.multiple_of(c * C, C), C)
                q = qn_s[hh, rows, :]
                k = kn_s[hh, rows, :]
                v = vn_s[hh, rows, :]
                gbc = gb_s[hh, rows, :]
                kq = _bdot_nt(jnp.concatenate([k, q], axis=0), k)
                kk, qk = kq[:C], kq[C:]
                cs_f = jnp.dot(low, gbc, precision=hi, preferred_element_type=f32)
                cs_b = jnp.dot(upp, gbc, precision=hi, preferred_element_type=f32)
                for d in range(2):
                    cs = cs_f if d == 0 else cs_b
                    gc = cs[:, d:d + 1]
                    gc_row = cs.T[d:d + 1, :]
                    beta = gbc[:, 2 + d:3 + d]
                    incl = low if d == 0 else upp
                    decay = jnp.exp(jnp.where(incl > 0, gc - gc_row, -jnp.inf))
                    g_last = gc[C - 1:C, :] if d == 0 else gc[0:1, :]
                    probs.append(dict(d=d, c=c, q=q, k=k, v=v, qk=qk, gc=gc, beta=beta, decay=decay,
                                      g_last=g_last, N=(beta * kk) * decay * (1.0 - eye)))
            Ts = _inv_unit_tri([p["N"] for p in probs], eye, m16, moffs)
            for p, T in zip(probs, Ts):
                d, c, gc, beta = p["d"], p["c"], p["gc"], p["beta"]
                eg = jnp.exp(gc)
                uw = _bdot(T, jnp.concatenate([p["v"] * beta, p["k"] * beta * eg], axis=1))
                u_s[hh, d, c] = uw[:, :Dh]
                w_s[hh, d, c] = uw[:, Dh:].astype(bf16)
                qs_s[hh, d, c] = (p["q"] * eg).astype(bf16)
                kst_s[hh, d, c] = (p["k"] * jnp.exp(p["g_last"] - gc)).T.astype(bf16)
                qkd_s[hh, d, c] = (p["qk"] * p["decay"]).astype(bf16)
                el_s[hh, d, c] = jnp.broadcast_to(jnp.exp(p["g_last"]), (8, 128))
            return carry
        return chunk_group

    for hh in range(HP):
        lax.fori_loop(0, nch // GDN_CHUNKS_PER_STEP, make_chunk_group(hh), 0)

    def step(t, S):
        cb = jnp.where(t < nctx, nctx - 1 - t, nch - 1 + nctx - t)
        idx = [(hh, d, c) for hh in range(HP) for d, c in ((0, t), (1, cb))]
        mm = lambda a, b: jnp.dot(a, b, preferred_element_type=f32)
        Sbf = [Sd.astype(bf16) for Sd in S]
        wS = [mm(w_s[i], sb) for i, sb in zip(idx, Sbf)]
        vb = [(u_s[i] - x).astype(bf16) for i, x in zip(idx, wS)]
        upd = [mm(kst_s[i], v) for i, v in zip(idx, vb)]
        o_in = [mm(qkd_s[i], v) for i, v in zip(idx, vb)]
        o_st = [mm(qs_s[i], sb) for i, sb in zip(idx, Sbf)]
        for i, a, b in zip(idx, o_st, o_in):
            o_s[i] = a + b
        return tuple(Sd * el_s[i][0:1, :] + x for i, Sd, x in zip(idx, S, upd))

    z = jnp.zeros((Dh, Dh), f32)
    lax.fori_loop(0, nch, step, (z,) * (2 * HP))
    for hh in range(HP):
        o_ref[0, :, hh * Dh:(hh + 1) * Dh] = (o_s[hh, 0] + o_s[hh, 1]).reshape(LT, Dh)


def _gdn_mixer(p3, ab, conv_w, a_log, dt_bias):
    B, LT, _ = p3.shape
    H, Dh, C, HP = GDN_HEADS, GDN_HEAD_DIM, GDN_ROWS, GDN_HEADS_PER_STEP
    W = HP * Dh
    nch = LT // C
    qb = C_QKV // W
    alog_row = jnp.zeros((1, 128), jnp.float32).at[0, :2 * H].set(a_log.reshape(-1))
    dtb_row = jnp.zeros((1, 128), jnp.float32).at[0, :2 * H].set(dt_bias.reshape(-1))
    taps = conv_w.shape[0]
    nstep = H // HP
    per_head = lambda dt: pltpu.VMEM((HP, LT, Dh), dt)
    per_chunk = lambda dt: pltpu.VMEM((HP, 2, nch, C, Dh), dt)
    once = pl.Buffered(1)
    return pl.pallas_call(
        _gdn_kernel,
        out_shape=jax.ShapeDtypeStruct((B, LT, H * Dh), jnp.float32),
        grid=(B, nstep),
        in_specs=[
            pl.BlockSpec((1, LT, W), lambda b, h: (b, 0, qb + h), pipeline_mode=once),
            pl.BlockSpec((1, LT, W), lambda b, h: (b, 0, qb + nstep + h), pipeline_mode=once),
            pl.BlockSpec((1, LT, W), lambda b, h: (b, 0, qb + 2 * nstep + h), pipeline_mode=once),
            pl.BlockSpec((1, LT, 128), lambda b, h: (b, 0, 0)),
            pl.BlockSpec((taps, W), lambda b, h: (0, h)),
            pl.BlockSpec((taps, W), lambda b, h: (0, nstep + h)),
            pl.BlockSpec((taps, W), lambda b, h: (0, 2 * nstep + h)),
            pl.BlockSpec((1, 128), lambda b, h: (0, 0)),
            pl.BlockSpec((1, 128), lambda b, h: (0, 0)),
        ],
        out_specs=pl.BlockSpec((1, LT, W), lambda b, h: (b, 0, h)),
        scratch_shapes=[
            per_head(jnp.float32), per_head(jnp.float32), per_head(jnp.float32), per_head(jnp.float32),
            pltpu.VMEM((LT, 128), jnp.float32),
            pltpu.VMEM((LT + 3 * CONV_PAD, 128), jnp.float32),
            per_chunk(jnp.float32), per_chunk(jnp.bfloat16), per_chunk(jnp.bfloat16), per_chunk(jnp.bfloat16),
            per_chunk(jnp.bfloat16),
            pltpu.VMEM((HP, 2, nch, 8, 128), jnp.float32),
            per_chunk(jnp.float32),
        ],
        compiler_params=_cparams("parallel", "arbitrary"),
        name="gdn_mixer",
    )(p3, p3, p3, ab, conv_w, conv_w, conv_w, alog_row, dtb_row)


def _ret_kernel(lg_ref, q_ref, k_ref, v_ref, cos_ref, sin_ref, o_ref, qr_s, kr_s, st_s):
    f32 = jnp.float32
    h = pl.program_id(1)
    LT = q_ref.shape[1]
    C = RET_ROWS
    nch = LT // C
    nctx = CTX_LEN // C
    cos = cos_ref[...]
    sin = sin_ref[...]
    half = RET_KEY_DIM // 2
    q = q_ref[0]
    k = k_ref[0]
    qr_s[...] = (q * cos + pltpu.roll(q, half, axis=1) * sin) * (RET_KEY_DIM ** -0.5)
    kr_s[...] = k * cos + pltpu.roll(k, half, axis=1) * sin

    lg_f = lg_ref[0, h]
    lg_b = lg_ref[1, h]
    ri = lax.broadcasted_iota(jnp.int32, (C, C), 0)
    ci = lax.broadcasted_iota(jnp.int32, (C, C), 1)
    diff = (ri - ci).astype(f32)
    dm = (jnp.exp(jnp.where(diff >= 0, lg_f * diff, -jnp.inf))
          + jnp.exp(jnp.where(diff <= 0, -lg_b * diff, -jnp.inf)))
    pcol = lax.broadcasted_iota(jnp.int32, (C, 1), 0).astype(f32)
    cross_f = jnp.exp(lg_f * (pcol + 1.0))
    in_f = jnp.exp(lg_f * (C - 1.0 - pcol))
    cross_b = jnp.exp(lg_b * (C - pcol))
    in_b = jnp.exp(lg_b * pcol)
    cd_f = jnp.exp(lg_f * C)
    cd_b = jnp.exp(lg_b * C)

    G = RET_CHUNKS_PER_STEP
    chunk_rows = lambda i, j: pl.ds(pl.multiple_of((i * G + j) * C, C), C)

    def local(i, carry):
        rs = [chunk_rows(i, j) for j in range(G)]
        qs = [qr_s[r, :] for r in rs]
        ks = [kr_s[r, :] for r in rs]
        vs = [v_ref[0, r, :] for r in rs]
        ss = [_bdot_nt(q, k) for q, k in zip(qs, ks)]
        os_ = [_bdot(s * dm, v) for s, v in zip(ss, vs)]
        mf = [_bdot_tn(k * in_f, v) for k, v in zip(ks, vs)]
        mb = [_bdot_tn(k * in_b, v) for k, v in zip(ks, vs)]
        for j in range(G):
            o_ref[0, rs[j], :] = os_[j]
            st_s[0, i * G + j] = mf[j]
            st_s[1, i * G + j] = mb[j]
        return carry

    lax.fori_loop(0, nch // G, local, 0)

    fwd_order = list(range(nch))
    bwd_order = list(range(nctx - 1, -1, -1)) + list(range(nch - 1, nctx - 1, -1))
    for d, order, cd in ((0, fwd_order, cd_f), (1, bwd_order, cd_b)):
        S = jnp.zeros((RET_KEY_DIM, RET_VAL_DIM), f32)
        for c in order:
            contrib = st_s[d, c]
            st_s[d, c] = S
            S = S * cd + contrib

    def cross(i, carry):
        rs = [chunk_rows(i, j) for j in range(G)]
        qs = [qr_s[r, :] for r in rs]
        cf = [_bdot(q, st_s[0, i * G + j]) for j, q in enumerate(qs)]
        cb = [_bdot(q, st_s[1, i * G + j]) for j, q in enumerate(qs)]
        for j in range(G):
            o_ref[0, rs[j], :] += cf[j] * cross_f + cb[j] * cross_b
        return carry

    lax.fori_loop(0, nch // G, cross, 0)


def _ret_mixer(p3, log_gamma, cos_t, sin_t):
    B, LT, _ = p3.shape
    H, Dk, Dv = RET_HEADS, RET_KEY_DIM, RET_VAL_DIM
    grid_spec = pltpu.PrefetchScalarGridSpec(
        num_scalar_prefetch=1,
        grid=(B, H),
        in_specs=[
            pl.BlockSpec((1, LT, Dk), lambda b, h, lg: (b, 0, C_RQ // Dk + h)),
            pl.BlockSpec((1, LT, Dk), lambda b, h, lg: (b, 0, C_RK // Dk + h)),
            pl.BlockSpec((1, LT, Dv), lambda b, h, lg: (b, 0, C_RV // Dv + h)),
            pl.BlockSpec((LT, Dk), lambda b, h, lg: (0, 0)),
            pl.BlockSpec((LT, Dk), lambda b, h, lg: (0, 0)),
        ],
        out_specs=pl.BlockSpec((1, LT, Dv), lambda b, h, lg: (b, 0, h)),
        scratch_shapes=[pltpu.VMEM((LT, Dk), jnp.float32), pltpu.VMEM((LT, Dk), jnp.float32),
                        pltpu.VMEM((2, LT // RET_ROWS, Dk, Dv), jnp.float32)],
    )
    return pl.pallas_call(
        _ret_kernel,
        out_shape=jax.ShapeDtypeStruct((B, LT, H * Dv), jnp.float32),
        grid_spec=grid_spec,
        compiler_params=_cparams("parallel", "parallel"),
        name="ret_mixer",
    )(log_gamma, p3, p3, p3, cos_t, sin_t)


def _axial_rope(rows, dim):
    n = dim // 4
    freq = ROPE_BASE ** (-jnp.arange(n, dtype=jnp.float32) / n)
    row = jnp.repeat(jnp.arange(rows, dtype=jnp.float32), GRID_W)
    col = (jnp.arange(rows * GRID_W) % GRID_W).astype(jnp.float32)
    ang = jnp.concatenate([row[:, None] * freq, col[:, None] * freq], axis=-1)
    ang = jnp.concatenate([ang, ang], axis=-1)[:, None, :]
    return jnp.cos(ang), jnp.sin(ang)


def _rope_tables(L):
    cos, sin = _axial_rope(L // GRID_W, RET_KEY_DIM)
    half = RET_KEY_DIM // 2
    sign = jnp.concatenate([-jnp.ones((half,), jnp.float32), jnp.ones((half,), jnp.float32)])
    cos_t = jnp.concatenate([jnp.ones((CTX_LEN, RET_KEY_DIM), jnp.float32), cos[:, 0, :]], axis=0)
    sin_t = jnp.concatenate([jnp.zeros((CTX_LEN, RET_KEY_DIM), jnp.float32), sin[:, 0, :] * sign], axis=0)
    return cos_t, sin_t


def _seg_table(ctx_vec, lat_vec):
    B, D = lat_vec.shape
    return jnp.stack([jnp.broadcast_to(ctx_vec, (B, D)), lat_vec], axis=1).reshape(B, 2, 1, D)


def kernel(x, c, ctx, c_ctx, mod_w, mod_b, norm1_w, w_in, gm_norm_w, gm_spatial_w, gm_spatial_b, gdn_conv_w, gdn_a_log, gdn_dt_bias, gdn_norm_w, ret_decay_logit, ret_norm_w, w_br_gm, w_br_gdn, w_br_ret, w_out, norm2_w, router_group_w, router_group_b, router_expert_w, router_expert_b, moe_w_gate, moe_w_up, moe_w_down, final_norm_w):
    B, L, D = x.shape
    depth = mod_w.shape[0]
    LT = CTX_LEN + L
    T = B * LT
    bf16 = jnp.bfloat16
    cos_t, sin_t = _rope_tables(L)

    w_in_t = jnp.swapaxes(w_in, 1, 2)
    wgm_b, wgdn_b, wret_b, wout_b = (t.astype(bf16) for t in (w_br_gm, w_br_gdn, w_br_ret, w_out))
    ws_b = gm_spatial_w.astype(bf16)
    bs_full = jnp.broadcast_to(gm_spatial_b[..., None], gm_spatial_b.shape + (GM_GROUP_CH,))
    pad = 128 - MOE_GROUPS - MOE_EXPERTS
    w_router = jnp.concatenate([router_group_w, router_expert_w, jnp.zeros((depth, D, pad), jnp.float32)], axis=-1)
    b_router = jnp.concatenate([router_group_b, router_expert_b, jnp.zeros((depth, pad), jnp.float32)], axis=-1)
    moe_wg, moe_wu = (t.reshape(depth * MOE_EXPERTS, D, MOE_HIDDEN) for t in (moe_w_gate, moe_w_up))
    moe_wd = moe_w_down.reshape(depth * MOE_EXPERTS, MOE_HIDDEN, D)
    main_rows = lambda j: j * PROJ_COLS + jnp.where(j * PROJ_COLS < AB_START, 0, AB_END - AB_START)

    cc = jnp.concatenate([jax.nn.silu(c), jax.nn.silu(c_ctx)[None], jnp.zeros((16 - B - 1, D), c.dtype)], axis=0)
    cc = cc.astype(bf16)
    mods = []
    for l in range(depth):
        mod = _matmul(cc, mod_w, l, 16, 1024, "adaln_mod") + mod_b[l]
        mods.append([_seg_table(mod[B, i * D:(i + 1) * D], mod[:B, i * D:(i + 1) * D]) for i in range(6)])

    z = jnp.concatenate([ctx, x], axis=1)
    h = _normmod(z, norm1_w[0], mods[0][0], mods[0][1], bf16)
    for l in range(depth):
        sh1, sc1, gt1, sh2, sc2, gt2 = mods[l]
        log_gamma = jax.nn.log_sigmoid(ret_decay_logit[l])

        h = h.reshape(T, D)
        p = _in_proj(h, w_in_t, l, LT, PROJ_COLS, P_COLS, main_rows, "in_proj")
        pab = _in_proj(h, w_in_t, l, LT, 128, 128, lambda j: j * 0 + AB_START, "in_proj_ab")
        p3 = p.reshape(B, LT, P_COLS)
        o_gdn = _gdn_mixer(p3, pab.reshape(B, LT, 128), gdn_conv_w[l], gdn_a_log[l], gdn_dt_bias[l])
        o_ret = _ret_mixer(p3, log_gamma, cos_t, sin_t)
        o_gdn = o_gdn.reshape(T, GDN_GATE_COLS)
        o_ret = o_ret.reshape(T, RET_V_COLS)

        merged = _merge(p, o_gdn, o_ret, gm_norm_w[l], ws_b[l], bs_full[l], gdn_norm_w[l], ret_norm_w[l],
                        wgm_b[l], wgdn_b[l], wret_b[l])
        z, h2, route = _out_proj_route(merged, wout_b[l], z, gt1, norm2_w[l], sh2, sc2, w_router[l], b_router[l])
        expert_idx = route[:, :MOE_TOP_K].astype(jnp.int32)
        weights = route[:, MOE_TOP_K:2 * MOE_TOP_K]
        block_expert, block_cnt, slot_tok, slot_dst = _dispatch_tables(expert_idx)
        y = _moe(h2, block_expert + l * MOE_EXPERTS, block_cnt, slot_tok, slot_dst, moe_wg, moe_wu, moe_wd)
        if l + 1 == depth:
            return _combine_final(z, y, weights, gt2, final_norm_w)
        z, h = _combine(z, y, weights, gt2, norm1_w[l + 1], mods[l + 1][0], mods[l + 1][1])
```
